```python
import jax, jax.numpy as jnp
from jax import lax
import numpy as np

D_MODEL = 1024
BATCH = 2
SEQ = 8192
DEPTH = 4
DEC_BATCH = 4
DEC_SEQ = 4096
PAST_LEN = 128

N_META = 16
N_HEADS = 16
N_KV_HEADS = 4
HEAD_DIM = 64
GQA_GROUP = N_HEADS // N_KV_HEADS
ATTN_WIDTH = N_HEADS * HEAD_DIM
KV_WIDTH = N_KV_HEADS * HEAD_DIM
WINDOW = 128
BLOCK = 128
POOL_WINDOWS = (2, 4, 8, 16)
POOL_GROUPS = len(POOL_WINDOWS)
POOL_GROUP_WIDTH = 128
POOL_WIDTH = POOL_GROUPS * POOL_GROUP_WIDTH
N_BRANCHES = 2
IN_WIDTH = ATTN_WIDTH + 2 * KV_WIDTH + POOL_WIDTH + N_BRANCHES * D_MODEL
D_FF = 4 * D_MODEL
LN_EPS = 1e-5
DEEPNORM_ALPHA = float((2 * DEPTH) ** 0.25)
DEEPNORM_BETA = float((8 * DEPTH) ** -0.25)

kernel_name = "hybrid_swa_pool_deepnorm_encoder"

F32 = jnp.float32


def layer_norm(x, g, b):
    xf = x.astype(F32)
    mu = jnp.mean(xf, axis=-1, keepdims=True)
    var = jnp.mean(jnp.square(xf - mu), axis=-1, keepdims=True)
    y = (xf - mu) * lax.rsqrt(var + LN_EPS)
    return (y * g.astype(F32) + b.astype(F32)).astype(x.dtype)


def alibi_slopes():
    return jnp.asarray(2.0 ** (-8.0 * np.arange(1, N_HEADS + 1) / N_HEADS), dtype=F32)


def windowed_gqa(q, k, v, sink):
    B, L = q.shape[0], q.shape[1]
    S = L - N_META
    nb = S // BLOCK
    scale = HEAD_DIM ** -0.5
    slopes = alibi_slopes().reshape(N_KV_HEADS, GQA_GROUP)
    sink_kg = sink.astype(F32).reshape(N_KV_HEADS, GQA_GROUP)
    qm, qr = q[:, :N_META], q[:, N_META:]
    km, kr = k[:, :N_META], k[:, N_META:]
    vm, vr = v[:, :N_META], v[:, N_META:]

    qb = qr.reshape(B, nb, BLOCK, N_KV_HEADS, GQA_GROUP, HEAD_DIM)
    pad = ((0, 0), (BLOCK, BLOCK), (0, 0), (0, 0))
    kp = jnp.pad(kr, pad).reshape(B, nb + 2, BLOCK, N_KV_HEADS, HEAD_DIM)
    vp = jnp.pad(vr, pad).reshape(B, nb + 2, BLOCK, N_KV_HEADS, HEAD_DIM)
    kband = jnp.concatenate([kp[:, :-2], kp[:, 1:-1], kp[:, 2:]], axis=2)
    vband = jnp.concatenate([vp[:, :-2], vp[:, 1:-1], vp[:, 2:]], axis=2)
    s_band = jnp.einsum('bnqkgd,bnskd->bnkgqs', qb, kband, preferred_element_type=F32) * scale
    qi = jnp.arange(BLOCK)
    si = jnp.arange(3 * BLOCK)
    rel = si[None, :] - BLOCK - qi[:, None]
    key_idx = jnp.arange(nb)[:, None] * BLOCK - BLOCK + si[None, :]
    valid = (jnp.abs(rel) <= WINDOW)[None] & ((key_idx >= 0) & (key_idx < S))[:, None, :]
    bias = -slopes[:, :, None, None] * jnp.abs(rel).astype(F32)
    s_band = jnp.where(valid[None, :, None, None], s_band + bias[None, None], -jnp.inf)
    s_meta = jnp.einsum('bnqkgd,bmkd->bnkgqm', qb, km, preferred_element_type=F32) * scale
    sink_r = jnp.broadcast_to(sink_kg[None, None, :, :, None, None],
                              (B, nb, N_KV_HEADS, GQA_GROUP, BLOCK, 1))
    p = jax.nn.softmax(jnp.concatenate([s_meta, s_band, sink_r], axis=-1), axis=-1)
    pm = p[..., :N_META].astype(v.dtype)
    pb = p[..., N_META:N_META + 3 * BLOCK].astype(v.dtype)
    out_r = (jnp.einsum('bnkgqm,bmkd->bnqkgd', pm, vm)
             + jnp.einsum('bnkgqs,bnskd->bnqkgd', pb, vband)).reshape(B, S, ATTN_WIDTH)

    qmg = qm.reshape(B, N_META, N_KV_HEADS, GQA_GROUP, HEAD_DIM)
    kf, vf = kr[:, :BLOCK], vr[:, :BLOCK]
    s_mm = jnp.einsum('bqkgd,bmkd->bkgqm', qmg, km, preferred_element_type=F32) * scale
    s_mr = jnp.einsum('bqkgd,bskd->bkgqs', qmg, kf, preferred_element_type=F32) * scale
    dist = (N_META + jnp.arange(BLOCK))[None, :] - jnp.arange(N_META)[:, None]
    s_mr = jnp.where((dist <= WINDOW)[None, None, None],
                     s_mr - slopes[None, :, :, None, None] * dist.astype(F32)[None, None, None], -jnp.inf)
    sink_m = jnp.broadcast_to(sink_kg[None, :, :, None, None], (B, N_KV_HEADS, GQA_GROUP, N_META, 1))
    pmq = jax.nn.softmax(jnp.concatenate([s_mm, s_mr, sink_m], axis=-1), axis=-1)
    out_m = (jnp.einsum('bkgqm,bmkd->bqkgd', pmq[..., :N_META].astype(v.dtype), vm)
             + jnp.einsum('bkgqs,bskd->bqkgd', pmq[..., N_META:N_META + BLOCK].astype(v.dtype), vf)
             ).reshape(B, N_META, ATTN_WIDTH)
    return jnp.concatenate([out_m, out_r], axis=1)


def multiscale_pool(u, w_grp, pool_scale):
    B, L, _ = u.shape
    uf = u.astype(F32)
    cs = jnp.pad(jnp.cumsum(uf, axis=1), ((0, 0), (1, 0), (0, 0)))
    t = jnp.arange(L)
    diffs = []
    for g, w in enumerate(POOL_WINDOWS):
        lo = jnp.clip(t - w // 2, 0, L)
        hi = jnp.clip(t + w // 2, 0, L)
        sl = slice(g * POOL_GROUP_WIDTH, (g + 1) * POOL_GROUP_WIDTH)
        csg = cs[..., sl]
        mean = (jnp.take(csg, hi, axis=1) - jnp.take(csg, lo, axis=1)) / (hi - lo).astype(F32)[None, :, None]
        diffs.append(mean - uf[..., sl])
    d = jnp.stack(diffs, axis=2).astype(u.dtype)
    y = jnp.einsum('blgc,gcd->blgd', d, w_grp).reshape(B, L, POOL_WIDTH)
    return y * pool_scale


def encoder_layer(x, w_in, sink, w_pool, pool_scale, w_bo_attn, w_bo_pool, w_out, ln1_g, ln1_b,
                  w_mlp1, b_mlp1, w_mlp2, b_mlp2, ln2_g, ln2_b):
    B, L, _ = x.shape
    proj = x @ w_in
    q, k, v, u, gates = jnp.split(
        proj, [ATTN_WIDTH, ATTN_WIDTH + KV_WIDTH, ATTN_WIDTH + 2 * KV_WIDTH,
               ATTN_WIDTH + 2 * KV_WIDTH + POOL_WIDTH], axis=-1)
    q = q.reshape(B, L, N_HEADS, HEAD_DIM)
    k = k.reshape(B, L, N_KV_HEADS, HEAD_DIM)
    v = v.reshape(B, L, N_KV_HEADS, HEAD_DIM)
    ya = windowed_gqa(q, k, v, sink) @ w_bo_attn
    yb = multiscale_pool(u, w_pool, pool_scale) @ w_bo_pool
    g = jax.nn.sigmoid(gates.astype(F32)).astype(x.dtype)
    mixed = (g[..., :D_MODEL] * ya + g[..., D_MODEL:] * yb) @ w_out
    x = layer_norm(DEEPNORM_ALPHA * x + mixed, ln1_g, ln1_b)
    h = jnp.square(jax.nn.relu(x @ w_mlp1 + b_mlp1))
    return layer_norm(DEEPNORM_ALPHA * x + (h @ w_mlp2 + b_mlp2), ln2_g, ln2_b)


def encode(x, meta_tokens, ln_emb_g, ln_emb_b, layer_params):
    B = x.shape[0]
    meta = jnp.broadcast_to(meta_tokens.astype(x.dtype)[None], (B, N_META, D_MODEL))
    h = layer_norm(jnp.concatenate([meta, x], axis=1), ln_emb_g, ln_emb_b)
    for l in range(DEPTH):
        h = encoder_layer(h, *[p[l] for p in layer_params])
    return h[:, N_META:]


def setup_inputs(seed: int = 0) -> dict:
    key = jax.random.key(seed)
    ks = jax.random.split(key, 24)
    nrm = lambda k, shape, s: jax.random.normal(k, shape, dtype=F32) * s
    return {
        "x_prompt": nrm(ks[0], (BATCH, SEQ, D_MODEL), 1.0),
        "x_sample": nrm(ks[1], (DEC_BATCH, DEC_SEQ, D_MODEL), 1.0),
        "meta_tokens": nrm(ks[2], (N_META, D_MODEL), 1.0),
        "ln_emb_g": 1.0 + nrm(ks[3], (D_MODEL,), 0.02),
        "ln_emb_b": nrm(ks[4], (D_MODEL,), 0.02),
        "w_in": nrm(ks[5], (DEPTH, D_MODEL, IN_WIDTH), D_MODEL ** -0.5),
        "sink": nrm(ks[6], (DEPTH, N_HEADS), 0.5),
        "w_pool": nrm(ks[7], (DEPTH, POOL_GROUPS, POOL_GROUP_WIDTH, POOL_GROUP_WIDTH), POOL_GROUP_WIDTH ** -0.5),
        "pool_scale": 1.0 + nrm(ks[8], (DEPTH, POOL_WIDTH), 0.02),
        "w_bo_attn": nrm(ks[9], (DEPTH, ATTN_WIDTH, D_MODEL), ATTN_WIDTH ** -0.5),
        "w_bo_pool": nrm(ks[10], (DEPTH, POOL_WIDTH, D_MODEL), POOL_WIDTH ** -0.5),
        "w_out": nrm(ks[11], (DEPTH, D_MODEL, D_MODEL), DEEPNORM_BETA * D_MODEL ** -0.5),
        "ln1_g": 1.0 + nrm(ks[12], (DEPTH, D_MODEL), 0.02),
        "ln1_b": nrm(ks[13], (DEPTH, D_MODEL), 0.02),
        "w_mlp1": nrm(ks[14], (DEPTH, D_MODEL, D_FF), D_MODEL ** -0.5),
        "b_mlp1": nrm(ks[15], (DEPTH, D_FF), 0.02),
        "w_mlp2": nrm(ks[16], (DEPTH, D_FF, D_MODEL), DEEPNORM_BETA * D_FF ** -0.5),
        "b_mlp2": nrm(ks[17], (DEPTH, D_MODEL), 0.02),
        "ln2_g": 1.0 + nrm(ks[18], (DEPTH, D_MODEL), 0.02),
        "ln2_b": nrm(ks[19], (DEPTH, D_MODEL), 0.02),
    }


def reference(x_prompt, x_sample, meta_tokens, ln_emb_g, ln_emb_b, w_in, sink, w_pool, pool_scale,
              w_bo_attn, w_bo_pool, w_out, ln1_g, ln1_b, w_mlp1, b_mlp1, w_mlp2, b_mlp2, ln2_g, ln2_b):
    layer_params = (w_in, sink, w_pool, pool_scale, w_bo_attn, w_bo_pool, w_out, ln1_g, ln1_b,
                    w_mlp1, b_mlp1, w_mlp2, b_mlp2, ln2_g, ln2_b)
    y_prompt = encode(x_prompt, meta_tokens, ln_emb_g, ln_emb_b, layer_params)
    y_sample = encode(x_sample, meta_tokens, ln_emb_g, ln_emb_b, layer_params)
    return (y_prompt, y_sample)
```

```python
import functools

import numpy as np
import jax
import jax.numpy as jnp
from jax import lax
from jax.experimental import pallas as pl
from jax.experimental.pallas import tpu as pltpu

D_MODEL = 1024
N_META = 16
N_HEADS = 16
N_KV_HEADS = 4
HEAD_DIM = 64
GQA_GROUP = N_HEADS // N_KV_HEADS
WINDOW = 128
BLOCK = 128
POOL_WINDOWS = (2, 4, 8, 16)
POOL_GROUP_WIDTH = 128
POOL_WIDTH = 512
KV_WIDTH = N_KV_HEADS * HEAD_DIM
D_FF = 4 * D_MODEL
LN_EPS = 1e-5
HALO = 8
META_ROW0 = BLOCK - N_META
HALF = BLOCK // 2
KEY_COLS = 3 * BLOCK
BAND_COLS = KEY_COLS - HALF
NEG = -1e30
TOKEN_TILE = 512
VMEM_LIMIT = 56 * 1024 * 1024

F32 = jnp.float32
BF16 = jnp.bfloat16


def _layout(seq_lens):
    blocks_per_tile = TOKEN_TILE // BLOCK
    starts, n = [], 0
    for s in seq_lens:
        assert s % BLOCK == 0 and s >= BLOCK
        starts.append(n)
        n += 1 + s // BLOCK
    nb = -(-n // blocks_per_tile) * blocks_per_tile
    prev_i = np.arange(nb, dtype=np.int32)
    next_i = np.arange(nb, dtype=np.int32)
    meta_i = np.arange(nb, dtype=np.int32)
    flags = np.full(nb, 2, dtype=np.int32)
    t0 = np.zeros(nb + 2, dtype=np.int32)
    ln = np.zeros(nb + 2, dtype=np.int32)
    for s, b0 in zip(seq_lens, starts):
        nreal = s // BLOCK
        for j in range(nreal + 1):
            b = b0 + j
            meta_i[b] = b0
            t0[b + 1] = -META_ROW0 + BLOCK * j
            ln[b + 1] = N_META + s
            f = 0 if j == 0 else 2
            if j >= 2:
                prev_i[b] = b - 1
                f |= 1
            if j < nreal:
                next_i[b] = b + 1
                f |= 4
            flags[b] = f
    return dict(nb=nb, starts=starts, prev=prev_i, next=next_i, meta=meta_i, flags=flags, t0=t0, ln=ln)


def _alibi_bias_table():
    slopes = 2.0 ** (-8.0 * np.arange(1, N_HEADS + 1) / N_HEADS)
    col = np.arange(KEY_COLS)[None, :]
    row = np.arange(HALF)[:, None]
    rel = np.abs(col - BLOCK - row).astype(np.float64)
    band = (col < BAND_COLS) & (rel <= WINDOW)
    meta = col >= KEY_COLS - N_META
    out = np.empty((N_HEADS, HALF, KEY_COLS), np.float32)
    for h in range(N_HEADS):
        out[h] = np.where(band, -np.float32(slopes[h]) * rel.astype(np.float32), np.where(meta, 0.0, NEG))
    return out.reshape(N_KV_HEADS, GQA_GROUP * HALF, KEY_COLS)


def _const_spec(shape, index_map):
    return pl.BlockSpec(shape, index_map, pipeline_mode=pl.Buffered(1))


def _layer_norm(x, g, b):
    mu = jnp.mean(x, axis=-1, keepdims=True)
    xc = x - mu
    var = jnp.mean(xc * xc, axis=-1, keepdims=True)
    return xc * lax.rsqrt(var + LN_EPS) * g + b


def _dot(a, b):
    return jnp.dot(a, b, preferred_element_type=F32)


def _embed_kernel(x_ref, g_ref, b_ref, o_ref):
    o_ref[...] = _layer_norm(x_ref[...], g_ref[...], b_ref[...])


def _embed(x, g, b):
    t = x.shape[0]
    return pl.pallas_call(
        _embed_kernel,
        grid=(t // TOKEN_TILE,),
        in_specs=[pl.BlockSpec((TOKEN_TILE, D_MODEL), lambda i: (i, 0)),
                  pl.BlockSpec((1, D_MODEL), lambda i: (0, 0)),
                  pl.BlockSpec((1, D_MODEL), lambda i: (0, 0))],
        out_specs=pl.BlockSpec((TOKEN_TILE, D_MODEL), lambda i: (i, 0)),
        out_shape=jax.ShapeDtypeStruct((t, D_MODEL), F32),
        name="embed_ln",
    )(x, g, b)


def _inproj_kernel(h_ref, w_ref, q_ref, kv_ref, u_ref, g_ref):
    x = h_ref[...].astype(BF16)
    q_ref[...] = (_dot(x, w_ref[:, 0:D_MODEL]) * (HEAD_DIM ** -0.5)).astype(BF16)
    kv_ref[...] = _dot(x, w_ref[:, D_MODEL:D_MODEL + 2 * KV_WIDTH]).astype(BF16)
    u_ref[...] = _dot(x, w_ref[:, D_MODEL + 2 * KV_WIDTH:2 * D_MODEL])
    g_ref[...] = jax.nn.sigmoid(_dot(x, w_ref[:, 2 * D_MODEL:4 * D_MODEL])).astype(BF16)


def _inproj(h, w_in, layer):
    t = h.shape[0]
    row = lambda i: (i, 0)
    return pl.pallas_call(
        _inproj_kernel,
        grid=(t // TOKEN_TILE,),
        in_specs=[pl.BlockSpec((TOKEN_TILE, D_MODEL), row),
                  _const_spec((None, D_MODEL, 4 * D_MODEL), lambda i: (layer, 0, 0))],
        out_specs=[pl.BlockSpec((TOKEN_TILE, D_MODEL), row),
                   pl.BlockSpec((TOKEN_TILE, 2 * KV_WIDTH), row),
                   pl.BlockSpec((TOKEN_TILE, POOL_WIDTH), row),
                   pl.BlockSpec((TOKEN_TILE, 2 * D_MODEL), row)],
        out_shape=[jax.ShapeDtypeStruct((t, D_MODEL), BF16),
                   jax.ShapeDtypeStruct((t, 2 * KV_WIDTH), BF16),
                   jax.ShapeDtypeStruct((t, POOL_WIDTH), F32),
                   jax.ShapeDtypeStruct((t, 2 * D_MODEL), BF16)],
        compiler_params=pltpu.CompilerParams(vmem_limit_bytes=VMEM_LIMIT),
        name="in_proj",
    )(h, w_in)


def _attn_kernel(prev_t, next_t, meta_t, flag_t,
                 sink_ref, q_ref, kvp_ref, kvo_ref, kvn_ref, kvm_ref, bias_ref, o_ref):
    del prev_t, next_t, meta_t
    f = flag_t[pl.program_id(0)]
    ok_prev = jnp.where((f & 1) != 0, 0.0, NEG).astype(F32)
    ok_own = jnp.where((f & 2) != 0, 0.0, NEG).astype(F32)
    ok_next = jnp.where((f & 4) != 0, 0.0, NEG).astype(F32)
    col = lax.broadcasted_iota(jnp.int32, (1, KEY_COLS), 1)
    low = lax.broadcasted_iota(jnp.int32, (1, BLOCK), 1) < HEAD_DIM

    def split_pair(qp):
        qf = qp.astype(F32)
        even = jnp.where(low, qf, 0.0).astype(BF16)
        odd = jnp.where(low, pltpu.roll(qf, HEAD_DIM, 1), 0.0).astype(BF16)
        return even, odd

    for k in range(N_KV_HEADS):
        ks = slice(BLOCK * k, BLOCK * (k + 1))
        kp, ko, kn = kvp_ref[:, ks], kvo_ref[:, ks], kvn_ref[:, ks]
        km = kvm_ref[HALF:BLOCK, ks]
        qa_e, qa_o = split_pair(q_ref[:, 2 * BLOCK * k:2 * BLOCK * k + BLOCK])
        qb_e, qb_o = split_pair(q_ref[:, 2 * BLOCK * k + BLOCK:2 * BLOCK * (k + 1)])
        sink_col = jnp.concatenate(
            [jnp.full((HALF, 1), sink_ref[GQA_GROUP * k + j], F32) for j in range(GQA_GROUP)], axis=0)
        for c in range(2):
            rows = slice(HALF * c, HALF * (c + 1))
            own0 = BLOCK - HALF * c
            if c == 0:
                kc = jnp.concatenate([kp, ko, kn[0:HALF], km], axis=0)
            else:
                kc = jnp.concatenate([kp[HALF:BLOCK], ko, kn, km], axis=0)
            colmask = jnp.where(col < own0, ok_prev,
                                jnp.where(col < own0 + BLOCK, ok_own,
                                          jnp.where(col < BAND_COLS, ok_next, 0.0)))
            q4 = jnp.concatenate([qa_e[rows], qa_o[rows], qb_e[rows], qb_o[rows]], axis=0)
            s = lax.dot_general(q4, kc, (((1,), (1,)), ((), ())), preferred_element_type=F32)
            s = s + bias_ref[k] + colmask
            m = jnp.maximum(jnp.max(s, axis=-1, keepdims=True), sink_col)
            p = jnp.exp(s - m)
            denom = jnp.sum(p, axis=-1, keepdims=True) + jnp.exp(sink_col - m)
            o = _dot(p.astype(BF16), kc) / denom
            pair_a = jnp.where(low, pltpu.roll(o[0:HALF], HEAD_DIM, 1), o[HALF:2 * HALF])
            pair_b = jnp.where(low, pltpu.roll(o[2 * HALF:3 * HALF], HEAD_DIM, 1), o[3 * HALF:4 * HALF])
            o_ref[rows, 2 * BLOCK * k:2 * BLOCK * k + BLOCK] = pair_a.astype(BF16)
            o_ref[rows, 2 * BLOCK * k + BLOCK:2 * BLOCK * (k + 1)] = pair_b.astype(BF16)


def _attention(q, kv, sink, bias, lay):
    t = q.shape[0]
    kvw = 2 * KV_WIDTH
    grid_spec = pltpu.PrefetchScalarGridSpec(
        num_scalar_prefetch=4,
        grid=(lay["nb"],),
        in_specs=[pl.BlockSpec(memory_space=pltpu.SMEM),
                  pl.BlockSpec((BLOCK, D_MODEL), lambda b, p, n, m, f: (b, 0)),
                  pl.BlockSpec((BLOCK, kvw), lambda b, p, n, m, f: (p[b], 0)),
                  pl.BlockSpec((BLOCK, kvw), lambda b, p, n, m, f: (b, 0)),
                  pl.BlockSpec((BLOCK, kvw), lambda b, p, n, m, f: (n[b], 0)),
                  pl.BlockSpec((BLOCK, kvw), lambda b, p, n, m, f: (m[b], 0)),
                  _const_spec((N_KV_HEADS, GQA_GROUP * HALF, KEY_COLS), lambda b, p, n, m, f: (0, 0, 0))],
        out_specs=pl.BlockSpec((BLOCK, D_MODEL), lambda b, p, n, m, f: (b, 0)),
    )
    return pl.pallas_call(
        _attn_kernel,
        grid_spec=grid_spec,
        out_shape=jax.ShapeDtypeStruct((t, D_MODEL), BF16),
        name="band_attn",
    )(lay["prev"], lay["next"], lay["meta"], lay["flags"], sink, q, kv, kv, kv, kv, bias)


def _pool_diff(u_ext, t_ext, l_ext):
    n = TOKEN_TILE + 2 * HALO
    valid = (t_ext >= 0) & (t_ext < l_ext)
    t = t_ext[HALO:HALO + TOKEN_TILE]
    ln = l_ext[HALO:HALO + TOKEN_TILE]
    fwd = lambda x, k: x + pltpu.roll(x, n - k, 0)
    outs = []
    for gi, w in enumerate(POOL_WINDOWS):
        x = jnp.where(valid, u_ext[:, gi * POOL_GROUP_WIDTH:(gi + 1) * POOL_GROUP_WIDTH], 0.0)
        acc, k = x, 1
        while 2 * k <= min(w, HALO):
            acc = fwd(acc, k)
            k *= 2
        if w <= HALO:
            win = pltpu.roll(acc, w // 2, 0)
        else:
            win = acc + pltpu.roll(acc, HALO, 0)
        cnt = jnp.minimum(t + w // 2, ln) - jnp.maximum(t - w // 2, 0)
        cnt = jnp.maximum(cnt, 1).astype(F32)
        outs.append(win[HALO:HALO + TOKEN_TILE] / cnt - x[HALO:HALO + TOKEN_TILE])
    return outs


def _mix_kernel(t0_t, ln_t,
                attn_ref, u_ref, up_ref, un_ref, g_ref, h_ref,
                wa_ref, wp_ref, ps_ref, wb_ref, wo_ref, lg_ref, lb_ref, o_ref, *, alpha):
    i = pl.program_id(0)
    bpt = TOKEN_TILE // BLOCK
    b0 = i * bpt

    def rows_of(entry, first, count):
        r = lax.broadcasted_iota(jnp.int32, (count, POOL_GROUP_WIDTH), 0) + first
        return r + t0_t[entry], jnp.zeros((count, POOL_GROUP_WIDTH), jnp.int32) + ln_t[entry]

    parts = [rows_of(b0, BLOCK - HALO, HALO)]
    parts += [rows_of(b0 + 1 + j, 0, BLOCK) for j in range(bpt)]
    parts += [rows_of(b0 + 1 + bpt, 0, HALO)]
    t_ext = jnp.concatenate([p[0] for p in parts], axis=0)
    l_ext = jnp.concatenate([p[1] for p in parts], axis=0)
    u_ext = jnp.concatenate([up_ref[...], u_ref[...], un_ref[...]], axis=0)
    diffs = _pool_diff(u_ext, t_ext, l_ext)
    y = jnp.concatenate([_dot(d.astype(BF16), wp_ref[gi]) for gi, d in enumerate(diffs)], axis=1)
    yb = _dot((y * ps_ref[...]).astype(BF16), wb_ref[...])
    ya = _dot(attn_ref[...], wa_ref[...])
    g = g_ref[...].astype(F32)
    mixed_in = g[:, 0:D_MODEL] * ya + g[:, D_MODEL:2 * D_MODEL] * yb
    mixed = _dot(mixed_in.astype(BF16), wo_ref[...])
    o_ref[...] = _layer_norm(alpha * h_ref[...] + mixed, lg_ref[...], lb_ref[...])


def _mix(attn, u, g, h, w_bo_attn, w_pool, pool_scale, w_bo_pool, w_out, ln_g, ln_b, layer, lay, alpha):
    t = h.shape[0]
    n_halo = t // HALO
    per = TOKEN_TILE // HALO
    row = lambda i, a, b: (i, 0)
    vec = lambda i, a, b: (layer, 0, 0)
    grid_spec = pltpu.PrefetchScalarGridSpec(
        num_scalar_prefetch=2,
        grid=(t // TOKEN_TILE,),
        in_specs=[pl.BlockSpec((TOKEN_TILE, D_MODEL), row),
                  pl.BlockSpec((TOKEN_TILE, POOL_WIDTH), row),
                  pl.BlockSpec((HALO, POOL_WIDTH), lambda i, a, b: (jnp.maximum(i * per - 1, 0), 0)),
                  pl.BlockSpec((HALO, POOL_WIDTH), lambda i, a, b: (jnp.minimum((i + 1) * per, n_halo - 1), 0)),
                  pl.BlockSpec((TOKEN_TILE, 2 * D_MODEL), row),
                  pl.BlockSpec((TOKEN_TILE, D_MODEL), row),
                  _const_spec((None, D_MODEL, D_MODEL), vec),
                  _const_spec((None, len(POOL_WINDOWS), POOL_GROUP_WIDTH, POOL_GROUP_WIDTH),
                              lambda i, a, b: (layer, 0, 0, 0)),
                  _const_spec((None, 1, POOL_WIDTH), vec),
                  _const_spec((None, POOL_WIDTH, D_MODEL), vec),
                  _const_spec((None, D_MODEL, D_MODEL), vec),
                  _const_spec((None, 1, D_MODEL), vec),
                  _const_spec((None, 1, D_MODEL), vec)],
        out_specs=pl.BlockSpec((TOKEN_TILE, D_MODEL), row),
    )
    return pl.pallas_call(
        functools.partial(_mix_kernel, alpha=alpha),
        grid_spec=grid_spec,
        out_shape=jax.ShapeDtypeStruct((t, D_MODEL), F32),
        compiler_params=pltpu.CompilerParams(vmem_limit_bytes=VMEM_LIMIT),
        name="mix_ln",
    )(lay["t0"], lay["ln"], attn, u, u, u, g, h, w_bo_attn, w_pool, pool_scale, w_bo_pool, w_out, ln_g, ln_b)


FF_CHUNK = 1024


def _mlp_kernel(x_ref, w1_ref, b1_ref, w2_ref, b2_ref, lg_ref, lb_ref, o_ref, *, alpha):
    x = x_ref[...]
    xb = x.astype(BF16)
    acc = jnp.zeros((TOKEN_TILE, D_MODEL), F32)
    for c in range(D_FF // FF_CHUNK):
        cs = slice(c * FF_CHUNK, (c + 1) * FF_CHUNK)
        a = jnp.maximum(_dot(xb, w1_ref[:, cs]) + b1_ref[:, cs], 0.0)
        acc = acc + _dot((a * a).astype(BF16), w2_ref[cs, :])
    o_ref[...] = _layer_norm(alpha * x + (acc + b2_ref[...]), lg_ref[...], lb_ref[...])


def _mlp(x, w1, b1, w2, b2, ln_g, ln_b, layer, alpha):
    t = x.shape[0]
    row = lambda i: (i, 0)
    vec = lambda i: (layer, 0, 0)
    return pl.pallas_call(
        functools.partial(_mlp_kernel, alpha=alpha),
        grid=(t // TOKEN_TILE,),
        in_specs=[pl.BlockSpec((TOKEN_TILE, D_MODEL), row),
                  _const_spec((None, D_MODEL, D_FF), vec),
                  _const_spec((None, 1, D_FF), vec),
                  _const_spec((None, D_FF, D_MODEL), vec),
                  _const_spec((None, 1, D_MODEL), vec),
                  _const_spec((None, 1, D_MODEL), vec),
                  _const_spec((None, 1, D_MODEL), vec)],
        out_specs=pl.BlockSpec((TOKEN_TILE, D_MODEL), row),
        out_shape=jax.ShapeDtypeStruct((t, D_MODEL), F32),
        compiler_params=pltpu.CompilerParams(vmem_limit_bytes=VMEM_LIMIT),
        name="mlp_ln",
    )(x, w1, b1, w2, b2, ln_g, ln_b)


def _permute_w_in(w_in):
    q = w_in[..., 0:D_MODEL]
    k = w_in[..., D_MODEL:D_MODEL + KV_WIDTH].reshape(w_in.shape[:-1] + (N_KV_HEADS, 1, HEAD_DIM))
    v = w_in[..., D_MODEL + KV_WIDTH:D_MODEL + 2 * KV_WIDTH].reshape(w_in.shape[:-1] + (N_KV_HEADS, 1, HEAD_DIM))
    kv = jnp.concatenate([k, v], axis=-2).reshape(w_in.shape[:-1] + (2 * KV_WIDTH,))
    rest = w_in[..., D_MODEL + 2 * KV_WIDTH:]
    return jnp.concatenate([q, kv, rest], axis=-1).astype(BF16)


def kernel(x_prompt, x_sample, meta_tokens, ln_emb_g, ln_emb_b, w_in, sink, w_pool, pool_scale, w_bo_attn,
           w_bo_pool, w_out, ln1_g, ln1_b, w_mlp1, b_mlp1, w_mlp2, b_mlp2, ln2_g, ln2_b):
    depth = w_in.shape[0]
    alpha = float((2 * depth) ** 0.25)
    groups = (x_prompt, x_sample)
    seq_lens = [x.shape[1] for x in groups for _ in range(x.shape[0])]
    lay = _layout(seq_lens)
    t = lay["nb"] * BLOCK

    header = jnp.concatenate([jnp.zeros((META_ROW0, D_MODEL), F32), meta_tokens.astype(F32)], axis=0)
    pieces = []
    for x in groups:
        hb = jnp.broadcast_to(header[None], (x.shape[0], BLOCK, D_MODEL))
        pieces.append(jnp.concatenate([hb, x], axis=1).reshape(-1, D_MODEL))
    used = sum(p.shape[0] for p in pieces)
    if t > used:
        pieces.append(jnp.zeros((t - used, D_MODEL), F32))
    x0 = jnp.concatenate(pieces, axis=0)

    row3 = lambda a: a.reshape(a.shape[0], 1, a.shape[-1])
    w_in_b = _permute_w_in(w_in)
    w_pool_b, w_ba_b, w_bp_b, w_out_b = (a.astype(BF16) for a in (w_pool, w_bo_attn, w_bo_pool, w_out))
    w1_b, w2_b = w_mlp1.astype(BF16), w_mlp2.astype(BF16)
    bias = jnp.asarray(_alibi_bias_table())

    h = _embed(x0, ln_emb_g.reshape(1, -1), ln_emb_b.reshape(1, -1))
    for l in range(depth):
        q, kv, u, g = _inproj(h, w_in_b, l)
        attn = _attention(q, kv, sink[l].astype(F32), bias, lay)
        x1 = _mix(attn, u, g, h, w_ba_b, w_pool_b, row3(pool_scale), w_bp_b, w_out_b, row3(ln1_g), row3(ln1_b),
                  l, lay, alpha)
        h = _mlp(x1, w1_b, row3(b_mlp1), w2_b, row3(b_mlp2), row3(ln2_g), row3(ln2_b), l, alpha)

    outs = []
    for x, b0 in zip(groups, (0, x_prompt.shape[0])):
        bsz, s = x.shape[0], x.shape[1]
        r0 = lay["starts"][b0] * BLOCK
        seg = h[r0:r0 + bsz * (BLOCK + s)].reshape(bsz, BLOCK + s, D_MODEL)
        outs.append(seg[:, BLOCK:])
    return tuple(outs)
```

```python
import functools

import numpy as np
import jax
import jax.numpy as jnp
from jax import lax
from jax.experimental import pallas as pl
from jax.experimental.pallas import tpu as pltpu

D_MODEL = 1024
N_META = 16
N_HEADS = 16
N_KV_HEADS = 4
HEAD_DIM = 64
GQA_GROUP = N_HEADS // N_KV_HEADS
WINDOW = 128
BLOCK = 128
POOL_WINDOWS = (2, 4, 8, 16)
POOL_GROUP_WIDTH = 128
POOL_WIDTH = 512
KV_WIDTH = N_KV_HEADS * HEAD_DIM
GATE_WIDTH = 2 * D_MODEL
D_FF = 4 * D_MODEL
LN_EPS = 1e-5
HALO = 8
META_ROW0 = BLOCK - N_META
BAND_KEYS = 3 * BLOCK
KEYS = BAND_KEYS + N_META
NEG = -1e30
LOG2E = 1.4426950408889634
ONES_ROWS = 16
BLOCK_FLAG_VARIANTS = (7, 6, 3, 4, 2)
TOKEN_TILE = 512
VMEM_LIMIT = 56 * 1024 * 1024

F32 = jnp.float32
BF16 = jnp.bfloat16
NT_DIMS = (((1,), (1,)), ((), ()))


def _layout(group_shapes):
    blocks_per_tile = TOKEN_TILE // BLOCK
    seqs = []
    n = 0
    for gi, (bsz, s) in enumerate(group_shapes):
        assert s % BLOCK == 0 and s >= BLOCK
        for bi in range(bsz):
            seqs.append((gi, bi, s // BLOCK, n))
            n += 1 + s // BLOCK
    nb = -(-n // blocks_per_tile) * blocks_per_tile
    meta_i = np.arange(nb, dtype=np.int32)
    flags = np.full(nb, 2, dtype=np.int32)
    t0 = np.zeros(nb + 2, dtype=np.int32)
    ln = np.zeros(nb + 2, dtype=np.int32)
    kind = np.full(nb, -2, dtype=np.int32)
    src = np.zeros((len(group_shapes), nb), dtype=np.int32)
    for gi, bi, nreal, b0 in seqs:
        for j in range(nreal + 1):
            b = b0 + j
            meta_i[b] = b0
            t0[b + 1] = -META_ROW0 + BLOCK * j
            ln[b + 1] = N_META + BLOCK * nreal
            flags[b] = (0 if j == 0 else 2) | (1 if j >= 2 else 0) | (4 if j < nreal else 0)
            kind[b] = -1 if j == 0 else gi
            if j > 0:
                src[gi, b:] = bi * nreal + (j - 1)
    variant = np.asarray([BLOCK_FLAG_VARIANTS.index(int(f)) for f in flags], dtype=np.int32)
    starts = [b0 for _, _, _, b0 in seqs]
    return dict(nb=nb, starts=starts, meta=meta_i, variant=variant, t0=t0, ln=ln, kind=kind, src=src)


def _alibi_bias_table():
    nv = len(BLOCK_FLAG_VARIANTS)
    slopes = 2.0 ** (-8.0 * np.arange(1, N_HEADS + 1) / N_HEADS)
    key = jnp.arange(BAND_KEYS, dtype=jnp.int32)[:, None]
    qry = jnp.arange(BLOCK, dtype=jnp.int32)[None, :]
    rel = jnp.abs(key - BLOCK - qry).astype(F32)
    scaled = jnp.asarray(slopes * LOG2E, F32).reshape(N_KV_HEADS, 1, GQA_GROUP, 1)
    rel = rel[None, :, None, :]
    band = jnp.where(rel <= WINDOW, -(scaled * rel), NEG)
    flags = jnp.asarray(BLOCK_FLAG_VARIANTS, jnp.int32)[:, None]
    usable = (flags >> (jnp.arange(BAND_KEYS, dtype=jnp.int32)[None, :] // BLOCK)) & 1
    band = jnp.where(usable[:, None, :, None, None] != 0, band[None], NEG)
    band = band.reshape(nv, N_KV_HEADS, BAND_KEYS, GQA_GROUP * BLOCK)
    meta = jnp.zeros((nv, N_KV_HEADS, N_META, GQA_GROUP * BLOCK), F32)
    return jnp.concatenate([band, meta], axis=2)


def _const_spec(shape, index_map):
    return pl.BlockSpec(shape, index_map, pipeline_mode=pl.Buffered(1))


def _layer_norm(x, g, b):
    mu = jnp.mean(x, axis=-1, keepdims=True)
    xc = x - mu
    var = jnp.mean(xc * xc, axis=-1, keepdims=True)
    return xc * lax.rsqrt(var + LN_EPS) * g + b


def _dot(a, b):
    return jnp.dot(a, b, preferred_element_type=F32)


def _embed_kernel(kind_t, srca_t, srcb_t, xa_ref, xb_ref, hdr_ref, g_ref, b_ref, o_ref):
    del srca_t, srcb_t
    kind = kind_t[pl.program_id(0)]
    x = jnp.where(kind == 0, xa_ref[...], jnp.where(kind == 1, xb_ref[...], 0.0))
    x = jnp.where(kind == -1, hdr_ref[...], x)
    o_ref[...] = _layer_norm(x, g_ref[...], b_ref[...])


def _embed(xa, xb, header, g, b, lay):
    blk = lambda sel: pl.BlockSpec((BLOCK, D_MODEL), sel)
    grid_spec = pltpu.PrefetchScalarGridSpec(
        num_scalar_prefetch=3,
        grid=(lay["nb"],),
        in_specs=[blk(lambda i, k, sa, sb: (sa[i], 0)),
                  blk(lambda i, k, sa, sb: (sb[i], 0)),
                  blk(lambda i, k, sa, sb: (0, 0)),
                  pl.BlockSpec((1, D_MODEL), lambda i, k, sa, sb: (0, 0)),
                  pl.BlockSpec((1, D_MODEL), lambda i, k, sa, sb: (0, 0))],
        out_specs=blk(lambda i, k, sa, sb: (i, 0)),
    )
    return pl.pallas_call(
        _embed_kernel,
        grid_spec=grid_spec,
        out_shape=jax.ShapeDtypeStruct((lay["nb"] * BLOCK, D_MODEL), F32),
        name="embed_ln",
    )(lay["kind"], lay["src"][0], lay["src"][1], xa, xb, header, g, b)


def _inproj_kernel(h_ref, w_ref, wt_ref, qt_ref, k_ref, vt_ref, u_ref, g_ref):
    x = h_ref[...].astype(BF16)
    qt = lax.dot_general(wt_ref[0:D_MODEL, :], x, NT_DIMS, preferred_element_type=F32)
    qt_ref[...] = (qt * (HEAD_DIM ** -0.5 * LOG2E)).astype(BF16)
    vt = lax.dot_general(wt_ref[D_MODEL:D_MODEL + KV_WIDTH, :], x, NT_DIMS, preferred_element_type=F32)
    vt_ref[...] = vt.astype(BF16)
    k_ref[...] = _dot(x, w_ref[:, 0:KV_WIDTH]).astype(BF16)
    u_ref[...] = _dot(x, w_ref[:, KV_WIDTH:KV_WIDTH + POOL_WIDTH])
    gates = _dot(x, w_ref[:, KV_WIDTH + POOL_WIDTH:KV_WIDTH + POOL_WIDTH + GATE_WIDTH])
    g_ref[...] = jax.nn.sigmoid(gates).astype(BF16)


def _inproj(h, w_main, w_qvt, layer):
    t = h.shape[0]
    row = lambda i: (i, 0)
    col = lambda i: (0, i)
    n_main = KV_WIDTH + POOL_WIDTH + GATE_WIDTH
    return pl.pallas_call(
        _inproj_kernel,
        grid=(t // TOKEN_TILE,),
        in_specs=[pl.BlockSpec((TOKEN_TILE, D_MODEL), row),
                  _const_spec((None, D_MODEL, n_main), lambda i: (layer, 0, 0)),
                  _const_spec((None, D_MODEL + KV_WIDTH, D_MODEL), lambda i: (layer, 0, 0))],
        out_specs=[pl.BlockSpec((D_MODEL, TOKEN_TILE), col),
                   pl.BlockSpec((TOKEN_TILE, KV_WIDTH), row),
                   pl.BlockSpec((KV_WIDTH, TOKEN_TILE), col),
                   pl.BlockSpec((TOKEN_TILE, POOL_WIDTH), row),
                   pl.BlockSpec((TOKEN_TILE, GATE_WIDTH), row)],
        out_shape=[jax.ShapeDtypeStruct((D_MODEL, t), BF16),
                   jax.ShapeDtypeStruct((t, KV_WIDTH), BF16),
                   jax.ShapeDtypeStruct((KV_WIDTH, t), BF16),
                   jax.ShapeDtypeStruct((t, POOL_WIDTH), F32),
                   jax.ShapeDtypeStruct((t, GATE_WIDTH), BF16)],
        compiler_params=pltpu.CompilerParams(vmem_limit_bytes=VMEM_LIMIT),
        name="in_proj",
    )(h, w_main, w_qvt)


def _attn_kernel(meta_t, var_t,
                 sink_ref, qt_ref, kp_ref, ko_ref, kn_ref, kha_ref, khb_ref,
                 vp_ref, vo_ref, vn_ref, vha_ref, vhb_ref, bias_ref, o_ref):
    del meta_t
    step = pl.program_id(0)
    width = GQA_GROUP * BLOCK
    zero_q = jnp.zeros((HEAD_DIM, width), BF16)
    zero_p = jnp.zeros((META_ROW0, width), BF16)
    ones_band = jnp.ones((ONES_ROWS, BAND_KEYS), BF16)
    ones_meta = jnp.ones((ONES_ROWS, BLOCK), BF16)
    kh_refs, vh_refs = (kha_ref, khb_ref), (vha_ref, vhb_ref)

    def scores(blk, k):
        pair = slice(BLOCK * (k // 2), BLOCK * (k // 2 + 1))
        band = [kp_ref[:, pair], ko_ref[:, pair]] if blk == 0 else [ko_ref[:, pair], kn_ref[:, pair]]
        kc = jnp.concatenate(band + [kh_refs[blk][META_ROW0:BLOCK, pair]], axis=0)
        q0 = GQA_GROUP * HEAD_DIM * k
        qs = slice(BLOCK * blk, BLOCK * (blk + 1))
        qt4 = jnp.concatenate([qt_ref[q0 + HEAD_DIM * j:q0 + HEAD_DIM * (j + 1), qs]
                               for j in range(GQA_GROUP)], axis=1)
        rhs = jnp.concatenate([qt4, zero_q] if k % 2 == 0 else [zero_q, qt4], axis=0)
        return _dot(kc, rhs) + bias_ref[var_t[2 * step + blk], k]

    def finish(blk, k, s):
        sink_row = jnp.concatenate([jnp.full((1, BLOCK), sink_ref[GQA_GROUP * k + j] * LOG2E, F32)
                                    for j in range(GQA_GROUP)], axis=1)
        m = jnp.maximum(jnp.max(s, axis=0, keepdims=True), sink_row)
        pb = jnp.exp2(s - m).astype(BF16)
        vs = slice(HEAD_DIM * k, HEAD_DIM * (k + 1))
        band = [vp_ref[vs, :], vo_ref[vs, :]] if blk == 0 else [vo_ref[vs, :], vn_ref[vs, :]]
        vc = jnp.concatenate([jnp.concatenate(band, axis=1), ones_band], axis=0)
        vh = jnp.concatenate([vh_refs[blk][vs, :], ones_meta], axis=0)
        p_meta = jnp.concatenate([zero_p, pb[BAND_KEYS:KEYS]], axis=0)
        acc = _dot(vc, pb[0:BAND_KEYS]) + _dot(vh, p_meta)
        denom = acc[HEAD_DIM:HEAD_DIM + 1] + jnp.exp2(sink_row - m)
        ot = acc[0:HEAD_DIM] / denom
        rows = slice(BLOCK * blk, BLOCK * (blk + 1))
        for j in range(GQA_GROUP // 2):
            two = jnp.concatenate([ot[:, 2 * BLOCK * j:2 * BLOCK * j + BLOCK],
                                   ot[:, 2 * BLOCK * j + BLOCK:2 * BLOCK * (j + 1)]], axis=0)
            c0 = GQA_GROUP * HEAD_DIM * k + BLOCK * j
            o_ref[rows, c0:c0 + BLOCK] = two.T.astype(BF16)

    units = [(blk, k) for blk in range(2) for k in range(N_KV_HEADS)]
    s_next = scores(*units[0])
    for idx, unit in enumerate(units):
        s_cur = s_next
        if idx + 1 < len(units):
            s_next = scores(*units[idx + 1])
        finish(*unit, s_cur)


def _attention(qt, kk, vt, sink, bias, lay):
    t = kk.shape[0]
    nb = lay["nb"]
    kspec = lambda rows, sel: pl.BlockSpec((rows, KV_WIDTH), sel)
    vspec = lambda cols, sel: pl.BlockSpec((KV_WIDTH, cols), sel)
    before = lambda i: jnp.maximum(2 * i - 1, 0)
    after = lambda i: jnp.minimum(2 * i + 2, nb - 1)
    grid_spec = pltpu.PrefetchScalarGridSpec(
        num_scalar_prefetch=2,
        grid=(nb // 2,),
        in_specs=[pl.BlockSpec(memory_space=pltpu.SMEM),
                  pl.BlockSpec((D_MODEL, 2 * BLOCK), lambda i, m, v: (0, i)),
                  kspec(BLOCK, lambda i, m, v: (before(i), 0)),
                  kspec(2 * BLOCK, lambda i, m, v: (i, 0)),
                  kspec(BLOCK, lambda i, m, v: (after(i), 0)),
                  kspec(BLOCK, lambda i, m, v: (m[2 * i], 0)),
                  kspec(BLOCK, lambda i, m, v: (m[2 * i + 1], 0)),
                  vspec(BLOCK, lambda i, m, v: (0, before(i))),
                  vspec(2 * BLOCK, lambda i, m, v: (0, i)),
                  vspec(BLOCK, lambda i, m, v: (0, after(i))),
                  vspec(BLOCK, lambda i, m, v: (0, m[2 * i])),
                  vspec(BLOCK, lambda i, m, v: (0, m[2 * i + 1])),
                  _const_spec((len(BLOCK_FLAG_VARIANTS), N_KV_HEADS, KEYS, GQA_GROUP * BLOCK),
                              lambda i, m, v: (0, 0, 0, 0))],
        out_specs=pl.BlockSpec((2 * BLOCK, D_MODEL), lambda i, m, v: (i, 0)),
    )
    return pl.pallas_call(
        _attn_kernel,
        grid_spec=grid_spec,
        out_shape=jax.ShapeDtypeStruct((t, D_MODEL), BF16),
        compiler_params=pltpu.CompilerParams(vmem_limit_bytes=VMEM_LIMIT),
        name="band_attn",
    )(lay["meta"], lay["variant"], sink, qt, kk, kk, kk, kk, kk, vt, vt, vt, vt, vt, bias)


def _pool_diff(u_ext, t_ext, l_ext):
    n = TOKEN_TILE + 2 * HALO
    valid = (t_ext >= 0) & (t_ext < l_ext)
    t = t_ext[HALO:HALO + TOKEN_TILE]
    ln = l_ext[HALO:HALO + TOKEN_TILE]
    fwd = lambda x, k: x + pltpu.roll(x, n - k, 0)
    outs = []
    for gi, w in enumerate(POOL_WINDOWS):
        x = jnp.where(valid, u_ext[:, gi * POOL_GROUP_WIDTH:(gi + 1) * POOL_GROUP_WIDTH], 0.0)
        acc, k = x, 1
        while 2 * k <= min(w, HALO):
            acc = fwd(acc, k)
            k *= 2
        if w <= HALO:
            win = pltpu.roll(acc, w // 2, 0)
        else:
            win = acc + pltpu.roll(acc, HALO, 0)
        cnt = jnp.minimum(t + w // 2, ln) - jnp.maximum(t - w // 2, 0)
        cnt = jnp.maximum(cnt, 1).astype(F32)
        outs.append(win[HALO:HALO + TOKEN_TILE] / cnt - x[HALO:HALO + TOKEN_TILE])
    return outs


def _mix_kernel(t0_t, ln_t,
                attn_ref, u_ref, up_ref, un_ref, g_ref, h_ref,
                wa_ref, wp_ref, ps_ref, wb_ref, wo_ref, lg_ref, lb_ref, o_ref, *, alpha):
    i = pl.program_id(0)
    bpt = TOKEN_TILE // BLOCK
    b0 = i * bpt

    def rows_of(entry, first, count):
        r = lax.broadcasted_iota(jnp.int32, (count, POOL_GROUP_WIDTH), 0) + first
        return r + t0_t[entry], jnp.zeros((count, POOL_GROUP_WIDTH), jnp.int32) + ln_t[entry]

    parts = [rows_of(b0, BLOCK - HALO, HALO)]
    parts += [rows_of(b0 + 1 + j, 0, BLOCK) for j in range(bpt)]
    parts += [rows_of(b0 + 1 + bpt, 0, HALO)]
    t_ext = jnp.concatenate([p[0] for p in parts], axis=0)
    l_ext = jnp.concatenate([p[1] for p in parts], axis=0)
    u_ext = jnp.concatenate([up_ref[...], u_ref[...], un_ref[...]], axis=0)
    diffs = _pool_diff(u_ext, t_ext, l_ext)
    y = jnp.concatenate([_dot(d.astype(BF16), wp_ref[gi]) for gi, d in enumerate(diffs)], axis=1)
    yb = _dot((y * ps_ref[...]).astype(BF16), wb_ref[...])
    ya = _dot(attn_ref[...], wa_ref[...])
    g = g_ref[...].astype(F32)
    mixed_in = g[:, 0:D_MODEL] * ya + g[:, D_MODEL:2 * D_MODEL] * yb
    mixed = _dot(mixed_in.astype(BF16), wo_ref[...])
    o_ref[...] = _layer_norm(alpha * h_ref[...] + mixed, lg_ref[...], lb_ref[...])


def _mix(attn, u, g, h, w_bo_attn, w_pool, pool_scale, w_bo_pool, w_out, ln_g, ln_b, layer, lay, alpha):
    t = h.shape[0]
    n_halo = t // HALO
    per = TOKEN_TILE // HALO
    row = lambda i, a, b: (i, 0)
    vec = lambda i, a, b: (layer, 0, 0)
    grid_spec = pltpu.PrefetchScalarGridSpec(
        num_scalar_prefetch=2,
        grid=(t // TOKEN_TILE,),
        in_specs=[pl.BlockSpec((TOKEN_TILE, D_MODEL), row),
                  pl.BlockSpec((TOKEN_TILE, POOL_WIDTH), row),
                  pl.BlockSpec((HALO, POOL_WIDTH), lambda i, a, b: (jnp.maximum(i * per - 1, 0), 0)),
                  pl.BlockSpec((HALO, POOL_WIDTH), lambda i, a, b: (jnp.minimum((i + 1) * per, n_halo - 1), 0)),
                  pl.BlockSpec((TOKEN_TILE, GATE_WIDTH), row),
                  pl.BlockSpec((TOKEN_TILE, D_MODEL), row),
                  _const_spec((None, D_MODEL, D_MODEL), vec),
                  _const_spec((None, len(POOL_WINDOWS), POOL_GROUP_WIDTH, POOL_GROUP_WIDTH),
                              lambda i, a, b: (layer, 0, 0, 0)),
                  _const_spec((None, 1, POOL_WIDTH), vec),
                  _const_spec((None, POOL_WIDTH, D_MODEL), vec),
                  _const_spec((None, D_MODEL, D_MODEL), vec),
                  _const_spec((None, 1, D_MODEL), vec),
                  _const_spec((None, 1, D_MODEL), vec)],
        out_specs=pl.BlockSpec((TOKEN_TILE, D_MODEL), row),
    )
    return pl.pallas_call(
        functools.partial(_mix_kernel, alpha=alpha),
        grid_spec=grid_spec,
        out_shape=jax.ShapeDtypeStruct((t, D_MODEL), F32),
        compiler_params=pltpu.CompilerParams(vmem_limit_bytes=VMEM_LIMIT),
        name="mix_ln",
    )(lay["t0"], lay["ln"], attn, u, u, u, g, h, w_bo_attn, w_pool, pool_scale, w_bo_pool, w_out, ln_g, ln_b)


FF_CHUNK = 1024


def _mlp_kernel(x_ref, w1_ref, b1_ref, w2_ref, b2_ref, lg_ref, lb_ref, o_ref, *, alpha):
    x = x_ref[...]
    xb = x.astype(BF16)
    acc = jnp.zeros((TOKEN_TILE, D_MODEL), F32)
    for c in range(D_FF // FF_CHUNK):
        cs = slice(c * FF_CHUNK, (c + 1) * FF_CHUNK)
        a = jnp.maximum(_dot(xb, w1_ref[:, cs]) + b1_ref[:, cs], 0.0)
        acc = acc + _dot((a * a).astype(BF16), w2_ref[cs, :])
    o_ref[...] = _layer_norm(alpha * x + (acc + b2_ref[...]), lg_ref[...], lb_ref[...])


def _mlp(x, w1, b1, w2, b2, ln_g, ln_b, layer, alpha):
    t = x.shape[0]
    row = lambda i: (i, 0)
    vec = lambda i: (layer, 0, 0)
    return pl.pallas_call(
        functools.partial(_mlp_kernel, alpha=alpha),
        grid=(t // TOKEN_TILE,),
        in_specs=[pl.BlockSpec((TOKEN_TILE, D_MODEL), row),
                  _const_spec((None, D_MODEL, D_FF), vec),
                  _const_spec((None, 1, D_FF), vec),
                  _const_spec((None, D_FF, D_MODEL), vec),
                  _const_spec((None, 1, D_MODEL), vec),
                  _const_spec((None, 1, D_MODEL), vec),
                  _const_spec((None, 1, D_MODEL), vec)],
        out_specs=pl.BlockSpec((TOKEN_TILE, D_MODEL), row),
        out_shape=jax.ShapeDtypeStruct((t, D_MODEL), F32),
        compiler_params=pltpu.CompilerParams(vmem_limit_bytes=VMEM_LIMIT),
        name="mlp_ln",
    )(x, w1, b1, w2, b2, ln_g, ln_b)


def _split_w_in(w_in):
    q = w_in[..., 0:D_MODEL]
    k = w_in[..., D_MODEL:D_MODEL + KV_WIDTH]
    v = w_in[..., D_MODEL + KV_WIDTH:D_MODEL + 2 * KV_WIDTH]
    rest = w_in[..., D_MODEL + 2 * KV_WIDTH:]
    w_main = jnp.concatenate([k, rest], axis=-1).astype(BF16)
    w_qvt = jnp.swapaxes(jnp.concatenate([q, v], axis=-1), -1, -2).astype(BF16)
    return w_main, w_qvt


def kernel(x_prompt, x_sample, meta_tokens, ln_emb_g, ln_emb_b, w_in, sink, w_pool, pool_scale, w_bo_attn,
           w_bo_pool, w_out, ln1_g, ln1_b, w_mlp1, b_mlp1, w_mlp2, b_mlp2, ln2_g, ln2_b):
    depth = w_in.shape[0]
    alpha = float((2 * depth) ** 0.25)
    groups = (x_prompt, x_sample)
    lay = _layout([x.shape[:2] for x in groups])
    header = jnp.concatenate([jnp.zeros((META_ROW0, D_MODEL), F32), meta_tokens.astype(F32)], axis=0)

    row3 = lambda a: a.reshape(a.shape[0], 1, a.shape[-1])
    w_main, w_qvt = _split_w_in(w_in)
    w_pool_b, w_ba_b, w_bp_b, w_out_b = (a.astype(BF16) for a in (w_pool, w_bo_attn, w_bo_pool, w_out))
    w1_b, w2_b = w_mlp1.astype(BF16), w_mlp2.astype(BF16)
    bias = _alibi_bias_table()

    h = _embed(x_prompt.reshape(-1, D_MODEL), x_sample.reshape(-1, D_MODEL), header,
               ln_emb_g.reshape(1, -1), ln_emb_b.reshape(1, -1), lay)
    for l in range(depth):
        qt, kk, vt, u, g = _inproj(h, w_main, w_qvt, l)
        attn = _attention(qt, kk, vt, sink[l].astype(F32), bias, lay)
        x1 = _mix(attn, u, g, h, w_ba_b, w_pool_b, row3(pool_scale), w_bp_b, w_out_b, row3(ln1_g), row3(ln1_b),
                  l, lay, alpha)
        h = _mlp(x1, w1_b, row3(b_mlp1), w2_b, row3(b_mlp2), row3(ln2_g), row3(ln2_b), l, alpha)

    outs = []
    for x, b0 in zip(groups, (0, x_prompt.shape[0])):
        bsz, s = x.shape[0], x.shape[1]
        r0 = lay["starts"][b0] * BLOCK
        seg = h[r0:r0 + bsz * (BLOCK + s)].reshape(bsz, BLOCK + s, D_MODEL)
        outs.append(seg[:, BLOCK:])
    return tuple(outs)
```

```python
import functools

import numpy as np
import jax
import jax.numpy as jnp
from jax import lax
from jax.experimental import pallas as pl
from jax.experimental.pallas import tpu as pltpu

D_MODEL = 1024
N_META = 16
N_HEADS = 16
N_KV_HEADS = 4
HEAD_DIM = 64
GQA_GROUP = N_HEADS // N_KV_HEADS
WINDOW = 128
BLOCK = 128
POOL_WINDOWS = (2, 4, 8, 16)
POOL_GROUP_WIDTH = 128
POOL_WIDTH = 512
KV_WIDTH = N_KV_HEADS * HEAD_DIM
GATE_WIDTH = 2 * D_MODEL
D_FF = 4 * D_MODEL
FF_CHUNK = 1024
LN_EPS = 1e-5
HALO = 8
META_ROW0 = BLOCK - N_META
BAND_KEYS = 3 * BLOCK
KEYS = BAND_KEYS + N_META
NEG = -1e30
LOG2E = 1.4426950408889634
ONES_ROWS = 16
BLOCK_FLAG_VARIANTS = (7, 6, 3, 4, 2)
TOKEN_TILE = 1024
HALF_TILE = TOKEN_TILE // 2
BLOCKS_PER_TILE = TOKEN_TILE // BLOCK
VMEM_LIMIT = 56 * 1024 * 1024

F32 = jnp.float32
BF16 = jnp.bfloat16
NT_DIMS = (((1,), (1,)), ((), ()))


def _layout(group_shapes):
    seqs = []
    n = 0
    for gi, (bsz, s) in enumerate(group_shapes):
        assert s % BLOCK == 0 and s >= BLOCK
        for bi in range(bsz):
            seqs.append((gi, bi, s // BLOCK, n))
            n += 1 + s // BLOCK
    nb = -(-n // BLOCKS_PER_TILE) * BLOCKS_PER_TILE
    meta_i = np.arange(nb, dtype=np.int32)
    flags = np.full(nb, 2, dtype=np.int32)
    t0 = np.zeros(nb + 2, dtype=np.int32)
    ln = np.zeros(nb + 2, dtype=np.int32)
    kind = np.full(nb, -2, dtype=np.int32)
    src = np.zeros((len(group_shapes), nb), dtype=np.int32)
    for gi, bi, nreal, b0 in seqs:
        for j in range(nreal + 1):
            b = b0 + j
            meta_i[b] = b0
            t0[b + 1] = -META_ROW0 + BLOCK * j
            ln[b + 1] = N_META + BLOCK * nreal
            flags[b] = (0 if j == 0 else 2) | (1 if j >= 2 else 0) | (4 if j < nreal else 0)
            kind[b] = -1 if j == 0 else gi
            if j > 0:
                src[gi, b:] = bi * nreal + (j - 1)
    variant = np.asarray([BLOCK_FLAG_VARIANTS.index(int(f)) for f in flags], dtype=np.int32)
    return dict(nb=nb, meta=meta_i, variant=variant, t0=t0, ln=ln, kind=kind, src=src)


def _alibi_bias_table():
    nv = len(BLOCK_FLAG_VARIANTS)
    slopes = 2.0 ** (-8.0 * np.arange(1, N_HEADS + 1) / N_HEADS)
    key = jnp.arange(BAND_KEYS, dtype=jnp.int32)[:, None]
    qry = jnp.arange(BLOCK, dtype=jnp.int32)[None, :]
    rel = jnp.abs(key - BLOCK - qry).astype(F32)
    scaled = jnp.asarray(slopes * LOG2E, F32).reshape(N_KV_HEADS, 1, GQA_GROUP, 1)
    rel = rel[None, :, None, :]
    band = jnp.where(rel <= WINDOW, -(scaled * rel), NEG)
    flags = jnp.asarray(BLOCK_FLAG_VARIANTS, jnp.int32)[:, None]
    usable = (flags >> (jnp.arange(BAND_KEYS, dtype=jnp.int32)[None, :] // BLOCK)) & 1
    band = jnp.where(usable[:, None, :, None, None] != 0, band[None], NEG)
    band = band.reshape(nv, N_KV_HEADS, BAND_KEYS, GQA_GROUP * BLOCK)
    meta = jnp.zeros((nv, N_KV_HEADS, N_META, GQA_GROUP * BLOCK), F32)
    return jnp.concatenate([band, meta], axis=2)


def _const_spec(shape, index_map):
    return pl.BlockSpec(shape, index_map, pipeline_mode=pl.Buffered(1))


def _layer_norm(x, g, b):
    mu = jnp.mean(x, axis=-1, keepdims=True)
    xc = x - mu
    var = jnp.mean(xc * xc, axis=-1, keepdims=True)
    return xc * lax.rsqrt(var + LN_EPS) * g + b


def _dot(a, b):
    return jnp.dot(a, b, preferred_element_type=F32)


def _half(hf):
    return slice(HALF_TILE * hf, HALF_TILE * (hf + 1))


def _embed_kernel(kind_t, srca_t, srcb_t, xa_ref, xb_ref, hdr_ref, g_ref, b_ref, o_ref):
    del srca_t, srcb_t
    kind = kind_t[pl.program_id(0)]
    x = jnp.where(kind == 0, xa_ref[...], jnp.where(kind == 1, xb_ref[...], 0.0))
    x = jnp.where(kind == -1, hdr_ref[...], x)
    o_ref[...] = _layer_norm(x, g_ref[...], b_ref[...])


def _embed(xa, xb, header, g, b, lay):
    blk = lambda sel: pl.BlockSpec((BLOCK, D_MODEL), sel)
    grid_spec = pltpu.PrefetchScalarGridSpec(
        num_scalar_prefetch=3,
        grid=(lay["nb"],),
        in_specs=[blk(lambda i, k, sa, sb: (sa[i], 0)),
                  blk(lambda i, k, sa, sb: (sb[i], 0)),
                  blk(lambda i, k, sa, sb: (0, 0)),
                  pl.BlockSpec((1, D_MODEL), lambda i, k, sa, sb: (0, 0)),
                  pl.BlockSpec((1, D_MODEL), lambda i, k, sa, sb: (0, 0))],
        out_specs=blk(lambda i, k, sa, sb: (i, 0)),
    )
    return pl.pallas_call(
        _embed_kernel,
        grid_spec=grid_spec,
        out_shape=jax.ShapeDtypeStruct((lay["nb"] * BLOCK, D_MODEL), F32),
        name="embed_ln",
    )(lay["kind"], lay["src"][0], lay["src"][1], xa, xb, header, g, b)


def _inproj_kernel(h_ref, w_ref, wt_ref, qt_ref, k_ref, vt_ref, u_ref, g_ref):
    k0 = D_MODEL
    u0 = D_MODEL + 2 * KV_WIDTH
    g0 = u0 + POOL_WIDTH

    def project(hf):
        rows = _half(hf)
        x = h_ref[rows, :].astype(BF16)
        qt = lax.dot_general(wt_ref[0:D_MODEL, :], x, NT_DIMS, preferred_element_type=F32)
        qt_ref[:, rows] = (qt * (HEAD_DIM ** -0.5 * LOG2E)).astype(BF16)
        vt = lax.dot_general(wt_ref[D_MODEL:D_MODEL + KV_WIDTH, :], x, NT_DIMS, preferred_element_type=F32)
        vt_ref[:, rows] = vt.astype(BF16)
        k_ref[rows, :] = _dot(x, w_ref[:, k0:k0 + KV_WIDTH]).astype(BF16)
        u_ref[rows, :] = _dot(x, w_ref[:, u0:u0 + POOL_WIDTH])
        return _dot(x, w_ref[:, g0:g0 + GATE_WIDTH])

    gates_a = project(0)
    gates_b = project(1)
    g_ref[_half(0), :] = jax.nn.sigmoid(gates_a).astype(BF16)
    g_ref[_half(1), :] = jax.nn.sigmoid(gates_b).astype(BF16)


def _inproj(h, w_in, w_qvt, layer):
    t = h.shape[0]
    row = lambda i: (i, 0)
    col = lambda i: (0, i)
    return pl.pallas_call(
        _inproj_kernel,
        grid=(t // TOKEN_TILE,),
        in_specs=[pl.BlockSpec((TOKEN_TILE, D_MODEL), row),
                  _const_spec((None, D_MODEL, 4 * D_MODEL), lambda i: (layer, 0, 0)),
                  _const_spec((None, D_MODEL + KV_WIDTH, D_MODEL), lambda i: (layer, 0, 0))],
        out_specs=[pl.BlockSpec((D_MODEL, TOKEN_TILE), col),
                   pl.BlockSpec((TOKEN_TILE, KV_WIDTH), row),
                   pl.BlockSpec((KV_WIDTH, TOKEN_TILE), col),
                   pl.BlockSpec((TOKEN_TILE, POOL_WIDTH), row),
                   pl.BlockSpec((TOKEN_TILE, GATE_WIDTH), row)],
        out_shape=[jax.ShapeDtypeStruct((D_MODEL, t), BF16),
                   jax.ShapeDtypeStruct((t, KV_WIDTH), BF16),
                   jax.ShapeDtypeStruct((KV_WIDTH, t), BF16),
                   jax.ShapeDtypeStruct((t, POOL_WIDTH), F32),
                   jax.ShapeDtypeStruct((t, GATE_WIDTH), BF16)],
        compiler_params=pltpu.CompilerParams(vmem_limit_bytes=VMEM_LIMIT),
        name="in_proj",
    )(h, w_in, w_qvt)


def _attn_kernel(meta_t, var_t,
                 sink_ref, qt_ref, kp_ref, ko_ref, kn_ref, kha_ref, khb_ref,
                 vp_ref, vo_ref, vn_ref, vha_ref, vhb_ref, bias_ref, o_ref):
    del meta_t
    step = pl.program_id(0)
    width = GQA_GROUP * BLOCK
    zero_q = jnp.zeros((HEAD_DIM, width), BF16)
    zero_p = jnp.zeros((META_ROW0, width), BF16)
    ones_band = jnp.ones((ONES_ROWS, BAND_KEYS), BF16)
    ones_meta = jnp.ones((ONES_ROWS, BLOCK), BF16)
    kh_refs, vh_refs = (kha_ref, khb_ref), (vha_ref, vhb_ref)

    def scores(blk, k):
        pair = slice(BLOCK * (k // 2), BLOCK * (k // 2 + 1))
        band = [kp_ref[:, pair], ko_ref[:, pair]] if blk == 0 else [ko_ref[:, pair], kn_ref[:, pair]]
        kc = jnp.concatenate(band + [kh_refs[blk][META_ROW0:BLOCK, pair]], axis=0)
        q0 = GQA_GROUP * HEAD_DIM * k
        qs = slice(BLOCK * blk, BLOCK * (blk + 1))
        qt4 = jnp.concatenate([qt_ref[q0 + HEAD_DIM * j:q0 + HEAD_DIM * (j + 1), qs]
                               for j in range(GQA_GROUP)], axis=1)
        rhs = jnp.concatenate([qt4, zero_q] if k % 2 == 0 else [zero_q, qt4], axis=0)
        return _dot(kc, rhs) + bias_ref[var_t[2 * step + blk], k]

    def finish(blk, k, s):
        sink_row = jnp.concatenate([jnp.full((1, BLOCK), sink_ref[GQA_GROUP * k + j] * LOG2E, F32)
                                    for j in range(GQA_GROUP)], axis=1)
        m = jnp.maximum(jnp.max(s, axis=0, keepdims=True), sink_row)
        pb = jnp.exp2(s - m).astype(BF16)
        vs = slice(HEAD_DIM * k, HEAD_DIM * (k + 1))
        band = [vp_ref[vs, :], vo_ref[vs, :]] if blk == 0 else [vo_ref[vs, :], vn_ref[vs, :]]
        vc = jnp.concatenate([jnp.concatenate(band, axis=1), ones_band], axis=0)
        vh = jnp.concatenate([vh_refs[blk][vs, :], ones_meta], axis=0)
        p_meta = jnp.concatenate([zero_p, pb[BAND_KEYS:KEYS]], axis=0)
        acc = _dot(vc, pb[0:BAND_KEYS]) + _dot(vh, p_meta)
        denom = acc[HEAD_DIM:HEAD_DIM + 1] + jnp.exp2(sink_row - m)
        ot = acc[0:HEAD_DIM] / denom
        rows = slice(BLOCK * blk, BLOCK * (blk + 1))
        for j in range(GQA_GROUP // 2):
            two = jnp.concatenate([ot[:, 2 * BLOCK * j:2 * BLOCK * j + BLOCK],
                                   ot[:, 2 * BLOCK * j + BLOCK:2 * BLOCK * (j + 1)]], axis=0)
            c0 = GQA_GROUP * HEAD_DIM * k + BLOCK * j
            o_ref[rows, c0:c0 + BLOCK] = two.T.astype(BF16)

    units = [(blk, k) for blk in range(2) for k in range(N_KV_HEADS)]
    s_next = scores(*units[0])
    for idx, unit in enumerate(units):
        s_cur = s_next
        if idx + 1 < len(units):
            s_next = scores(*units[idx + 1])
        finish(*unit, s_cur)


def _attention(qt, kk, vt, sink, bias, lay):
    t = kk.shape[0]
    nb = lay["nb"]
    kspec = lambda rows, sel: pl.BlockSpec((rows, KV_WIDTH), sel)
    vspec = lambda cols, sel: pl.BlockSpec((KV_WIDTH, cols), sel)
    before = lambda i: jnp.maximum(2 * i - 1, 0)
    after = lambda i: jnp.minimum(2 * i + 2, nb - 1)
    grid_spec = pltpu.PrefetchScalarGridSpec(
        num_scalar_prefetch=2,
        grid=(nb // 2,),
        in_specs=[pl.BlockSpec(memory_space=pltpu.SMEM),
                  pl.BlockSpec((D_MODEL, 2 * BLOCK), lambda i, m, v: (0, i)),
                  kspec(BLOCK, lambda i, m, v: (before(i), 0)),
                  kspec(2 * BLOCK, lambda i, m, v: (i, 0)),
                  kspec(BLOCK, lambda i, m, v: (after(i), 0)),
                  kspec(BLOCK, lambda i, m, v: (m[2 * i], 0)),
                  kspec(BLOCK, lambda i, m, v: (m[2 * i + 1], 0)),
                  vspec(BLOCK, lambda i, m, v: (0, before(i))),
                  vspec(2 * BLOCK, lambda i, m, v: (0, i)),
                  vspec(BLOCK, lambda i, m, v: (0, after(i))),
                  vspec(BLOCK, lambda i, m, v: (0, m[2 * i])),
                  vspec(BLOCK, lambda i, m, v: (0, m[2 * i + 1])),
                  _const_spec((len(BLOCK_FLAG_VARIANTS), N_KV_HEADS, KEYS, GQA_GROUP * BLOCK),
                              lambda i, m, v: (0, 0, 0, 0))],
        out_specs=pl.BlockSpec((2 * BLOCK, D_MODEL), lambda i, m, v: (i, 0)),
    )
    return pl.pallas_call(
        _attn_kernel,
        grid_spec=grid_spec,
        out_shape=jax.ShapeDtypeStruct((t, D_MODEL), BF16),
        compiler_params=pltpu.CompilerParams(vmem_limit_bytes=VMEM_LIMIT),
        name="band_attn",
    )(lay["meta"], lay["variant"], sink, qt, kk, kk, kk, kk, kk, vt, vt, vt, vt, vt, bias)


def _pool_diff(u_ext, t_ext, l_ext):
    n = u_ext.shape[0]
    rows = n - 2 * HALO
    valid = (t_ext >= 0) & (t_ext < l_ext)
    t = t_ext[HALO:HALO + rows]
    ln = l_ext[HALO:HALO + rows]
    fwd = lambda x, k: x + pltpu.roll(x, n - k, 0)
    outs = []
    for gi, w in enumerate(POOL_WINDOWS):
        x = jnp.where(valid, u_ext[:, gi * POOL_GROUP_WIDTH:(gi + 1) * POOL_GROUP_WIDTH], 0.0)
        acc, k = x, 1
        while 2 * k <= min(w, HALO):
            acc = fwd(acc, k)
            k *= 2
        if w <= HALO:
            win = pltpu.roll(acc, w // 2, 0)
        else:
            win = acc + pltpu.roll(acc, HALO, 0)
        cnt = jnp.minimum(t + w // 2, ln) - jnp.maximum(t - w // 2, 0)
        cnt = jnp.maximum(cnt, 1).astype(F32)
        outs.append(win[HALO:HALO + rows] / cnt - x[HALO:HALO + rows])
    return outs


def _mix_kernel(t0_t, ln_t,
                attn_ref, u_ref, up_ref, un_ref, g_ref, h_ref,
                wa_ref, wp_ref, ps_ref, wb_ref, wo_ref, lg_ref, lb_ref, o_ref, *, alpha):
    b0 = pl.program_id(0) * BLOCKS_PER_TILE

    def rows_of(entry, first, count):
        r = lax.broadcasted_iota(jnp.int32, (count, POOL_GROUP_WIDTH), 0) + first
        return r + t0_t[entry], jnp.zeros((count, POOL_GROUP_WIDTH), jnp.int32) + ln_t[entry]

    parts = [rows_of(b0, BLOCK - HALO, HALO)]
    parts += [rows_of(b0 + 1 + j, 0, BLOCK) for j in range(BLOCKS_PER_TILE)]
    parts += [rows_of(b0 + 1 + BLOCKS_PER_TILE, 0, HALO)]
    t_ext = jnp.concatenate([p[0] for p in parts], axis=0)
    l_ext = jnp.concatenate([p[1] for p in parts], axis=0)
    u_ext = jnp.concatenate([up_ref[...], u_ref[...], un_ref[...]], axis=0)

    def pool(hf):
        ext = slice(HALF_TILE * hf, HALF_TILE * (hf + 1) + 2 * HALO)
        return _pool_diff(u_ext[ext], t_ext[ext], l_ext[ext])

    def attn_branch(hf):
        return _dot(attn_ref[_half(hf), :], wa_ref[...])

    def pool_branch(diffs):
        y = jnp.concatenate([_dot(d.astype(BF16), wp_ref[gi]) for gi, d in enumerate(diffs)], axis=1)
        return _dot((y * ps_ref[...]).astype(BF16), wb_ref[...])

    def gate(hf, ya, yb):
        g = g_ref[_half(hf), :].astype(F32)
        return (g[:, 0:D_MODEL] * ya + g[:, D_MODEL:2 * D_MODEL] * yb).astype(BF16)

    def norm(hf, mixed):
        o_ref[_half(hf), :] = _layer_norm(alpha * h_ref[_half(hf), :] + mixed, lg_ref[...], lb_ref[...])

    ya_a = attn_branch(0)
    pool_a = pool(0)
    ya_b = attn_branch(1)
    pool_b = pool(1)
    yb_a = pool_branch(pool_a)
    yb_b = pool_branch(pool_b)
    gated_a = gate(0, ya_a, yb_a)
    mixed_a = _dot(gated_a, wo_ref[...])
    gated_b = gate(1, ya_b, yb_b)
    mixed_b = _dot(gated_b, wo_ref[...])
    norm(0, mixed_a)
    norm(1, mixed_b)


def _mix(attn, u, g, h, weights, layer, lay, alpha):
    t = h.shape[0]
    n_halo = t // HALO
    per = TOKEN_TILE // HALO
    row = lambda i, a, b: (i, 0)
    vec = lambda i, a, b: (layer, 0, 0)
    mat = lambda r, c: _const_spec((None, r, c), vec)
    grid_spec = pltpu.PrefetchScalarGridSpec(
        num_scalar_prefetch=2,
        grid=(t // TOKEN_TILE,),
        in_specs=[pl.BlockSpec((TOKEN_TILE, D_MODEL), row),
                  pl.BlockSpec((TOKEN_TILE, POOL_WIDTH), row),
                  pl.BlockSpec((HALO, POOL_WIDTH), lambda i, a, b: (jnp.maximum(i * per - 1, 0), 0)),
                  pl.BlockSpec((HALO, POOL_WIDTH), lambda i, a, b: (jnp.minimum((i + 1) * per, n_halo - 1), 0)),
                  pl.BlockSpec((TOKEN_TILE, GATE_WIDTH), row),
                  pl.BlockSpec((TOKEN_TILE, D_MODEL), row),
                  mat(D_MODEL, D_MODEL),
                  _const_spec((None, len(POOL_WINDOWS), POOL_GROUP_WIDTH, POOL_GROUP_WIDTH),
                              lambda i, a, b: (layer, 0, 0, 0)),
                  mat(1, POOL_WIDTH), mat(POOL_WIDTH, D_MODEL), mat(D_MODEL, D_MODEL),
                  mat(1, D_MODEL), mat(1, D_MODEL)],
        out_specs=pl.BlockSpec((TOKEN_TILE, D_MODEL), row),
    )
    return pl.pallas_call(
        functools.partial(_mix_kernel, alpha=alpha),
        grid_spec=grid_spec,
        out_shape=jax.ShapeDtypeStruct((t, D_MODEL), F32),
        compiler_params=pltpu.CompilerParams(vmem_limit_bytes=VMEM_LIMIT),
        name="mix_ln",
    )(lay["t0"], lay["ln"], attn, u, u, u, g, h, *weights)


def _mlp_body(x_ref, w1_ref, b1_ref, w2_ref, b2_ref, lg_ref, lb_ref, store, alpha):
    def ff(xb, chunks):
        acc = None
        for c in chunks:
            cs = slice(c * FF_CHUNK, (c + 1) * FF_CHUNK)
            a = jnp.maximum(_dot(xb, w1_ref[:, cs]) + b1_ref[:, cs], 0.0)
            part = _dot((a * a).astype(BF16), w2_ref[cs, :])
            acc = part if acc is None else acc + part
        return acc

    def norm(x, acc):
        return _layer_norm(alpha * x + (acc + b2_ref[...]), lg_ref[...], lb_ref[...])

    first, rest = (0,), tuple(range(1, D_FF // FF_CHUNK))
    x_a = x_ref[_half(0), :]
    x_b = x_ref[_half(1), :]
    xb_a, xb_b = x_a.astype(BF16), x_b.astype(BF16)
    acc_a = ff(xb_a, first + rest)
    acc_b = ff(xb_b, first)
    store(0, norm(x_a, acc_a))
    acc_b = acc_b + ff(xb_b, rest)
    store(1, norm(x_b, acc_b))


def _mlp_kernel(kind_t, dsta_t, dstb_t, *refs, alpha):
    del kind_t, dsta_t, dstb_t
    ins, o_ref = refs[:-1], refs[-1]

    def store(hf, value):
        o_ref[_half(hf), :] = value

    _mlp_body(*ins, store, alpha)


def _mlp_final_kernel(kind_t, dsta_t, dstb_t, *refs, alpha):
    ins, (ya_ref, yb_ref, buf_ref, sem_ref) = refs[:-4], refs[-4:]
    step = pl.program_id(0)
    slot = step % 2

    def store(hf, value):
        buf_ref[slot, _half(hf), :] = value

    _mlp_body(*ins, store, alpha)

    def copies(of_step, of_slot, act):
        for j in range(BLOCKS_PER_TILE):
            b = of_step * BLOCKS_PER_TILE + j
            for gi, (y_ref, dst_t) in enumerate(((ya_ref, dsta_t), (yb_ref, dstb_t))):
                @pl.when(kind_t[b] == gi)
                def _():
                    src = buf_ref.at[of_slot, pl.ds(BLOCK * j, BLOCK), :]
                    dst = y_ref.at[pl.ds(pl.multiple_of(dst_t[b] * BLOCK, BLOCK), BLOCK), :]
                    act(pltpu.make_async_copy(src, dst, sem_ref.at[of_slot, j]))

    copies(step, slot, lambda c: c.start())

    @pl.when(step > 0)
    def _():
        copies(step - 1, 1 - slot, lambda c: c.wait())

    @pl.when(step == pl.num_programs(0) - 1)
    def _():
        copies(step, slot, lambda c: c.wait())


def _mlp(x, weights, layer, lay, alpha, out_rows=None):
    t = x.shape[0]
    row = lambda i, *_: (i, 0)
    vec = lambda i, *_: (layer, 0, 0)
    mat = lambda r, c: _const_spec((None, r, c), vec)
    in_specs = [pl.BlockSpec((TOKEN_TILE, D_MODEL), row),
                mat(D_MODEL, D_FF), mat(1, D_FF), mat(D_FF, D_MODEL), mat(1, D_MODEL),
                mat(1, D_MODEL), mat(1, D_MODEL)]
    if out_rows is None:
        body, name = _mlp_kernel, "mlp_ln"
        out_specs = pl.BlockSpec((TOKEN_TILE, D_MODEL), row)
        out_shape = jax.ShapeDtypeStruct((t, D_MODEL), F32)
        scratch = []
    else:
        body, name = _mlp_final_kernel, "mlp_ln_out"
        out_specs = [pl.BlockSpec(memory_space=pl.ANY)] * 2
        out_shape = [jax.ShapeDtypeStruct((r, D_MODEL), F32) for r in out_rows]
        scratch = [pltpu.VMEM((2, TOKEN_TILE, D_MODEL), F32),
                   pltpu.SemaphoreType.DMA((2, BLOCKS_PER_TILE))]
    grid_spec = pltpu.PrefetchScalarGridSpec(
        num_scalar_prefetch=3, grid=(t // TOKEN_TILE,),
        in_specs=in_specs, out_specs=out_specs, scratch_shapes=scratch)
    return pl.pallas_call(
        functools.partial(body, alpha=alpha),
        grid_spec=grid_spec,
        out_shape=out_shape,
        compiler_params=pltpu.CompilerParams(vmem_limit_bytes=VMEM_LIMIT,
                                             dimension_semantics=("arbitrary",)),
        name=name,
    )(lay["kind"], lay["src"][0], lay["src"][1], x, *weights)


def _qv_transposed(w_in):
    q = w_in[..., 0:D_MODEL]
    v = w_in[..., D_MODEL + KV_WIDTH:D_MODEL + 2 * KV_WIDTH]
    return jnp.swapaxes(jnp.concatenate([q, v], axis=-1).astype(BF16), -1, -2)


def kernel(x_prompt, x_sample, meta_tokens, ln_emb_g, ln_emb_b, w_in, sink, w_pool, pool_scale, w_bo_attn,
           w_bo_pool, w_out, ln1_g, ln1_b, w_mlp1, b_mlp1, w_mlp2, b_mlp2, ln2_g, ln2_b):
    depth = w_in.shape[0]
    alpha = float((2 * depth) ** 0.25)
    groups = (x_prompt, x_sample)
    lay = _layout([x.shape[:2] for x in groups])
    header = jnp.concatenate([jnp.zeros((META_ROW0, D_MODEL), F32), meta_tokens.astype(F32)], axis=0)

    row3 = lambda a: a.reshape(a.shape[0], 1, a.shape[-1])
    w_in_b, w_qvt = w_in.astype(BF16), _qv_transposed(w_in)
    mix_w = (w_bo_attn.astype(BF16), w_pool.astype(BF16), row3(pool_scale), w_bo_pool.astype(BF16),
             w_out.astype(BF16), row3(ln1_g), row3(ln1_b))
    mlp_w = (w_mlp1.astype(BF16), row3(b_mlp1), w_mlp2.astype(BF16), row3(b_mlp2), row3(ln2_g), row3(ln2_b))
    bias = _alibi_bias_table()

    h = _embed(x_prompt.reshape(-1, D_MODEL), x_sample.reshape(-1, D_MODEL), header,
               ln_emb_g.reshape(1, -1), ln_emb_b.reshape(1, -1), lay)
    out_rows = [x.shape[0] * x.shape[1] for x in groups]
    for l in range(depth):
        qt, kk, vt, u, g = _inproj(h, w_in_b, w_qvt, l)
        attn = _attention(qt, kk, vt, sink[l].astype(F32), bias, lay)
        x1 = _mix(attn, u, g, h, mix_w, l, lay, alpha)
        h = _mlp(x1, mlp_w, l, lay, alpha, out_rows if l == depth - 1 else None)
    return tuple(y.reshape(x.shape) for y, x in zip(h, groups))
```

```python
import functools

import numpy as np
import jax
import jax.numpy as jnp
from jax import lax
from jax.experimental import pallas as pl
from jax.experimental.pallas import tpu as pltpu

D_MODEL = 1024
N_META = 16
N_HEADS = 16
N_KV_HEADS = 4
HEAD_DIM = 64
GQA_GROUP = N_HEADS // N_KV_HEADS
WINDOW = 128
BLOCK = 128
POOL_WINDOWS = (2, 4, 8, 16)
POOL_GROUP_WIDTH = 128
POOL_WIDTH = 512
KV_WIDTH = N_KV_HEADS * HEAD_DIM
GATE_WIDTH = 2 * D_MODEL
D_FF = 4 * D_MODEL
FF_CHUNK = 1024
LN_EPS = 1e-5
HALO = 8
META_ROW0 = BLOCK - N_META
BAND_KEYS = 3 * BLOCK
KEYS = BAND_KEYS + N_META
NEG = -1e30
LOG2E = 1.4426950408889634
ONES_ROWS = 16
BLOCK_FLAG_VARIANTS = (7, 6, 3, 4, 2)
TOKEN_TILE = 1024
HALF_TILE = TOKEN_TILE // 2
BLOCKS_PER_TILE = TOKEN_TILE // BLOCK
EMBED_BLOCKS = 4
ATTN_BLOCKS = 8
VMEM_LIMIT = 56 * 1024 * 1024

F32 = jnp.float32
BF16 = jnp.bfloat16
NT_DIMS = (((1,), (1,)), ((), ()))


def _layout(group_shapes):
    seqs = []
    n = 0
    for gi, (bsz, s) in enumerate(group_shapes):
        assert s % BLOCK == 0 and s >= BLOCK
        for bi in range(bsz):
            seqs.append((gi, bi, s // BLOCK, n))
            n += 1 + s // BLOCK
    nb = -(-n // BLOCKS_PER_TILE) * BLOCKS_PER_TILE
    meta_i = np.arange(nb, dtype=np.int32)
    flags = np.full(nb, 2, dtype=np.int32)
    t0 = np.zeros(nb + 2, dtype=np.int32)
    ln = np.zeros(nb + 2, dtype=np.int32)
    kind = np.full(nb, -2, dtype=np.int32)
    src = np.zeros((len(group_shapes), nb), dtype=np.int32)
    for gi, bi, nreal, b0 in seqs:
        for j in range(nreal + 1):
            b = b0 + j
            meta_i[b] = b0
            t0[b + 1] = -META_ROW0 + BLOCK * j
            ln[b + 1] = N_META + BLOCK * nreal
            flags[b] = (0 if j == 0 else 2) | (1 if j >= 2 else 0) | (4 if j < nreal else 0)
            kind[b] = -1 if j == 0 else gi
            if j > 0:
                src[gi, b:] = bi * nreal + (j - 1)
    variant = np.asarray([BLOCK_FLAG_VARIANTS.index(int(f)) for f in flags], dtype=np.int32)
    return dict(nb=nb, meta=meta_i, variant=variant, t0=t0, ln=ln, kind=kind, src=src)


def _alibi_bias_table():
    nv = len(BLOCK_FLAG_VARIANTS)
    slopes = 2.0 ** (-8.0 * np.arange(1, N_HEADS + 1) / N_HEADS)
    key = np.arange(BAND_KEYS)[:, None]
    qry = np.arange(BLOCK)[None, :]
    rel = np.abs(key - BLOCK - qry).astype(np.float32)[None, :, None, :]
    scaled = (slopes * LOG2E).astype(np.float32).reshape(N_KV_HEADS, 1, GQA_GROUP, 1)
    band = np.where(rel <= WINDOW, -(scaled * rel), np.float32(NEG))
    flags = np.asarray(BLOCK_FLAG_VARIANTS)[:, None]
    usable = (flags >> (np.arange(BAND_KEYS)[None, :] // BLOCK)) & 1
    band = np.where(usable[:, None, :, None, None] != 0, band[None], np.float32(NEG))
    band = band.reshape(nv, N_KV_HEADS, BAND_KEYS, GQA_GROUP * BLOCK)
    meta = np.zeros((nv, N_KV_HEADS, N_META, GQA_GROUP * BLOCK), np.float32)
    return np.concatenate([band, meta], axis=2).astype(np.float32)


def _const_spec(shape, index_map):
    return pl.BlockSpec(shape, index_map, pipeline_mode=pl.Buffered(1))


def _layer_norm(x, g, b):
    mu = jnp.mean(x, axis=-1, keepdims=True)
    xc = x - mu
    var = jnp.mean(xc * xc, axis=-1, keepdims=True)
    return xc * lax.rsqrt(var + LN_EPS) * g + b


def _dot(a, b):
    return jnp.dot(a, b, preferred_element_type=F32)


def _half(hf):
    return slice(HALF_TILE * hf, HALF_TILE * (hf + 1))


def _embed_kernel(kind_t, srca_t, srcb_t, *refs):
    del srca_t, srcb_t
    xa_refs, xb_refs = refs[:EMBED_BLOCKS], refs[EMBED_BLOCKS:2 * EMBED_BLOCKS]
    hdr_ref, g_ref, b_ref, o_ref = refs[2 * EMBED_BLOCKS:]
    for j in range(EMBED_BLOCKS):
        kind = kind_t[pl.program_id(0) * EMBED_BLOCKS + j]
        x = jnp.where(kind == 0, xa_refs[j][...], jnp.where(kind == 1, xb_refs[j][...], 0.0))
        x = jnp.where(kind == -1, hdr_ref[...], x)
        o_ref[BLOCK * j:BLOCK * (j + 1), :] = _layer_norm(x, g_ref[...], b_ref[...])


def _embed(xa, xb, header, g, b, lay):
    blk = lambda sel: pl.BlockSpec((BLOCK, D_MODEL), sel)
    from_a = [blk(lambda i, k, sa, sb, j=j: (sa[i * EMBED_BLOCKS + j], 0)) for j in range(EMBED_BLOCKS)]
    from_b = [blk(lambda i, k, sa, sb, j=j: (sb[i * EMBED_BLOCKS + j], 0)) for j in range(EMBED_BLOCKS)]
    const = lambda i, k, sa, sb: (0, 0)
    grid_spec = pltpu.PrefetchScalarGridSpec(
        num_scalar_prefetch=3,
        grid=(lay["nb"] // EMBED_BLOCKS,),
        in_specs=from_a + from_b + [blk(const), pl.BlockSpec((1, D_MODEL), const),
                                    pl.BlockSpec((1, D_MODEL), const)],
        out_specs=pl.BlockSpec((EMBED_BLOCKS * BLOCK, D_MODEL), lambda i, k, sa, sb: (i, 0)),
    )
    return pl.pallas_call(
        _embed_kernel,
        grid_spec=grid_spec,
        out_shape=jax.ShapeDtypeStruct((lay["nb"] * BLOCK, D_MODEL), F32),
        name="embed_ln",
    )(lay["kind"], lay["src"][0], lay["src"][1], *([xa] * EMBED_BLOCKS), *([xb] * EMBED_BLOCKS), header, g, b)


def _qvt_kernel(w_ref, o_ref):
    o_ref[...] = w_ref[...].T.astype(BF16)


def _qv_transposed(w_in):
    depth = w_in.shape[0]
    q_blocks = D_MODEL // BLOCK
    v_block0 = (D_MODEL + KV_WIDTH) // BLOCK
    n_blocks = q_blocks + KV_WIDTH // BLOCK
    src_col = lambda c: jnp.where(c < q_blocks, c, c - q_blocks + v_block0)
    return pl.pallas_call(
        _qvt_kernel,
        grid=(depth, n_blocks),
        in_specs=[pl.BlockSpec((None, D_MODEL, BLOCK), lambda l, c: (l, 0, src_col(c)))],
        out_specs=pl.BlockSpec((None, BLOCK, D_MODEL), lambda l, c: (l, c, 0)),
        out_shape=jax.ShapeDtypeStruct((depth, D_MODEL + KV_WIDTH, D_MODEL), BF16),
        name="qv_transpose",
    )(w_in)


def _inproj_kernel(h_ref, w_ref, wt_ref, qt_ref, k_ref, vt_ref, u_ref, g_ref):
    k0 = D_MODEL
    u0 = D_MODEL + 2 * KV_WIDTH
    g0 = u0 + POOL_WIDTH

    def project(hf):
        rows = _half(hf)
        x = h_ref[rows, :].astype(BF16)
        qt = lax.dot_general(wt_ref[0:D_MODEL, :], x, NT_DIMS, preferred_element_type=F32)
        qt_ref[:, rows] = (qt * (HEAD_DIM ** -0.5 * LOG2E)).astype(BF16)
        vt = lax.dot_general(wt_ref[D_MODEL:D_MODEL + KV_WIDTH, :], x, NT_DIMS, preferred_element_type=F32)
        vt_ref[:, rows] = vt.astype(BF16)
        k_ref[rows, :] = _dot(x, w_ref[:, k0:k0 + KV_WIDTH]).astype(BF16)
        u_ref[rows, :] = _dot(x, w_ref[:, u0:u0 + POOL_WIDTH])
        return _dot(x, w_ref[:, g0:g0 + GATE_WIDTH])

    gates_a = project(0)
    gates_b = project(1)
    g_ref[_half(0), :] = jax.nn.sigmoid(gates_a).astype(BF16)
    g_ref[_half(1), :] = jax.nn.sigmoid(gates_b).astype(BF16)


def _inproj(h, w_in, w_qvt, layer):
    t = h.shape[0]
    row = lambda i: (i, 0)
    col = lambda i: (0, i)
    return pl.pallas_call(
        _inproj_kernel,
        grid=(t // TOKEN_TILE,),
        in_specs=[pl.BlockSpec((TOKEN_TILE, D_MODEL), row),
                  _const_spec((None, D_MODEL, 4 * D_MODEL), lambda i: (layer, 0, 0)),
                  _const_spec((None, D_MODEL + KV_WIDTH, D_MODEL), lambda i: (layer, 0, 0))],
        out_specs=[pl.BlockSpec((D_MODEL, TOKEN_TILE), col),
                   pl.BlockSpec((TOKEN_TILE, KV_WIDTH), row),
                   pl.BlockSpec((KV_WIDTH, TOKEN_TILE), col),
                   pl.BlockSpec((TOKEN_TILE, POOL_WIDTH), row),
                   pl.BlockSpec((TOKEN_TILE, GATE_WIDTH), row)],
        out_shape=[jax.ShapeDtypeStruct((D_MODEL, t), BF16),
                   jax.ShapeDtypeStruct((t, KV_WIDTH), BF16),
                   jax.ShapeDtypeStruct((KV_WIDTH, t), BF16),
                   jax.ShapeDtypeStruct((t, POOL_WIDTH), F32),
                   jax.ShapeDtypeStruct((t, GATE_WIDTH), BF16)],
        compiler_params=pltpu.CompilerParams(vmem_limit_bytes=VMEM_LIMIT),
        name="in_proj",
    )(h, w_in, w_qvt)


def _attn_kernel(meta_t, var_t, sink_ref, qt_ref, *refs):
    del meta_t
    kp_ref, ko_ref, kn_ref = refs[0:3]
    kh_refs = refs[3:3 + ATTN_BLOCKS]
    vp_ref, vo_ref, vn_ref = refs[3 + ATTN_BLOCKS:6 + ATTN_BLOCKS]
    vh_refs = refs[6 + ATTN_BLOCKS:6 + 2 * ATTN_BLOCKS]
    bias_ref, o_ref = refs[6 + 2 * ATTN_BLOCKS:]
    step = pl.program_id(0)
    width = GQA_GROUP * BLOCK
    zero_q = jnp.zeros((HEAD_DIM, width), BF16)
    zero_p = jnp.zeros((META_ROW0, width), BF16)
    ones_band = jnp.ones((ONES_ROWS, BAND_KEYS), BF16)
    ones_meta = jnp.ones((ONES_ROWS, BLOCK), BF16)

    def window(blk, before, own, after, axis):
        parts = []
        for j in range(blk - 1, blk + 2):
            sel = [slice(None), slice(None)]
            if j < 0:
                parts.append(before)
            elif j >= ATTN_BLOCKS:
                parts.append(after)
            else:
                sel[axis] = slice(BLOCK * j, BLOCK * (j + 1))
                parts.append(own[tuple(sel)])
        return parts

    def scores(blk, k):
        pair = slice(BLOCK * (k // 2), BLOCK * (k // 2 + 1))
        band = window(blk, kp_ref[:, pair], ko_ref[:, pair], kn_ref[:, pair], 0)
        kc = jnp.concatenate(band + [kh_refs[blk][META_ROW0:BLOCK, pair]], axis=0)
        q0 = GQA_GROUP * HEAD_DIM * k
        qs = slice(BLOCK * blk, BLOCK * (blk + 1))
        qt4 = jnp.concatenate([qt_ref[q0 + HEAD_DIM * j:q0 + HEAD_DIM * (j + 1), qs]
                               for j in range(GQA_GROUP)], axis=1)
        rhs = jnp.concatenate([qt4, zero_q] if k % 2 == 0 else [zero_q, qt4], axis=0)
        return _dot(kc, rhs) + bias_ref[var_t[ATTN_BLOCKS * step + blk], k]

    def finish(blk, k, s):
        sink_row = jnp.concatenate([jnp.full((1, BLOCK), sink_ref[GQA_GROUP * k + j] * LOG2E, F32)
                                    for j in range(GQA_GROUP)], axis=1)
        m = jnp.maximum(jnp.max(s, axis=0, keepdims=True), sink_row)
        pb = jnp.exp2(s - m).astype(BF16)
        vs = slice(HEAD_DIM * k, HEAD_DIM * (k + 1))
        band = window(blk, vp_ref[vs, :], vo_ref[vs, :], vn_ref[vs, :], 1)
        vc = jnp.concatenate([jnp.concatenate(band, axis=1), ones_band], axis=0)
        vh = jnp.concatenate([vh_refs[blk][vs, :], ones_meta], axis=0)
        p_meta = jnp.concatenate([zero_p, pb[BAND_KEYS:KEYS]], axis=0)
        acc = _dot(vc, pb[0:BAND_KEYS]) + _dot(vh, p_meta)
        denom = acc[HEAD_DIM:HEAD_DIM + 1] + jnp.exp2(sink_row - m)
        ot = acc[0:HEAD_DIM] / denom
        rows = slice(BLOCK * blk, BLOCK * (blk + 1))
        for j in range(GQA_GROUP // 2):
            two = jnp.concatenate([ot[:, 2 * BLOCK * j:2 * BLOCK * j + BLOCK],
                                   ot[:, 2 * BLOCK * j + BLOCK:2 * BLOCK * (j + 1)]], axis=0)
            c0 = GQA_GROUP * HEAD_DIM * k + BLOCK * j
            o_ref[rows, c0:c0 + BLOCK] = two.T.astype(BF16)

    units = [(blk, k) for blk in range(ATTN_BLOCKS) for k in range(N_KV_HEADS)]
    s_next = scores(*units[0])
    for idx, unit in enumerate(units):
        s_cur = s_next
        if idx + 1 < len(units):
            s_next = scores(*units[idx + 1])
        finish(*unit, s_cur)


def _attention(qt, kk, vt, sink, bias, lay):
    t = kk.shape[0]
    nb = lay["nb"]
    n = ATTN_BLOCKS
    kspec = lambda rows, sel: pl.BlockSpec((rows, KV_WIDTH), sel)
    vspec = lambda cols, sel: pl.BlockSpec((KV_WIDTH, cols), sel)
    before = lambda i: jnp.maximum(n * i - 1, 0)
    after = lambda i: jnp.minimum(n * i + n, nb - 1)
    k_specs = [kspec(BLOCK, lambda i, m, v: (before(i), 0)),
               kspec(n * BLOCK, lambda i, m, v: (i, 0)),
               kspec(BLOCK, lambda i, m, v: (after(i), 0))]
    k_specs += [kspec(BLOCK, lambda i, m, v, j=j: (m[n * i + j], 0)) for j in range(n)]
    v_specs = [vspec(BLOCK, lambda i, m, v: (0, before(i))),
               vspec(n * BLOCK, lambda i, m, v: (0, i)),
               vspec(BLOCK, lambda i, m, v: (0, after(i)))]
    v_specs += [vspec(BLOCK, lambda i, m, v, j=j: (0, m[n * i + j])) for j in range(n)]
    grid_spec = pltpu.PrefetchScalarGridSpec(
        num_scalar_prefetch=2,
        grid=(nb // n,),
        in_specs=[pl.BlockSpec(memory_space=pltpu.SMEM),
                  pl.BlockSpec((D_MODEL, n * BLOCK), lambda i, m, v: (0, i))] + k_specs + v_specs
                 + [_const_spec((len(BLOCK_FLAG_VARIANTS), N_KV_HEADS, KEYS, GQA_GROUP * BLOCK),
                                lambda i, m, v: (0, 0, 0, 0))],
        out_specs=pl.BlockSpec((n * BLOCK, D_MODEL), lambda i, m, v: (i, 0)),
    )
    return pl.pallas_call(
        _attn_kernel,
        grid_spec=grid_spec,
        out_shape=jax.ShapeDtypeStruct((t, D_MODEL), BF16),
        compiler_params=pltpu.CompilerParams(vmem_limit_bytes=VMEM_LIMIT),
        name="band_attn",
    )(lay["meta"], lay["variant"], sink, qt, *([kk] * (3 + n)), *([vt] * (3 + n)), bias)


def _pool_diff(u_ext, t_ext, l_ext):
    return [_pool_group(u_ext, t_ext, l_ext, gi) for gi in range(len(POOL_WINDOWS))]


def _pool_group(u_ext, t_ext, l_ext, gi):
    w = POOL_WINDOWS[gi]
    n = u_ext.shape[0]
    rows = n - 2 * HALO
    valid = (t_ext >= 0) & (t_ext < l_ext)
    t = t_ext[HALO:HALO + rows]
    ln = l_ext[HALO:HALO + rows]
    fwd = lambda x, k: x + pltpu.roll(x, n - k, 0)
    x = jnp.where(valid, u_ext[:, gi * POOL_GROUP_WIDTH:(gi + 1) * POOL_GROUP_WIDTH], 0.0)
    acc, k = x, 1
    while 2 * k <= min(w, HALO):
        acc = fwd(acc, k)
        k *= 2
    if w <= HALO:
        win = pltpu.roll(acc, w // 2, 0)
    else:
        win = acc + pltpu.roll(acc, HALO, 0)
    cnt = jnp.minimum(t + w // 2, ln) - jnp.maximum(t - w // 2, 0)
    cnt = jnp.maximum(cnt, 1).astype(F32)
    return win[HALO:HALO + rows] / cnt - x[HALO:HALO + rows]


def _mix_kernel(t0_t, ln_t,
                attn_ref, u_ref, up_ref, un_ref, g_ref, h_ref,
                wa_ref, wp_ref, ps_ref, wb_ref, wo_ref, lg_ref, lb_ref, o_ref, *, alpha):
    b0 = pl.program_id(0) * BLOCKS_PER_TILE

    def rows_of(entry, first, count):
        r = lax.broadcasted_iota(jnp.int32, (count, POOL_GROUP_WIDTH), 0) + first
        return r + t0_t[entry], jnp.zeros((count, POOL_GROUP_WIDTH), jnp.int32) + ln_t[entry]

    parts = [rows_of(b0, BLOCK - HALO, HALO)]
    parts += [rows_of(b0 + 1 + j, 0, BLOCK) for j in range(BLOCKS_PER_TILE)]
    parts += [rows_of(b0 + 1 + BLOCKS_PER_TILE, 0, HALO)]
    t_ext = jnp.concatenate([p[0] for p in parts], axis=0)
    l_ext = jnp.concatenate([p[1] for p in parts], axis=0)
    u_ext = jnp.concatenate([up_ref[...], u_ref[...], un_ref[...]], axis=0)

    def pool(hf):
        ext = slice(HALF_TILE * hf, HALF_TILE * (hf + 1) + 2 * HALO)
        return _pool_diff(u_ext[ext], t_ext[ext], l_ext[ext])

    def attn_branch(hf):
        return _dot(attn_ref[_half(hf), :], wa_ref[...])

    def pool_branch(diffs):
        y = jnp.concatenate([_dot(d.astype(BF16), wp_ref[gi]) for gi, d in enumerate(diffs)], axis=1)
        return _dot((y * ps_ref[...]).astype(BF16), wb_ref[...])

    def gate(hf, ya, yb):
        g = g_ref[_half(hf), :].astype(F32)
        return (g[:, 0:D_MODEL] * ya + g[:, D_MODEL:2 * D_MODEL] * yb).astype(BF16)

    def norm(hf, mixed):
        o_ref[_half(hf), :] = _layer_norm(alpha * h_ref[_half(hf), :] + mixed, lg_ref[...], lb_ref[...])

    ya_a = attn_branch(0)
    pool_a = pool(0)
    ya_b = attn_branch(1)
    pool_b = pool(1)
    yb_a = pool_branch(pool_a)
    yb_b = pool_branch(pool_b)
    gated_a = gate(0, ya_a, yb_a)
    mixed_a = _dot(gated_a, wo_ref[...])
    gated_b = gate(1, ya_b, yb_b)
    mixed_b = _dot(gated_b, wo_ref[...])
    norm(0, mixed_a)
    norm(1, mixed_b)


def _mix(attn, u, g, h, weights, layer, lay, alpha):
    t = h.shape[0]
    n_halo = t // HALO
    per = TOKEN_TILE // HALO
    row = lambda i, a, b: (i, 0)
    vec = lambda i, a, b: (layer, 0, 0)
    mat = lambda r, c: _const_spec((None, r, c), vec)
    grid_spec = pltpu.PrefetchScalarGridSpec(
        num_scalar_prefetch=2,
        grid=(t // TOKEN_TILE,),
        in_specs=[pl.BlockSpec((TOKEN_TILE, D_MODEL), row),
                  pl.BlockSpec((TOKEN_TILE, POOL_WIDTH), row),
                  pl.BlockSpec((HALO, POOL_WIDTH), lambda i, a, b: (jnp.maximum(i * per - 1, 0), 0)),
                  pl.BlockSpec((HALO, POOL_WIDTH), lambda i, a, b: (jnp.minimum((i + 1) * per, n_halo - 1), 0)),
                  pl.BlockSpec((TOKEN_TILE, GATE_WIDTH), row),
                  pl.BlockSpec((TOKEN_TILE, D_MODEL), row),
                  mat(D_MODEL, D_MODEL),
                  _const_spec((None, len(POOL_WINDOWS), POOL_GROUP_WIDTH, POOL_GROUP_WIDTH),
                              lambda i, a, b: (layer, 0, 0, 0)),
                  mat(1, POOL_WIDTH), mat(POOL_WIDTH, D_MODEL), mat(D_MODEL, D_MODEL),
                  mat(1, D_MODEL), mat(1, D_MODEL)],
        out_specs=pl.BlockSpec((TOKEN_TILE, D_MODEL), row),
    )
    return pl.pallas_call(
        functools.partial(_mix_kernel, alpha=alpha),
        grid_spec=grid_spec,
        out_shape=jax.ShapeDtypeStruct((t, D_MODEL), F32),
        compiler_params=pltpu.CompilerParams(vmem_limit_bytes=VMEM_LIMIT),
        name="mix_ln",
    )(lay["t0"], lay["ln"], attn, u, u, u, g, h, *weights)


def _mlp_body(x_ref, w1_ref, b1_ref, w2_ref, b2_ref, lg_ref, lb_ref, store, alpha):
    def ff(xb, chunks):
        acc = None
        for c in chunks:
            cs = slice(c * FF_CHUNK, (c + 1) * FF_CHUNK)
            a = jnp.maximum(_dot(xb, w1_ref[:, cs]) + b1_ref[:, cs], 0.0)
            part = _dot((a * a).astype(BF16), w2_ref[cs, :])
            acc = part if acc is None else acc + part
        return acc

    def norm(x, acc):
        return _layer_norm(alpha * x + (acc + b2_ref[...]), lg_ref[...], lb_ref[...])

    first, rest = (0,), tuple(range(1, D_FF // FF_CHUNK))
    x_a = x_ref[_half(0), :]
    x_b = x_ref[_half(1), :]
    xb_a, xb_b = x_a.astype(BF16), x_b.astype(BF16)
    acc_a = ff(xb_a, first + rest)
    acc_b = ff(xb_b, first)
    store(0, norm(x_a, acc_a))
    acc_b = acc_b + ff(xb_b, rest)
    store(1, norm(x_b, acc_b))


def _mlp_kernel(kind_t, dsta_t, dstb_t, *refs, alpha):
    del kind_t, dsta_t, dstb_t
    ins, o_ref = refs[:-1], refs[-1]

    def store(hf, value):
        o_ref[_half(hf), :] = value

    _mlp_body(*ins, store, alpha)


def _mlp_final_kernel(kind_t, dsta_t, dstb_t, *refs, alpha):
    ins, (ya_ref, yb_ref, buf_ref, sem_ref) = refs[:-4], refs[-4:]
    step = pl.program_id(0)
    slot = step % 2

    def store(hf, value):
        buf_ref[slot, _half(hf), :] = value

    _mlp_body(*ins, store, alpha)

    def copies(of_step, of_slot, act):
        for j in range(BLOCKS_PER_TILE):
            b = of_step * BLOCKS_PER_TILE + j
            for gi, (y_ref, dst_t) in enumerate(((ya_ref, dsta_t), (yb_ref, dstb_t))):
                @pl.when(kind_t[b] == gi)
                def _():
                    src = buf_ref.at[of_slot, pl.ds(BLOCK * j, BLOCK), :]
                    dst = y_ref.at[pl.ds(pl.multiple_of(dst_t[b] * BLOCK, BLOCK), BLOCK), :]
                    act(pltpu.make_async_copy(src, dst, sem_ref.at[of_slot, j]))

    copies(step, slot, lambda c: c.start())

    @pl.when(step > 0)
    def _():
        copies(step - 1, 1 - slot, lambda c: c.wait())

    @pl.when(step == pl.num_programs(0) - 1)
    def _():
        copies(step, slot, lambda c: c.wait())


def _mlp(x, weights, layer, lay, alpha, out_rows=None):
    t = x.shape[0]
    row = lambda i, *_: (i, 0)
    vec = lambda i, *_: (layer, 0, 0)
    mat = lambda r, c: _const_spec((None, r, c), vec)
    in_specs = [pl.BlockSpec((TOKEN_TILE, D_MODEL), row),
                mat(D_MODEL, D_FF), mat(1, D_FF), mat(D_FF, D_MODEL), mat(1, D_MODEL),
                mat(1, D_MODEL), mat(1, D_MODEL)]
    if out_rows is None:
        body, name = _mlp_kernel, "mlp_ln"
        out_specs = pl.BlockSpec((TOKEN_TILE, D_MODEL), row)
        out_shape = jax.ShapeDtypeStruct((t, D_MODEL), F32)
        scratch = []
    else:
        body, name = _mlp_final_kernel, "mlp_ln_out"
        out_specs = [pl.BlockSpec(memory_space=pl.ANY)] * 2
        out_shape = [jax.ShapeDtypeStruct((r, D_MODEL), F32) for r in out_rows]
        scratch = [pltpu.VMEM((2, TOKEN_TILE, D_MODEL), F32),
                   pltpu.SemaphoreType.DMA((2, BLOCKS_PER_TILE))]
    grid_spec = pltpu.PrefetchScalarGridSpec(
        num_scalar_prefetch=3, grid=(t // TOKEN_TILE,),
        in_specs=in_specs, out_specs=out_specs, scratch_shapes=scratch)
    return pl.pallas_call(
        functools.partial(body, alpha=alpha),
        grid_spec=grid_spec,
        out_shape=out_shape,
        compiler_params=pltpu.CompilerParams(vmem_limit_bytes=VMEM_LIMIT,
                                             dimension_semantics=("arbitrary",)),
        name=name,
    )(lay["kind"], lay["src"][0], lay["src"][1], x, *weights)


def kernel(x_prompt, x_sample, meta_tokens, ln_emb_g, ln_emb_b, w_in, sink, w_pool, pool_scale, w_bo_attn,
           w_bo_pool, w_out, ln1_g, ln1_b, w_mlp1, b_mlp1, w_mlp2, b_mlp2, ln2_g, ln2_b):
    depth = w_in.shape[0]
    alpha = float((2 * depth) ** 0.25)
    groups = (x_prompt, x_sample)
    lay = _layout([x.shape[:2] for x in groups])
    header = jnp.concatenate([jnp.zeros((META_ROW0, D_MODEL), F32), meta_tokens.astype(F32)], axis=0)

    row3 = lambda a: a.reshape(a.shape[0], 1, a.shape[-1])
    w_in_b = w_in.astype(BF16)
    w_qvt = _qv_transposed(w_in)
    mix_w = (w_bo_attn.astype(BF16), w_pool.astype(BF16), row3(pool_scale), w_bo_pool.astype(BF16),
             w_out.astype(BF16), row3(ln1_g), row3(ln1_b))
    mlp_w = (w_mlp1.astype(BF16), row3(b_mlp1), w_mlp2.astype(BF16), row3(b_mlp2), row3(ln2_g), row3(ln2_b))
    bias = jnp.asarray(_alibi_bias_table())

    h = _embed(x_prompt.reshape(-1, D_MODEL), x_sample.reshape(-1, D_MODEL), header,
               ln_emb_g.reshape(1, -1), ln_emb_b.reshape(1, -1), lay)
    out_rows = [x.shape[0] * x.shape[1] for x in groups]
    for l in range(depth):
        qt, kk, vt, u, g = _inproj(h, w_in_b, w_qvt, l)
        attn = _attention(qt, kk, vt, sink[l].astype(F32), bias, lay)
        x1 = _mix(attn, u, g, h, mix_w, l, lay, alpha)
        h = _mlp(x1, mlp_w, l, lay, alpha, out_rows if l == depth - 1 else None)
    return tuple(y.reshape(x.shape) for y, x in zip(h, groups))
```

```python
import functools

import numpy as np
import jax
import jax.numpy as jnp
from jax import lax
from jax.experimental import pallas as pl
from jax.experimental.pallas import tpu as pltpu

D_MODEL = 1024
N_META = 16
N_HEADS = 16
N_KV_HEADS = 4
HEAD_DIM = 64
GQA_GROUP = N_HEADS // N_KV_HEADS
WINDOW = 128
BLOCK = 128
POOL_WINDOWS = (2, 4, 8, 16)
POOL_GROUP_WIDTH = 128
POOL_WIDTH = 512
KV_WIDTH = N_KV_HEADS * HEAD_DIM
GATE_WIDTH = 2 * D_MODEL
D_FF = 4 * D_MODEL
FF_CHUNK = 1024
LN_EPS = 1e-5
HALO = 8
META_ROW0 = BLOCK - N_META
BAND_KEYS = 3 * BLOCK
KEYS = BAND_KEYS + N_META
NEG = -1e30
LOG2E = 1.4426950408889634
ONES_ROWS = 16
BLOCK_FLAG_VARIANTS = (7, 6, 3, 4, 2)
TOKEN_TILE = 1024
HALF_TILE = TOKEN_TILE // 2
BLOCKS_PER_TILE = TOKEN_TILE // BLOCK
EMBED_BLOCKS = 4
ATTN_BLOCKS = 8
VMEM_LIMIT = 56 * 1024 * 1024

F32 = jnp.float32
BF16 = jnp.bfloat16
NT_DIMS = (((1,), (1,)), ((), ()))


def _layout(group_shapes):
    seqs = []
    n = 0
    for gi, (bsz, s) in enumerate(group_shapes):
        assert s % BLOCK == 0 and s >= BLOCK
        for bi in range(bsz):
            seqs.append((gi, bi, s // BLOCK, n))
            n += 1 + s // BLOCK
    nb = -(-n // BLOCKS_PER_TILE) * BLOCKS_PER_TILE
    meta_i = np.arange(nb, dtype=np.int32)
    flags = np.full(nb, 2, dtype=np.int32)
    t0 = np.zeros(nb + 2, dtype=np.int32)
    ln = np.zeros(nb + 2, dtype=np.int32)
    kind = np.full(nb, -2, dtype=np.int32)
    src = np.zeros((len(group_shapes), nb), dtype=np.int32)
    for gi, bi, nreal, b0 in seqs:
        for j in range(nreal + 1):
            b = b0 + j
            meta_i[b] = b0
            t0[b + 1] = -META_ROW0 + BLOCK * j
            ln[b + 1] = N_META + BLOCK * nreal
            flags[b] = (0 if j == 0 else 2) | (1 if j >= 2 else 0) | (4 if j < nreal else 0)
            kind[b] = -1 if j == 0 else gi
            if j > 0:
                src[gi, b:] = bi * nreal + (j - 1)
    variant = np.asarray([BLOCK_FLAG_VARIANTS.index(int(f)) for f in flags], dtype=np.int32)
    return dict(nb=nb, meta=meta_i, variant=variant, t0=t0, ln=ln, kind=kind, src=src)


def _alibi_bias_table():
    nv = len(BLOCK_FLAG_VARIANTS)
    slopes = 2.0 ** (-8.0 * np.arange(1, N_HEADS + 1) / N_HEADS)
    key = np.arange(BAND_KEYS)[:, None]
    qry = np.arange(BLOCK)[None, :]
    rel = np.abs(key - BLOCK - qry).astype(np.float32)[None, :, None, :]
    scaled = (slopes * LOG2E).astype(np.float32).reshape(N_KV_HEADS, 1, GQA_GROUP, 1)
    band = np.where(rel <= WINDOW, -(scaled * rel), np.float32(NEG))
    flags = np.asarray(BLOCK_FLAG_VARIANTS)[:, None]
    usable = (flags >> (np.arange(BAND_KEYS)[None, :] // BLOCK)) & 1
    band = np.where(usable[:, None, :, None, None] != 0, band[None], np.float32(NEG))
    band = band.reshape(nv, N_KV_HEADS, BAND_KEYS, GQA_GROUP * BLOCK)
    meta = np.zeros((nv, N_KV_HEADS, N_META, GQA_GROUP * BLOCK), np.float32)
    return np.concatenate([band, meta], axis=2).astype(np.float32)


def _const_spec(shape, index_map):
    return pl.BlockSpec(shape, index_map, pipeline_mode=pl.Buffered(1))


def _layer_norm(x, g, b):
    mu = jnp.mean(x, axis=-1, keepdims=True)
    xc = x - mu
    var = jnp.mean(xc * xc, axis=-1, keepdims=True)
    return xc * lax.rsqrt(var + LN_EPS) * g + b


def _dot(a, b):
    return jnp.dot(a, b, preferred_element_type=F32)


def _half(hf):
    return slice(HALF_TILE * hf, HALF_TILE * (hf + 1))


def _embed_kernel(kind_t, srca_t, srcb_t, *refs):
    del srca_t, srcb_t
    xa_refs, xb_refs = refs[:EMBED_BLOCKS], refs[EMBED_BLOCKS:2 * EMBED_BLOCKS]
    hdr_ref, g_ref, b_ref, o_ref = refs[2 * EMBED_BLOCKS:]
    for j in range(EMBED_BLOCKS):
        kind = kind_t[pl.program_id(0) * EMBED_BLOCKS + j]
        x = jnp.where(kind == 0, xa_refs[j][...], jnp.where(kind == 1, xb_refs[j][...], 0.0))
        x = jnp.where(kind == -1, hdr_ref[...], x)
        o_ref[BLOCK * j:BLOCK * (j + 1), :] = _layer_norm(x, g_ref[...], b_ref[...])


def _embed(xa, xb, header, g, b, lay):
    blk = lambda sel: pl.BlockSpec((BLOCK, D_MODEL), sel)
    from_a = [blk(lambda i, k, sa, sb, j=j: (sa[i * EMBED_BLOCKS + j], 0)) for j in range(EMBED_BLOCKS)]
    from_b = [blk(lambda i, k, sa, sb, j=j: (sb[i * EMBED_BLOCKS + j], 0)) for j in range(EMBED_BLOCKS)]
    const = lambda i, k, sa, sb: (0, 0)
    grid_spec = pltpu.PrefetchScalarGridSpec(
        num_scalar_prefetch=3,
        grid=(lay["nb"] // EMBED_BLOCKS,),
        in_specs=from_a + from_b + [blk(const), pl.BlockSpec((1, D_MODEL), const),
                                    pl.BlockSpec((1, D_MODEL), const)],
        out_specs=pl.BlockSpec((EMBED_BLOCKS * BLOCK, D_MODEL), lambda i, k, sa, sb: (i, 0)),
    )
    return pl.pallas_call(
        _embed_kernel,
        grid_spec=grid_spec,
        out_shape=jax.ShapeDtypeStruct((lay["nb"] * BLOCK, D_MODEL), F32),
        name="embed_ln",
    )(lay["kind"], lay["src"][0], lay["src"][1], *([xa] * EMBED_BLOCKS), *([xb] * EMBED_BLOCKS), header, g, b)


def _qvt_kernel(w_ref, o_ref):
    o_ref[...] = w_ref[...].T.astype(BF16)


def _qv_transposed(w_in):
    depth = w_in.shape[0]
    q_blocks = D_MODEL // KV_WIDTH
    v_block0 = (D_MODEL + KV_WIDTH) // KV_WIDTH
    n_blocks = q_blocks + 1
    src_col = lambda c: jnp.where(c < q_blocks, c, c - q_blocks + v_block0)
    return pl.pallas_call(
        _qvt_kernel,
        grid=(depth, n_blocks),
        in_specs=[pl.BlockSpec((None, D_MODEL, KV_WIDTH), lambda l, c: (l, 0, src_col(c)))],
        out_specs=pl.BlockSpec((None, KV_WIDTH, D_MODEL), lambda l, c: (l, c, 0)),
        out_shape=jax.ShapeDtypeStruct((depth, D_MODEL + KV_WIDTH, D_MODEL), BF16),
        name="qv_transpose",
    )(w_in)


def _inproj_kernel(h_ref, w_ref, wt_ref, qt_ref, k_ref, vt_ref, g_ref):
    k0 = D_MODEL
    g0 = D_MODEL + 2 * KV_WIDTH + POOL_WIDTH

    def project(hf):
        rows = _half(hf)
        x = h_ref[rows, :].astype(BF16)
        qt = lax.dot_general(wt_ref[0:D_MODEL, :], x, NT_DIMS, preferred_element_type=F32)
        qt_ref[:, rows] = (qt * (HEAD_DIM ** -0.5 * LOG2E)).astype(BF16)
        vt = lax.dot_general(wt_ref[D_MODEL:D_MODEL + KV_WIDTH, :], x, NT_DIMS, preferred_element_type=F32)
        vt_ref[:, rows] = vt.astype(BF16)
        k_ref[rows, :] = _dot(x, w_ref[:, k0:k0 + KV_WIDTH]).astype(BF16)
        return _dot(x, w_ref[:, g0:g0 + GATE_WIDTH])

    gates_a = project(0)
    gates_b = project(1)
    g_ref[_half(0), :] = jax.nn.sigmoid(gates_a).astype(BF16)
    g_ref[_half(1), :] = jax.nn.sigmoid(gates_b).astype(BF16)


def _inproj(h, w_in, w_qvt, layer):
    t = h.shape[0]
    row = lambda i: (i, 0)
    col = lambda i: (0, i)
    return pl.pallas_call(
        _inproj_kernel,
        grid=(t // TOKEN_TILE,),
        in_specs=[pl.BlockSpec((TOKEN_TILE, D_MODEL), row),
                  _const_spec((None, D_MODEL, 4 * D_MODEL), lambda i: (layer, 0, 0)),
                  _const_spec((None, D_MODEL + KV_WIDTH, D_MODEL), lambda i: (layer, 0, 0))],
        out_specs=[pl.BlockSpec((D_MODEL, TOKEN_TILE), col),
                   pl.BlockSpec((TOKEN_TILE, KV_WIDTH), row),
                   pl.BlockSpec((KV_WIDTH, TOKEN_TILE), col),
                   pl.BlockSpec((TOKEN_TILE, GATE_WIDTH), row)],
        out_shape=[jax.ShapeDtypeStruct((D_MODEL, t), BF16),
                   jax.ShapeDtypeStruct((t, KV_WIDTH), BF16),
                   jax.ShapeDtypeStruct((KV_WIDTH, t), BF16),
                   jax.ShapeDtypeStruct((t, GATE_WIDTH), BF16)],
        compiler_params=pltpu.CompilerParams(vmem_limit_bytes=VMEM_LIMIT),
        name="in_proj",
    )(h, w_in, w_qvt)


def _attn_kernel(meta_t, var_t, sink_ref, qt_ref, *refs):
    del meta_t
    kp_ref, ko_ref, kn_ref = refs[0:3]
    kh_refs = refs[3:3 + ATTN_BLOCKS]
    vp_ref, vo_ref, vn_ref = refs[3 + ATTN_BLOCKS:6 + ATTN_BLOCKS]
    vh_refs = refs[6 + ATTN_BLOCKS:6 + 2 * ATTN_BLOCKS]
    bias_ref, o_ref = refs[6 + 2 * ATTN_BLOCKS:]
    step = pl.program_id(0)
    width = GQA_GROUP * BLOCK
    zero_q = jnp.zeros((HEAD_DIM, width), BF16)
    zero_p = jnp.zeros((META_ROW0, width), BF16)
    ones_band = jnp.ones((ONES_ROWS, BAND_KEYS), BF16)
    ones_meta = jnp.ones((ONES_ROWS, BLOCK), BF16)

    def window(blk, before, own, after, axis):
        parts = []
        for j in range(blk - 1, blk + 2):
            sel = [slice(None), slice(None)]
            if j < 0:
                parts.append(before)
            elif j >= ATTN_BLOCKS:
                parts.append(after)
            else:
                sel[axis] = slice(BLOCK * j, BLOCK * (j + 1))
                parts.append(own[tuple(sel)])
        return parts

    def scores(blk, k):
        pair = slice(BLOCK * (k // 2), BLOCK * (k // 2 + 1))
        band = window(blk, kp_ref[:, pair], ko_ref[:, pair], kn_ref[:, pair], 0)
        kc = jnp.concatenate(band + [kh_refs[blk][META_ROW0:BLOCK, pair]], axis=0)
        q0 = GQA_GROUP * HEAD_DIM * k
        qs = slice(BLOCK * blk, BLOCK * (blk + 1))
        qt4 = jnp.concatenate([qt_ref[q0 + HEAD_DIM * j:q0 + HEAD_DIM * (j + 1), qs]
                               for j in range(GQA_GROUP)], axis=1)
        rhs = jnp.concatenate([qt4, zero_q] if k % 2 == 0 else [zero_q, qt4], axis=0)
        return _dot(kc, rhs) + bias_ref[var_t[ATTN_BLOCKS * step + blk], k]

    def finish(blk, k, s):
        sink_row = jnp.concatenate([jnp.full((1, BLOCK), sink_ref[GQA_GROUP * k + j] * LOG2E, F32)
                                    for j in range(GQA_GROUP)], axis=1)
        m = jnp.maximum(jnp.max(s, axis=0, keepdims=True), sink_row)
        pb = jnp.exp2(s - m).astype(BF16)
        vs = slice(HEAD_DIM * k, HEAD_DIM * (k + 1))
        band = window(blk, vp_ref[vs, :], vo_ref[vs, :], vn_ref[vs, :], 1)
        vc = jnp.concatenate([jnp.concatenate(band, axis=1), ones_band], axis=0)
        vh = jnp.concatenate([vh_refs[blk][vs, :], ones_meta], axis=0)
        p_meta = jnp.concatenate([zero_p, pb[BAND_KEYS:KEYS]], axis=0)
        acc = _dot(vc, pb[0:BAND_KEYS]) + _dot(vh, p_meta)
        denom = acc[HEAD_DIM:HEAD_DIM + 1] + jnp.exp2(sink_row - m)
        ot = acc[0:HEAD_DIM] / denom
        rows = slice(BLOCK * blk, BLOCK * (blk + 1))
        for j in range(GQA_GROUP // 2):
            two = jnp.concatenate([ot[:, 2 * BLOCK * j:2 * BLOCK * j + BLOCK],
                                   ot[:, 2 * BLOCK * j + BLOCK:2 * BLOCK * (j + 1)]], axis=0)
            c0 = GQA_GROUP * HEAD_DIM * k + BLOCK * j
            o_ref[rows, c0:c0 + BLOCK] = two.T.astype(BF16)

    units = [(blk, k) for blk in range(ATTN_BLOCKS) for k in range(N_KV_HEADS)]
    s_next = scores(*units[0])
    for idx, unit in enumerate(units):
        s_cur = s_next
        if idx + 1 < len(units):
            s_next = scores(*units[idx + 1])
        finish(*unit, s_cur)


def _attention(qt, kk, vt, sink, bias, lay):
    t = kk.shape[0]
    nb = lay["nb"]
    n = ATTN_BLOCKS
    kspec = lambda rows, sel: pl.BlockSpec((rows, KV_WIDTH), sel)
    vspec = lambda cols, sel: pl.BlockSpec((KV_WIDTH, cols), sel)
    before = lambda i: jnp.maximum(n * i - 1, 0)
    after = lambda i: jnp.minimum(n * i + n, nb - 1)
    k_specs = [kspec(BLOCK, lambda i, m, v: (before(i), 0)),
               kspec(n * BLOCK, lambda i, m, v: (i, 0)),
               kspec(BLOCK, lambda i, m, v: (after(i), 0))]
    k_specs += [kspec(BLOCK, lambda i, m, v, j=j: (m[n * i + j], 0)) for j in range(n)]
    v_specs = [vspec(BLOCK, lambda i, m, v: (0, before(i))),
               vspec(n * BLOCK, lambda i, m, v: (0, i)),
               vspec(BLOCK, lambda i, m, v: (0, after(i)))]
    v_specs += [vspec(BLOCK, lambda i, m, v, j=j: (0, m[n * i + j])) for j in range(n)]
    grid_spec = pltpu.PrefetchScalarGridSpec(
        num_scalar_prefetch=2,
        grid=(nb // n,),
        in_specs=[pl.BlockSpec(memory_space=pltpu.SMEM),
                  pl.BlockSpec((D_MODEL, n * BLOCK), lambda i, m, v: (0, i))] + k_specs + v_specs
                 + [_const_spec((len(BLOCK_FLAG_VARIANTS), N_KV_HEADS, KEYS, GQA_GROUP * BLOCK),
                                lambda i, m, v: (0, 0, 0, 0))],
        out_specs=pl.BlockSpec((n * BLOCK, D_MODEL), lambda i, m, v: (i, 0)),
    )
    return pl.pallas_call(
        _attn_kernel,
        grid_spec=grid_spec,
        out_shape=jax.ShapeDtypeStruct((t, D_MODEL), BF16),
        compiler_params=pltpu.CompilerParams(vmem_limit_bytes=VMEM_LIMIT),
        name="band_attn",
    )(lay["meta"], lay["variant"], sink, qt, *([kk] * (3 + n)), *([vt] * (3 + n)), bias)


def _pool_diff(u_ext, t_ext, l_ext):
    return [_pool_group(u_ext, t_ext, l_ext, gi) for gi in range(len(POOL_WINDOWS))]


def _pool_group(u_ext, t_ext, l_ext, gi):
    w = POOL_WINDOWS[gi]
    n = u_ext.shape[0]
    rows = n - 2 * HALO
    valid = (t_ext >= 0) & (t_ext < l_ext)
    t = t_ext[HALO:HALO + rows]
    ln = l_ext[HALO:HALO + rows]
    fwd = lambda x, k: x + pltpu.roll(x, n - k, 0)
    x = jnp.where(valid, u_ext[:, gi * POOL_GROUP_WIDTH:(gi + 1) * POOL_GROUP_WIDTH], 0.0)
    acc, k = x, 1
    while 2 * k <= min(w, HALO):
        acc = fwd(acc, k)
        k *= 2
    if w <= HALO:
        win = pltpu.roll(acc, w // 2, 0)
    else:
        win = acc + pltpu.roll(acc, HALO, 0)
    cnt = jnp.minimum(t + w // 2, ln) - jnp.maximum(t - w // 2, 0)
    cnt = jnp.maximum(cnt, 1).astype(F32)
    return win[HALO:HALO + rows] / cnt - x[HALO:HALO + rows]


def _mix_kernel(t0_t, ln_t,
                attn_ref, g_ref, h_ref, hp_ref, hn_ref,
                wu_ref, wa_ref, wp_ref, ps_ref, wb_ref, wo_ref, lg_ref, lb_ref, o_ref, *, alpha):
    b0 = pl.program_id(0) * BLOCKS_PER_TILE

    def rows_of(entry, first, count):
        r = lax.broadcasted_iota(jnp.int32, (count, POOL_GROUP_WIDTH), 0) + first
        return r + t0_t[entry], jnp.zeros((count, POOL_GROUP_WIDTH), jnp.int32) + ln_t[entry]

    parts = [rows_of(b0, BLOCK - HALO, HALO)]
    parts += [rows_of(b0 + 1 + j, 0, BLOCK) for j in range(BLOCKS_PER_TILE)]
    parts += [rows_of(b0 + 1 + BLOCKS_PER_TILE, 0, HALO)]
    t_ext = jnp.concatenate([p[0] for p in parts], axis=0)
    l_ext = jnp.concatenate([p[1] for p in parts], axis=0)

    u_halves = [_dot(h_ref[_half(hf), :].astype(BF16), wu_ref[...]) for hf in range(2)]
    h_halo = jnp.concatenate([hp_ref[...], hn_ref[...]], axis=0).astype(BF16)
    u_halo = _dot(h_halo, wu_ref[...])
    u_ext = jnp.concatenate([u_halo[0:HALO]] + u_halves + [u_halo[HALO:2 * HALO]], axis=0)

    def pool(hf):
        ext = slice(HALF_TILE * hf, HALF_TILE * (hf + 1) + 2 * HALO)
        return _pool_diff(u_ext[ext], t_ext[ext], l_ext[ext])

    def attn_branch(hf):
        return _dot(attn_ref[_half(hf), :], wa_ref[...])

    def pool_branch(diffs):
        y = jnp.concatenate([_dot(d.astype(BF16), wp_ref[gi]) for gi, d in enumerate(diffs)], axis=1)
        return _dot((y * ps_ref[...]).astype(BF16), wb_ref[...])

    def gate(hf, ya, yb):
        g = g_ref[_half(hf), :].astype(F32)
        return (g[:, 0:D_MODEL] * ya + g[:, D_MODEL:2 * D_MODEL] * yb).astype(BF16)

    def norm(hf, mixed):
        o_ref[_half(hf), :] = _layer_norm(alpha * h_ref[_half(hf), :] + mixed, lg_ref[...], lb_ref[...])

    ya_a = attn_branch(0)
    pool_a = pool(0)
    ya_b = attn_branch(1)
    pool_b = pool(1)
    yb_a = pool_branch(pool_a)
    yb_b = pool_branch(pool_b)
    gated_a = gate(0, ya_a, yb_a)
    mixed_a = _dot(gated_a, wo_ref[...])
    gated_b = gate(1, ya_b, yb_b)
    mixed_b = _dot(gated_b, wo_ref[...])
    norm(0, mixed_a)
    norm(1, mixed_b)


def _mix(attn, g, h, w_in, weights, layer, lay, alpha):
    t = h.shape[0]
    n_halo = t // HALO
    per = TOKEN_TILE // HALO
    u_block = (D_MODEL + 2 * KV_WIDTH) // POOL_WIDTH
    row = lambda i, a, b: (i, 0)
    vec = lambda i, a, b: (layer, 0, 0)
    mat = lambda r, c: _const_spec((None, r, c), vec)
    grid_spec = pltpu.PrefetchScalarGridSpec(
        num_scalar_prefetch=2,
        grid=(t // TOKEN_TILE,),
        in_specs=[pl.BlockSpec((TOKEN_TILE, D_MODEL), row),
                  pl.BlockSpec((TOKEN_TILE, GATE_WIDTH), row),
                  pl.BlockSpec((TOKEN_TILE, D_MODEL), row),
                  pl.BlockSpec((HALO, D_MODEL), lambda i, a, b: (jnp.maximum(i * per - 1, 0), 0)),
                  pl.BlockSpec((HALO, D_MODEL), lambda i, a, b: (jnp.minimum((i + 1) * per, n_halo - 1), 0)),
                  _const_spec((None, D_MODEL, POOL_WIDTH), lambda i, a, b: (layer, 0, u_block)),
                  mat(D_MODEL, D_MODEL),
                  _const_spec((None, len(POOL_WINDOWS), POOL_GROUP_WIDTH, POOL_GROUP_WIDTH),
                              lambda i, a, b: (layer, 0, 0, 0)),
                  mat(1, POOL_WIDTH), mat(POOL_WIDTH, D_MODEL), mat(D_MODEL, D_MODEL),
                  mat(1, D_MODEL), mat(1, D_MODEL)],
        out_specs=pl.BlockSpec((TOKEN_TILE, D_MODEL), row),
    )
    return pl.pallas_call(
        functools.partial(_mix_kernel, alpha=alpha),
        grid_spec=grid_spec,
        out_shape=jax.ShapeDtypeStruct((t, D_MODEL), F32),
        compiler_params=pltpu.CompilerParams(vmem_limit_bytes=VMEM_LIMIT),
        name="mix_ln",
    )(lay["t0"], lay["ln"], attn, g, h, h, h, w_in, *weights)


def _mlp_body(x_ref, w1_ref, b1_ref, w2_ref, b2_ref, lg_ref, lb_ref, store, alpha):
    def ff(xb, chunks):
        acc = None
        for c in chunks:
            cs = slice(c * FF_CHUNK, (c + 1) * FF_CHUNK)
            a = jnp.maximum(_dot(xb, w1_ref[:, cs]) + b1_ref[:, cs], 0.0)
            part = _dot((a * a).astype(BF16), w2_ref[cs, :])
            acc = part if acc is None else acc + part
        return acc

    def norm(x, acc):
        return _layer_norm(alpha * x + (acc + b2_ref[...]), lg_ref[...], lb_ref[...])

    first, rest = (0,), tuple(range(1, D_FF // FF_CHUNK))
    x_a = x_ref[_half(0), :]
    x_b = x_ref[_half(1), :]
    xb_a, xb_b = x_a.astype(BF16), x_b.astype(BF16)
    acc_a = ff(xb_a, first + rest)
    acc_b = ff(xb_b, first)
    store(0, norm(x_a, acc_a))
    acc_b = acc_b + ff(xb_b, rest)
    store(1, norm(x_b, acc_b))


def _mlp_kernel(kind_t, dsta_t, dstb_t, *refs, alpha):
    del kind_t, dsta_t, dstb_t
    ins, o_ref = refs[:-1], refs[-1]

    def store(hf, value):
        o_ref[_half(hf), :] = value

    _mlp_body(*ins, store, alpha)


def _mlp_final_kernel(kind_t, dsta_t, dstb_t, *refs, alpha):
    ins, (ya_ref, yb_ref, buf_ref, sem_ref) = refs[:-4], refs[-4:]
    step = pl.program_id(0)
    slot = step % 2

    def store(hf, value):
        buf_ref[slot, _half(hf), :] = value

    _mlp_body(*ins, store, alpha)

    def copies(of_step, of_slot, act):
        for j in range(BLOCKS_PER_TILE):
            b = of_step * BLOCKS_PER_TILE + j
            for gi, (y_ref, dst_t) in enumerate(((ya_ref, dsta_t), (yb_ref, dstb_t))):
                @pl.when(kind_t[b] == gi)
                def _():
                    src = buf_ref.at[of_slot, pl.ds(BLOCK * j, BLOCK), :]
                    dst = y_ref.at[pl.ds(pl.multiple_of(dst_t[b] * BLOCK, BLOCK), BLOCK), :]
                    act(pltpu.make_async_copy(src, dst, sem_ref.at[of_slot, j]))

    copies(step, slot, lambda c: c.start())

    @pl.when(step > 0)
    def _():
        copies(step - 1, 1 - slot, lambda c: c.wait())

    @pl.when(step == pl.num_programs(0) - 1)
    def _():
        copies(step, slot, lambda c: c.wait())


def _mlp(x, weights, layer, lay, alpha, out_rows=None):
    t = x.shape[0]
    row = lambda i, *_: (i, 0)
    vec = lambda i, *_: (layer, 0, 0)
    mat = lambda r, c: _const_spec((None, r, c), vec)
    in_specs = [pl.BlockSpec((TOKEN_TILE, D_MODEL), row),
                mat(D_MODEL, D_FF), mat(1, D_FF), mat(D_FF, D_MODEL), mat(1, D_MODEL),
                mat(1, D_MODEL), mat(1, D_MODEL)]
    if out_rows is None:
        body, name = _mlp_kernel, "mlp_ln"
        out_specs = pl.BlockSpec((TOKEN_TILE, D_MODEL), row)
        out_shape = jax.ShapeDtypeStruct((t, D_MODEL), F32)
        scratch = []
    else:
        body, name = _mlp_final_kernel, "mlp_ln_out"
        out_specs = [pl.BlockSpec(memory_space=pl.ANY)] * 2
        out_shape = [jax.ShapeDtypeStruct((r, D_MODEL), F32) for r in out_rows]
        scratch = [pltpu.VMEM((2, TOKEN_TILE, D_MODEL), F32),
                   pltpu.SemaphoreType.DMA((2, BLOCKS_PER_TILE))]
    grid_spec = pltpu.PrefetchScalarGridSpec(
        num_scalar_prefetch=3, grid=(t // TOKEN_TILE,),
        in_specs=in_specs, out_specs=out_specs, scratch_shapes=scratch)
    return pl.pallas_call(
        functools.partial(body, alpha=alpha),
        grid_spec=grid_spec,
        out_shape=out_shape,
        compiler_params=pltpu.CompilerParams(vmem_limit_bytes=VMEM_LIMIT,
                                             dimension_semantics=("arbitrary",)),
        name=name,
    )(lay["kind"], lay["src"][0], lay["src"][1], x, *weights)


def kernel(x_prompt, x_sample, meta_tokens, ln_emb_g, ln_emb_b, w_in, sink, w_pool, pool_scale, w_bo_attn,
           w_bo_pool, w_out, ln1_g, ln1_b, w_mlp1, b_mlp1, w_mlp2, b_mlp2, ln2_g, ln2_b):
    depth = w_in.shape[0]
    alpha = float((2 * depth) ** 0.25)
    groups = (x_prompt, x_sample)
    lay = _layout([x.shape[:2] for x in groups])
    header = jnp.concatenate([jnp.zeros((META_ROW0, D_MODEL), F32), meta_tokens.astype(F32)], axis=0)

    row3 = lambda a: a.reshape(a.shape[0], 1, a.shape[-1])
    w_in_b = w_in.astype(BF16)
    w_qvt = _qv_transposed(w_in)
    mix_w = (w_bo_attn.astype(BF16), w_pool.astype(BF16), row3(pool_scale), w_bo_pool.astype(BF16),
             w_out.astype(BF16), row3(ln1_g), row3(ln1_b))
    mlp_w = (w_mlp1.astype(BF16), row3(b_mlp1), w_mlp2.astype(BF16), row3(b_mlp2), row3(ln2_g), row3(ln2_b))
    bias = jnp.asarray(_alibi_bias_table())

    h = _embed(x_prompt.reshape(-1, D_MODEL), x_sample.reshape(-1, D_MODEL), header,
               ln_emb_g.reshape(1, -1), ln_emb_b.reshape(1, -1), lay)
    out_rows = [x.shape[0] * x.shape[1] for x in groups]
    for l in range(depth):
        qt, kk, vt, g = _inproj(h, w_in_b, w_qvt, l)
        attn = _attention(qt, kk, vt, sink[l].astype(F32), bias, lay)
        x1 = _mix(attn, g, h, w_in_b, mix_w, l, lay, alpha)
        h = _mlp(x1, mlp_w, l, lay, alpha, out_rows if l == depth - 1 else None)
    return tuple(y.reshape(x.shape) for y, x in zip(h, groups))
```

```python
import functools

import numpy as np
import jax
import jax.numpy as jnp
from jax import lax
from jax.experimental import pallas as pl
from jax.experimental.pallas import tpu as pltpu

D_MODEL = 1024
N_META = 16
N_HEADS = 16
N_KV_HEADS = 4
HEAD_DIM = 64
GQA_GROUP = N_HEADS // N_KV_HEADS
WINDOW = 128
BLOCK = 128
POOL_WINDOWS = (2, 4, 8, 16)
POOL_GROUP_WIDTH = 128
POOL_WIDTH = 512
KV_WIDTH = N_KV_HEADS * HEAD_DIM
GATE_WIDTH = 2 * D_MODEL
D_FF = 4 * D_MODEL
FF_CHUNK = 1024
LN_EPS = 1e-5
HALO = 8
META_ROW0 = BLOCK - N_META
BAND_KEYS = 3 * BLOCK
KEYS = BAND_KEYS + N_META
NEG = -1e30
LOG2E = 1.4426950408889634
ONES_ROWS = 16
BLOCK_FLAG_VARIANTS = (7, 6, 3, 4, 2)
TOKEN_TILE = 1024
HALF_TILE = TOKEN_TILE // 2
BLOCKS_PER_TILE = TOKEN_TILE // BLOCK
EMBED_BLOCKS = 4
ATTN_BLOCKS = 8
SCORE_LOOKAHEAD = 2
VMEM_LIMIT = 56 * 1024 * 1024

F32 = jnp.float32
BF16 = jnp.bfloat16
NT_DIMS = (((1,), (1,)), ((), ()))


def _layout(group_shapes):
    seqs = []
    n = 0
    for gi, (bsz, s) in enumerate(group_shapes):
        assert s % BLOCK == 0 and s >= BLOCK
        for bi in range(bsz):
            seqs.append((gi, bi, s // BLOCK, n))
            n += 1 + s // BLOCK
    nb = -(-n // BLOCKS_PER_TILE) * BLOCKS_PER_TILE
    meta_i = np.arange(nb, dtype=np.int32)
    flags = np.full(nb, 2, dtype=np.int32)
    t0 = np.zeros(nb + 2, dtype=np.int32)
    ln = np.zeros(nb + 2, dtype=np.int32)
    kind = np.full(nb, -2, dtype=np.int32)
    src = np.zeros((len(group_shapes), nb), dtype=np.int32)
    for gi, bi, nreal, b0 in seqs:
        for j in range(nreal + 1):
            b = b0 + j
            meta_i[b] = b0
            t0[b + 1] = -META_ROW0 + BLOCK * j
            ln[b + 1] = N_META + BLOCK * nreal
            flags[b] = (0 if j == 0 else 2) | (1 if j >= 2 else 0) | (4 if j < nreal else 0)
            kind[b] = -1 if j == 0 else gi
            if j > 0:
                src[gi, b:] = bi * nreal + (j - 1)
    variant = np.asarray([BLOCK_FLAG_VARIANTS.index(int(f)) for f in flags], dtype=np.int32)
    return dict(nb=nb, meta=meta_i, variant=variant, t0=t0, ln=ln, kind=kind, src=src)


def _alibi_bias_table():
    nv = len(BLOCK_FLAG_VARIANTS)
    slopes = 2.0 ** (-8.0 * np.arange(1, N_HEADS + 1) / N_HEADS)
    key = np.arange(BAND_KEYS)[:, None]
    qry = np.arange(BLOCK)[None, :]
    rel = np.abs(key - BLOCK - qry).astype(np.float32)[None, :, None, :]
    scaled = (slopes * LOG2E).astype(np.float32).reshape(N_KV_HEADS, 1, GQA_GROUP, 1)
    band = np.where(rel <= WINDOW, -(scaled * rel), np.float32(NEG))
    flags = np.asarray(BLOCK_FLAG_VARIANTS)[:, None]
    usable = (flags >> (np.arange(BAND_KEYS)[None, :] // BLOCK)) & 1
    band = np.where(usable[:, None, :, None, None] != 0, band[None], np.float32(NEG))
    band = band.reshape(nv, N_KV_HEADS, BAND_KEYS, GQA_GROUP * BLOCK)
    meta = np.zeros((nv, N_KV_HEADS, N_META, GQA_GROUP * BLOCK), np.float32)
    return np.concatenate([band, meta], axis=2).astype(np.float32)


def _const_spec(shape, index_map):
    return pl.BlockSpec(shape, index_map, pipeline_mode=pl.Buffered(1))


def _layer_norm(x, g, b):
    mu = jnp.mean(x, axis=-1, keepdims=True)
    xc = x - mu
    var = jnp.mean(xc * xc, axis=-1, keepdims=True)
    return xc * lax.rsqrt(var + LN_EPS) * g + b


def _dot(a, b):
    return jnp.dot(a, b, preferred_element_type=F32)


def _half(hf):
    return slice(HALF_TILE * hf, HALF_TILE * (hf + 1))


def _embed_kernel(kind_t, srca_t, srcb_t, *refs):
    del srca_t, srcb_t
    xa_refs, xb_refs = refs[:EMBED_BLOCKS], refs[EMBED_BLOCKS:2 * EMBED_BLOCKS]
    hdr_ref, g_ref, b_ref, o_ref = refs[2 * EMBED_BLOCKS:]
    for j in range(EMBED_BLOCKS):
        kind = kind_t[pl.program_id(0) * EMBED_BLOCKS + j]
        x = jnp.where(kind == 0, xa_refs[j][...], jnp.where(kind == 1, xb_refs[j][...], 0.0))
        x = jnp.where(kind == -1, hdr_ref[...], x)
        o_ref[BLOCK * j:BLOCK * (j + 1), :] = _layer_norm(x, g_ref[...], b_ref[...])


def _embed(xa, xb, header, g, b, lay):
    blk = lambda sel: pl.BlockSpec((BLOCK, D_MODEL), sel)
    from_a = [blk(lambda i, k, sa, sb, j=j: (sa[i * EMBED_BLOCKS + j], 0)) for j in range(EMBED_BLOCKS)]
    from_b = [blk(lambda i, k, sa, sb, j=j: (sb[i * EMBED_BLOCKS + j], 0)) for j in range(EMBED_BLOCKS)]
    const = lambda i, k, sa, sb: (0, 0)
    grid_spec = pltpu.PrefetchScalarGridSpec(
        num_scalar_prefetch=3,
        grid=(lay["nb"] // EMBED_BLOCKS,),
        in_specs=from_a + from_b + [blk(const), pl.BlockSpec((1, D_MODEL), const),
                                    pl.BlockSpec((1, D_MODEL), const)],
        out_specs=pl.BlockSpec((EMBED_BLOCKS * BLOCK, D_MODEL), lambda i, k, sa, sb: (i, 0)),
    )
    return pl.pallas_call(
        _embed_kernel,
        grid_spec=grid_spec,
        out_shape=jax.ShapeDtypeStruct((lay["nb"] * BLOCK, D_MODEL), F32),
        name="embed_ln",
    )(lay["kind"], lay["src"][0], lay["src"][1], *([xa] * EMBED_BLOCKS), *([xb] * EMBED_BLOCKS), header, g, b)


def _qvt_kernel(w_ref, o_ref):
    o_ref[...] = w_ref[...].T.astype(BF16)


def _qv_transposed(w_in):
    depth = w_in.shape[0]
    q_blocks = D_MODEL // KV_WIDTH
    v_block0 = (D_MODEL + KV_WIDTH) // KV_WIDTH
    n_blocks = q_blocks + 1
    src_col = lambda c: jnp.where(c < q_blocks, c, c - q_blocks + v_block0)
    return pl.pallas_call(
        _qvt_kernel,
        grid=(depth, n_blocks),
        in_specs=[pl.BlockSpec((None, D_MODEL, KV_WIDTH), lambda l, c: (l, 0, src_col(c)))],
        out_specs=pl.BlockSpec((None, KV_WIDTH, D_MODEL), lambda l, c: (l, c, 0)),
        out_shape=jax.ShapeDtypeStruct((depth, D_MODEL + KV_WIDTH, D_MODEL), BF16),
        name="qv_transpose",
    )(w_in)


def _inproj_kernel(h_ref, w_ref, wt_ref, qt_ref, k_ref, vt_ref, g_ref):
    k0 = D_MODEL
    g0 = D_MODEL + 2 * KV_WIDTH + POOL_WIDTH

    def project(hf):
        rows = _half(hf)
        x = h_ref[rows, :].astype(BF16)
        gates = _dot(x, w_ref[:, g0:g0 + GATE_WIDTH])
        g_ref[rows, :] = (0.5 * jnp.tanh(0.5 * gates) + 0.5).astype(BF16)
        qt = lax.dot_general(wt_ref[0:D_MODEL, :], x, NT_DIMS, preferred_element_type=F32)
        qt_ref[:, rows] = (qt * (HEAD_DIM ** -0.5 * LOG2E)).astype(BF16)
        vt = lax.dot_general(wt_ref[D_MODEL:D_MODEL + KV_WIDTH, :], x, NT_DIMS, preferred_element_type=F32)
        vt_ref[:, rows] = vt.astype(BF16)
        k_ref[rows, :] = _dot(x, w_ref[:, k0:k0 + KV_WIDTH]).astype(BF16)

    project(0)
    project(1)


def _inproj(h, w_in, w_qvt, layer):
    t = h.shape[0]
    row = lambda i: (i, 0)
    col = lambda i: (0, i)
    return pl.pallas_call(
        _inproj_kernel,
        grid=(t // TOKEN_TILE,),
        in_specs=[pl.BlockSpec((TOKEN_TILE, D_MODEL), row),
                  _const_spec((None, D_MODEL, 4 * D_MODEL), lambda i: (layer, 0, 0)),
                  _const_spec((None, D_MODEL + KV_WIDTH, D_MODEL), lambda i: (layer, 0, 0))],
        out_specs=[pl.BlockSpec((D_MODEL, TOKEN_TILE), col),
                   pl.BlockSpec((TOKEN_TILE, KV_WIDTH), row),
                   pl.BlockSpec((KV_WIDTH, TOKEN_TILE), col),
                   pl.BlockSpec((TOKEN_TILE, GATE_WIDTH), row)],
        out_shape=[jax.ShapeDtypeStruct((D_MODEL, t), BF16),
                   jax.ShapeDtypeStruct((t, KV_WIDTH), BF16),
                   jax.ShapeDtypeStruct((KV_WIDTH, t), BF16),
                   jax.ShapeDtypeStruct((t, GATE_WIDTH), BF16)],
        compiler_params=pltpu.CompilerParams(vmem_limit_bytes=VMEM_LIMIT),
        name="in_proj",
    )(h, w_in, w_qvt)


def _attn_kernel(meta_t, var_t, sink_ref, qt_ref, *refs):
    del meta_t
    kp_ref, ko_ref, kn_ref = refs[0:3]
    kh_refs = refs[3:3 + ATTN_BLOCKS]
    vp_ref, vo_ref, vn_ref = refs[3 + ATTN_BLOCKS:6 + ATTN_BLOCKS]
    vh_refs = refs[6 + ATTN_BLOCKS:6 + 2 * ATTN_BLOCKS]
    bias_ref, o_ref = refs[6 + 2 * ATTN_BLOCKS:]
    step = pl.program_id(0)
    width = GQA_GROUP * BLOCK
    zero_q = jnp.zeros((HEAD_DIM, width), BF16)
    zero_p = jnp.zeros((META_ROW0, width), BF16)
    ones_band = jnp.ones((ONES_ROWS, BAND_KEYS), BF16)
    ones_meta = jnp.ones((ONES_ROWS, BLOCK), BF16)

    def window(blk, before, own, after, axis):
        parts = []
        for j in range(blk - 1, blk + 2):
            sel = [slice(None), slice(None)]
            if j < 0:
                parts.append(before)
            elif j >= ATTN_BLOCKS:
                parts.append(after)
            else:
                sel[axis] = slice(BLOCK * j, BLOCK * (j + 1))
                parts.append(own[tuple(sel)])
        return parts

    def scores(blk, k):
        pair = slice(BLOCK * (k // 2), BLOCK * (k // 2 + 1))
        band = window(blk, kp_ref[:, pair], ko_ref[:, pair], kn_ref[:, pair], 0)
        kc = jnp.concatenate(band + [kh_refs[blk][META_ROW0:BLOCK, pair]], axis=0)
        q0 = GQA_GROUP * HEAD_DIM * k
        qs = slice(BLOCK * blk, BLOCK * (blk + 1))
        qt4 = jnp.concatenate([qt_ref[q0 + HEAD_DIM * j:q0 + HEAD_DIM * (j + 1), qs]
                               for j in range(GQA_GROUP)], axis=1)
        rhs = jnp.concatenate([qt4, zero_q] if k % 2 == 0 else [zero_q, qt4], axis=0)
        return _dot(kc, rhs) + bias_ref[var_t[ATTN_BLOCKS * step + blk], k]

    def finish(blk, k, s):
        sink_row = jnp.concatenate([jnp.full((1, BLOCK), sink_ref[GQA_GROUP * k + j] * LOG2E, F32)
                                    for j in range(GQA_GROUP)], axis=1)
        m = jnp.maximum(jnp.max(s, axis=0, keepdims=True), sink_row)
        pb = jnp.exp2(s - m).astype(BF16)
        vs = slice(HEAD_DIM * k, HEAD_DIM * (k + 1))
        band = window(blk, vp_ref[vs, :], vo_ref[vs, :], vn_ref[vs, :], 1)
        vc = jnp.concatenate([jnp.concatenate(band, axis=1), ones_band], axis=0)
        vh = jnp.concatenate([vh_refs[blk][vs, :], ones_meta], axis=0)
        p_meta = jnp.concatenate([zero_p, pb[BAND_KEYS:KEYS]], axis=0)
        acc = _dot(vc, pb[0:BAND_KEYS]) + _dot(vh, p_meta)
        denom = acc[HEAD_DIM:HEAD_DIM + 1] + jnp.exp2(sink_row - m)
        ot = acc[0:HEAD_DIM] / denom
        rows = slice(BLOCK * blk, BLOCK * (blk + 1))
        for j in range(GQA_GROUP // 2):
            two = jnp.concatenate([ot[:, 2 * BLOCK * j:2 * BLOCK * j + BLOCK],
                                   ot[:, 2 * BLOCK * j + BLOCK:2 * BLOCK * (j + 1)]], axis=0)
            c0 = GQA_GROUP * HEAD_DIM * k + BLOCK * j
            o_ref[rows, c0:c0 + BLOCK] = two.T.astype(BF16)

    units = [(blk, k) for blk in range(ATTN_BLOCKS) for k in range(N_KV_HEADS)]
    pending = [scores(*u) for u in units[:SCORE_LOOKAHEAD]]
    for idx, unit in enumerate(units):
        if idx + SCORE_LOOKAHEAD < len(units):
            pending.append(scores(*units[idx + SCORE_LOOKAHEAD]))
        finish(*unit, pending.pop(0))


def _attention(qt, kk, vt, sink, bias, lay):
    t = kk.shape[0]
    nb = lay["nb"]
    n = ATTN_BLOCKS
    kspec = lambda rows, sel: pl.BlockSpec((rows, KV_WIDTH), sel)
    vspec = lambda cols, sel: pl.BlockSpec((KV_WIDTH, cols), sel)
    before = lambda i: jnp.maximum(n * i - 1, 0)
    after = lambda i: jnp.minimum(n * i + n, nb - 1)
    k_specs = [kspec(BLOCK, lambda i, m, v: (before(i), 0)),
               kspec(n * BLOCK, lambda i, m, v: (i, 0)),
               kspec(BLOCK, lambda i, m, v: (after(i), 0))]
    k_specs += [kspec(BLOCK, lambda i, m, v, j=j: (m[n * i + j], 0)) for j in range(n)]
    v_specs = [vspec(BLOCK, lambda i, m, v: (0, before(i))),
               vspec(n * BLOCK, lambda i, m, v: (0, i)),
               vspec(BLOCK, lambda i, m, v: (0, after(i)))]
    v_specs += [vspec(BLOCK, lambda i, m, v, j=j: (0, m[n * i + j])) for j in range(n)]
    grid_spec = pltpu.PrefetchScalarGridSpec(
        num_scalar_prefetch=2,
        grid=(nb // n,),
        in_specs=[pl.BlockSpec(memory_space=pltpu.SMEM),
                  pl.BlockSpec((D_MODEL, n * BLOCK), lambda i, m, v: (0, i))] + k_specs + v_specs
                 + [_const_spec((len(BLOCK_FLAG_VARIANTS), N_KV_HEADS, KEYS, GQA_GROUP * BLOCK),
                                lambda i, m, v: (0, 0, 0, 0))],
        out_specs=pl.BlockSpec((n * BLOCK, D_MODEL), lambda i, m, v: (i, 0)),
    )
    return pl.pallas_call(
        _attn_kernel,
        grid_spec=grid_spec,
        out_shape=jax.ShapeDtypeStruct((t, D_MODEL), BF16),
        compiler_params=pltpu.CompilerParams(vmem_limit_bytes=VMEM_LIMIT),
        name="band_attn",
    )(lay["meta"], lay["variant"], sink, qt, *([kk] * (3 + n)), *([vt] * (3 + n)), bias)


def _pool_diff(u_ext, t_ext, l_ext):
    return [_pool_group(u_ext, t_ext, l_ext, gi) for gi in range(len(POOL_WINDOWS))]


def _pool_group(u_ext, t_ext, l_ext, gi):
    w = POOL_WINDOWS[gi]
    n = u_ext.shape[0]
    rows = n - 2 * HALO
    valid = (t_ext >= 0) & (t_ext < l_ext)
    t = t_ext[HALO:HALO + rows]
    ln = l_ext[HALO:HALO + rows]
    fwd = lambda x, k: x + pltpu.roll(x, n - k, 0)
    x = jnp.where(valid, u_ext[:, gi * POOL_GROUP_WIDTH:(gi + 1) * POOL_GROUP_WIDTH], 0.0)
    acc, k = x, 1
    while 2 * k <= min(w, HALO):
        acc = fwd(acc, k)
        k *= 2
    if w <= HALO:
        win = pltpu.roll(acc, w // 2, 0)
    else:
        win = acc + pltpu.roll(acc, HALO, 0)
    cnt = jnp.minimum(t + w // 2, ln) - jnp.maximum(t - w // 2, 0)
    cnt = jnp.maximum(cnt, 1).astype(F32)
    return win[HALO:HALO + rows] / cnt - x[HALO:HALO + rows]


def _mix_kernel(t0_t, ln_t,
                attn_ref, g_ref, h_ref, hp_ref, hn_ref,
                wu_ref, wa_ref, wp_ref, ps_ref, wb_ref, wo_ref, lg_ref, lb_ref, o_ref, *, alpha):
    b0 = pl.program_id(0) * BLOCKS_PER_TILE

    def rows_of(entry, first, count):
        r = lax.broadcasted_iota(jnp.int32, (count, POOL_GROUP_WIDTH), 0) + first
        return r + t0_t[entry], jnp.zeros((count, POOL_GROUP_WIDTH), jnp.int32) + ln_t[entry]

    parts = [rows_of(b0, BLOCK - HALO, HALO)]
    parts += [rows_of(b0 + 1 + j, 0, BLOCK) for j in range(BLOCKS_PER_TILE)]
    parts += [rows_of(b0 + 1 + BLOCKS_PER_TILE, 0, HALO)]
    t_ext = jnp.concatenate([p[0] for p in parts], axis=0)
    l_ext = jnp.concatenate([p[1] for p in parts], axis=0)

    u_halves = [_dot(h_ref[_half(hf), :].astype(BF16), wu_ref[...]) for hf in range(2)]
    h_halo = jnp.concatenate([hp_ref[...], hn_ref[...]], axis=0).astype(BF16)
    u_halo = _dot(h_halo, wu_ref[...])
    u_ext = jnp.concatenate([u_halo[0:HALO]] + u_halves + [u_halo[HALO:2 * HALO]], axis=0)

    def pool(hf):
        ext = slice(HALF_TILE * hf, HALF_TILE * (hf + 1) + 2 * HALO)
        return _pool_diff(u_ext[ext], t_ext[ext], l_ext[ext])

    def attn_branch(hf):
        return _dot(attn_ref[_half(hf), :], wa_ref[...])

    def pool_branch(diffs):
        y = jnp.concatenate([_dot(d.astype(BF16), wp_ref[gi]) for gi, d in enumerate(diffs)], axis=1)
        return _dot((y * ps_ref[...]).astype(BF16), wb_ref[...])

    def gate(hf, ya, yb):
        g = g_ref[_half(hf), :].astype(F32)
        return (g[:, 0:D_MODEL] * ya + g[:, D_MODEL:2 * D_MODEL] * yb).astype(BF16)

    def norm(hf, mixed):
        o_ref[_half(hf), :] = _layer_norm(alpha * h_ref[_half(hf), :] + mixed, lg_ref[...], lb_ref[...])

    ya_a = attn_branch(0)
    pool_a = pool(0)
    ya_b = attn_branch(1)
    pool_b = pool(1)
    yb_a = pool_branch(pool_a)
    yb_b = pool_branch(pool_b)
    gated_a = gate(0, ya_a, yb_a)
    mixed_a = _dot(gated_a, wo_ref[...])
    gated_b = gate(1, ya_b, yb_b)
    mixed_b = _dot(gated_b, wo_ref[...])
    norm(0, mixed_a)
    norm(1, mixed_b)


def _mix(attn, g, h, w_in, weights, layer, lay, alpha):
    t = h.shape[0]
    n_halo = t // HALO
    per = TOKEN_TILE // HALO
    u_block = (D_MODEL + 2 * KV_WIDTH) // POOL_WIDTH
    row = lambda i, a, b: (i, 0)
    vec = lambda i, a, b: (layer, 0, 0)
    mat = lambda r, c: _const_spec((None, r, c), vec)
    grid_spec = pltpu.PrefetchScalarGridSpec(
        num_scalar_prefetch=2,
        grid=(t // TOKEN_TILE,),
        in_specs=[pl.BlockSpec((TOKEN_TILE, D_MODEL), row),
                  pl.BlockSpec((TOKEN_TILE, GATE_WIDTH), row),
                  pl.BlockSpec((TOKEN_TILE, D_MODEL), row),
                  pl.BlockSpec((HALO, D_MODEL), lambda i, a, b: (jnp.maximum(i * per - 1, 0), 0)),
                  pl.BlockSpec((HALO, D_MODEL), lambda i, a, b: (jnp.minimum((i + 1) * per, n_halo - 1), 0)),
                  _const_spec((None, D_MODEL, POOL_WIDTH), lambda i, a, b: (layer, 0, u_block)),
                  mat(D_MODEL, D_MODEL),
                  _const_spec((None, len(POOL_WINDOWS), POOL_GROUP_WIDTH, POOL_GROUP_WIDTH),
                              lambda i, a, b: (layer, 0, 0, 0)),
                  mat(1, POOL_WIDTH), mat(POOL_WIDTH, D_MODEL), mat(D_MODEL, D_MODEL),
                  mat(1, D_MODEL), mat(1, D_MODEL)],
        out_specs=pl.BlockSpec((TOKEN_TILE, D_MODEL), row),
    )
    return pl.pallas_call(
        functools.partial(_mix_kernel, alpha=alpha),
        grid_spec=grid_spec,
        out_shape=jax.ShapeDtypeStruct((t, D_MODEL), F32),
        compiler_params=pltpu.CompilerParams(vmem_limit_bytes=VMEM_LIMIT),
        name="mix_ln",
    )(lay["t0"], lay["ln"], attn, g, h, h, h, w_in, *weights)


def _mlp_body(x_ref, w1_ref, b1_ref, w2_ref, b2_ref, lg_ref, lb_ref, store, alpha):
    def ff(xb, chunks):
        acc = None
        for c in chunks:
            cs = slice(c * FF_CHUNK, (c + 1) * FF_CHUNK)
            a = jnp.maximum(_dot(xb, w1_ref[:, cs]) + b1_ref[:, cs], 0.0)
            part = _dot((a * a).astype(BF16), w2_ref[cs, :])
            acc = part if acc is None else acc + part
        return acc

    def norm(x, acc):
        return _layer_norm(alpha * x + (acc + b2_ref[...]), lg_ref[...], lb_ref[...])

    first, rest = (0,), tuple(range(1, D_FF // FF_CHUNK))
    x_a = x_ref[_half(0), :]
    x_b = x_ref[_half(1), :]
    xb_a, xb_b = x_a.astype(BF16), x_b.astype(BF16)
    acc_a = ff(xb_a, first + rest)
    acc_b = ff(xb_b, first)
    store(0, norm(x_a, acc_a))
    acc_b = acc_b + ff(xb_b, rest)
    store(1, norm(x_b, acc_b))


def _mlp_kernel(kind_t, dsta_t, dstb_t, *refs, alpha):
    del kind_t, dsta_t, dstb_t
    ins, o_ref = refs[:-1], refs[-1]

    def store(hf, value):
        o_ref[_half(hf), :] = value

    _mlp_body(*ins, store, alpha)


def _mlp_final_kernel(kind_t, dsta_t, dstb_t, *refs, alpha):
    ins, (ya_ref, yb_ref, buf_ref, sem_ref) = refs[:-4], refs[-4:]
    step = pl.program_id(0)
    slot = step % 2

    def store(hf, value):
        buf_ref[slot, _half(hf), :] = value

    _mlp_body(*ins, store, alpha)

    def copies(of_step, of_slot, act):
        for j in range(BLOCKS_PER_TILE):
            b = of_step * BLOCKS_PER_TILE + j
            for gi, (y_ref, dst_t) in enumerate(((ya_ref, dsta_t), (yb_ref, dstb_t))):
                @pl.when(kind_t[b] == gi)
                def _():
                    src = buf_ref.at[of_slot, pl.ds(BLOCK * j, BLOCK), :]
                    dst = y_ref.at[pl.ds(pl.multiple_of(dst_t[b] * BLOCK, BLOCK), BLOCK), :]
                    act(pltpu.make_async_copy(src, dst, sem_ref.at[of_slot, j]))

    copies(step, slot, lambda c: c.start())

    @pl.when(step > 0)
    def _():
        copies(step - 1, 1 - slot, lambda c: c.wait())

    @pl.when(step == pl.num_programs(0) - 1)
    def _():
        copies(step, slot, lambda c: c.wait())


def _mlp(x, weights, layer, lay, alpha, out_rows=None):
    t = x.shape[0]
    row = lambda i, *_: (i, 0)
    vec = lambda i, *_: (layer, 0, 0)
    mat = lambda r, c: _const_spec((None, r, c), vec)
    in_specs = [pl.BlockSpec((TOKEN_TILE, D_MODEL), row),
                mat(D_MODEL, D_FF), mat(1, D_FF), mat(D_FF, D_MODEL), mat(1, D_MODEL),
                mat(1, D_MODEL), mat(1, D_MODEL)]
    if out_rows is None:
        body, name = _mlp_kernel, "mlp_ln"
        out_specs = pl.BlockSpec((TOKEN_TILE, D_MODEL), row)
        out_shape = jax.ShapeDtypeStruct((t, D_MODEL), F32)
        scratch = []
    else:
        body, name = _mlp_final_kernel, "mlp_ln_out"
        out_specs = [pl.BlockSpec(memory_space=pl.ANY)] * 2
        out_shape = [jax.ShapeDtypeStruct((r, D_MODEL), F32) for r in out_rows]
        scratch = [pltpu.VMEM((2, TOKEN_TILE, D_MODEL), F32),
                   pltpu.SemaphoreType.DMA((2, BLOCKS_PER_TILE))]
    grid_spec = pltpu.PrefetchScalarGridSpec(
        num_scalar_prefetch=3, grid=(t // TOKEN_TILE,),
        in_specs=in_specs, out_specs=out_specs, scratch_shapes=scratch)
    return pl.pallas_call(
        functools.partial(body, alpha=alpha),
        grid_spec=grid_spec,
        out_shape=out_shape,
        compiler_params=pltpu.CompilerParams(vmem_limit_bytes=VMEM_LIMIT,
                                             dimension_semantics=("arbitrary",)),
        name=name,
    )(lay["kind"], lay["src"][0], lay["src"][1], x, *weights)


def kernel(x_prompt, x_sample, meta_tokens, ln_emb_g, ln_emb_b, w_in, sink, w_pool, pool_scale, w_bo_attn,
           w_bo_pool, w_out, ln1_g, ln1_b, w_mlp1, b_mlp1, w_mlp2, b_mlp2, ln2_g, ln2_b):
    depth = w_in.shape[0]
    alpha = float((2 * depth) ** 0.25)
    groups = (x_prompt, x_sample)
    lay = _layout([x.shape[:2] for x in groups])
    header = jnp.concatenate([jnp.zeros((META_ROW0, D_MODEL), F32), meta_tokens.astype(F32)], axis=0)

    row3 = lambda a: a.reshape(a.shape[0], 1, a.shape[-1])
    w_in_b = w_in.astype(BF16)
    w_qvt = _qv_transposed(w_in)
    mix_w = (w_bo_attn.astype(BF16), w_pool.astype(BF16), row3(pool_scale), w_bo_pool.astype(BF16),
             w_out.astype(BF16), row3(ln1_g), row3(ln1_b))
    mlp_w = (w_mlp1.astype(BF16), row3(b_mlp1), w_mlp2.astype(BF16), row3(b_mlp2), row3(ln2_g), row3(ln2_b))
    bias = jnp.asarray(_alibi_bias_table())

    h = _embed(x_prompt.reshape(-1, D_MODEL), x_sample.reshape(-1, D_MODEL), header,
               ln_emb_g.reshape(1, -1), ln_emb_b.reshape(1, -1), lay)
    out_rows = [x.shape[0] * x.shape[1] for x in groups]
    for l in range(depth):
        qt, kk, vt, g = _inproj(h, w_in_b, w_qvt, l)
        attn = _attention(qt, kk, vt, sink[l].astype(F32), bias, lay)
        x1 = _mix(attn, g, h, w_in_b, mix_w, l, lay, alpha)
        h = _mlp(x1, mlp_w, l, lay, alpha, out_rows if l == depth - 1 else None)
    return tuple(y.reshape(x.shape) for y, x in zip(h, groups))
```

```python
import functools

import numpy as np
import jax
import jax.numpy as jnp
from jax import lax
from jax.experimental import pallas as pl
from jax.experimental.pallas import tpu as pltpu

D_MODEL = 1024
N_META = 16
N_HEADS = 16
N_KV_HEADS = 4
HEAD_DIM = 64
GQA_GROUP = N_HEADS // N_KV_HEADS
WINDOW = 128
BLOCK = 128
POOL_WINDOWS = (2, 4, 8, 16)
POOL_GROUP_WIDTH = 128
POOL_WIDTH = 512
KV_WIDTH = N_KV_HEADS * HEAD_DIM
GATE_WIDTH = 2 * D_MODEL
D_FF = 4 * D_MODEL
FF_CHUNK = 1024
LN_EPS = 1e-5
HALO = 8
META_ROW0 = BLOCK - N_META
BAND_KEYS = 3 * BLOCK
KEYS = BAND_KEYS + N_META
NEG = -1e30
LOG2E = 1.4426950408889634
ONES_ROWS = 16
BLOCK_FLAG_VARIANTS = (7, 6, 3, 4, 2)
TOKEN_TILE = 1024
HALF_TILE = TOKEN_TILE // 2
BLOCKS_PER_TILE = TOKEN_TILE // BLOCK
EMBED_BLOCKS = 8
ATTN_BLOCKS = 8
SCORE_LOOKAHEAD = 2
CAST_MIN_ROWS = 16
VMEM_LIMIT = 56 * 1024 * 1024

F32 = jnp.float32
BF16 = jnp.bfloat16
NT_DIMS = (((1,), (1,)), ((), ()))


def _layout(group_shapes):
    seqs = []
    n = 0
    for gi, (bsz, s) in enumerate(group_shapes):
        assert s % BLOCK == 0 and s >= BLOCK
        for bi in range(bsz):
            seqs.append((gi, bi, s // BLOCK, n))
            n += 1 + s // BLOCK
    nb = -(-n // BLOCKS_PER_TILE) * BLOCKS_PER_TILE
    meta_i = np.arange(nb, dtype=np.int32)
    flags = np.full(nb, 2, dtype=np.int32)
    t0 = np.zeros(nb + 2, dtype=np.int32)
    ln = np.zeros(nb + 2, dtype=np.int32)
    kind = np.full(nb, -2, dtype=np.int32)
    src = np.zeros((len(group_shapes), nb), dtype=np.int32)
    for gi, bi, nreal, b0 in seqs:
        for j in range(nreal + 1):
            b = b0 + j
            meta_i[b] = b0
            t0[b + 1] = -META_ROW0 + BLOCK * j
            ln[b + 1] = N_META + BLOCK * nreal
            flags[b] = (0 if j == 0 else 2) | (1 if j >= 2 else 0) | (4 if j < nreal else 0)
            kind[b] = -1 if j == 0 else gi
            if j > 0:
                src[gi, b:] = bi * nreal + (j - 1)
    variant = np.asarray([BLOCK_FLAG_VARIANTS.index(int(f)) for f in flags], dtype=np.int32)
    return dict(nb=nb, meta=meta_i, variant=variant, t0=t0, ln=ln, kind=kind, src=src)


def _alibi_bias_table():
    nv = len(BLOCK_FLAG_VARIANTS)
    slopes = 2.0 ** (-8.0 * np.arange(1, N_HEADS + 1) / N_HEADS)
    key = np.arange(BAND_KEYS)[:, None]
    qry = np.arange(BLOCK)[None, :]
    rel = np.abs(key - BLOCK - qry).astype(np.float32)[None, :, None, :]
    scaled = (slopes * LOG2E).astype(np.float32).reshape(N_KV_HEADS, 1, GQA_GROUP, 1)
    band = np.where(rel <= WINDOW, -(scaled * rel), np.float32(NEG))
    flags = np.asarray(BLOCK_FLAG_VARIANTS)[:, None]
    usable = (flags >> (np.arange(BAND_KEYS)[None, :] // BLOCK)) & 1
    band = np.where(usable[:, None, :, None, None] != 0, band[None], np.float32(NEG))
    band = band.reshape(nv, N_KV_HEADS, BAND_KEYS, GQA_GROUP * BLOCK)
    meta = np.zeros((nv, N_KV_HEADS, N_META, GQA_GROUP * BLOCK), np.float32)
    return np.concatenate([band, meta], axis=2).astype(np.float32)


def _const_spec(shape, index_map):
    return pl.BlockSpec(shape, index_map, pipeline_mode=pl.Buffered(1))


def _layer_of(stacked, layer):
    return layer if stacked.shape[0] > 1 else 0


def _layer_spec(stacked, layer):
    rest = stacked.shape[1:]
    index = (_layer_of(stacked, layer),) + (0,) * len(rest)
    return _const_spec((None,) + rest, lambda *_: index)


def _layer_norm(x, g, b):
    mu = jnp.mean(x, axis=-1, keepdims=True)
    xc = x - mu
    var = jnp.mean(xc * xc, axis=-1, keepdims=True)
    return xc * lax.rsqrt(var + LN_EPS) * g + b


def _dot(a, b):
    return jnp.dot(a, b, preferred_element_type=F32)


def _half(hf):
    return slice(HALF_TILE * hf, HALF_TILE * (hf + 1))


def _embed_kernel(kind_t, srca_t, srcb_t, *refs):
    del srca_t, srcb_t
    xa_refs, xb_refs = refs[:EMBED_BLOCKS], refs[EMBED_BLOCKS:2 * EMBED_BLOCKS]
    hdr_ref, g_ref, b_ref, o_ref = refs[2 * EMBED_BLOCKS:]
    for j in range(EMBED_BLOCKS):
        kind = kind_t[pl.program_id(0) * EMBED_BLOCKS + j]
        x = jnp.where(kind == 0, xa_refs[j][...], jnp.where(kind == 1, xb_refs[j][...], 0.0))
        x = jnp.where(kind == -1, hdr_ref[...], x)
        o_ref[BLOCK * j:BLOCK * (j + 1), :] = _layer_norm(x, g_ref[...], b_ref[...])


def _embed(xa, xb, header, g, b, lay):
    blk = lambda sel: pl.BlockSpec((BLOCK, D_MODEL), sel)
    from_a = [blk(lambda i, k, sa, sb, j=j: (sa[i * EMBED_BLOCKS + j], 0)) for j in range(EMBED_BLOCKS)]
    from_b = [blk(lambda i, k, sa, sb, j=j: (sb[i * EMBED_BLOCKS + j], 0)) for j in range(EMBED_BLOCKS)]
    const = lambda i, k, sa, sb: (0, 0)
    grid_spec = pltpu.PrefetchScalarGridSpec(
        num_scalar_prefetch=3,
        grid=(lay["nb"] // EMBED_BLOCKS,),
        in_specs=from_a + from_b + [blk(const), pl.BlockSpec((1, D_MODEL), const),
                                    pl.BlockSpec((1, D_MODEL), const)],
        out_specs=pl.BlockSpec((EMBED_BLOCKS * BLOCK, D_MODEL), lambda i, k, sa, sb: (i, 0)),
    )
    return pl.pallas_call(
        _embed_kernel,
        grid_spec=grid_spec,
        out_shape=jax.ShapeDtypeStruct((lay["nb"] * BLOCK, D_MODEL), F32),
        name="embed_ln",
    )(lay["kind"], lay["src"][0], lay["src"][1], *([xa] * EMBED_BLOCKS), *([xb] * EMBED_BLOCKS), header, g, b)


def _qvt_kernel(w_ref, o_ref):
    o_ref[...] = w_ref[...].T.astype(BF16)


def _qv_transposed(w_in):
    depth = w_in.shape[0]
    q_blocks = D_MODEL // KV_WIDTH
    v_block0 = (D_MODEL + KV_WIDTH) // KV_WIDTH
    n_blocks = q_blocks + 1
    src_col = lambda c: jnp.where(c < q_blocks, c, c - q_blocks + v_block0)
    return pl.pallas_call(
        _qvt_kernel,
        grid=(depth, n_blocks),
        in_specs=[pl.BlockSpec((None, D_MODEL, KV_WIDTH), lambda l, c: (l, 0, src_col(c)))],
        out_specs=pl.BlockSpec((None, KV_WIDTH, D_MODEL), lambda l, c: (l, c, 0)),
        out_shape=jax.ShapeDtypeStruct((depth, D_MODEL + KV_WIDTH, D_MODEL), BF16),
        name="qv_transpose",
    )(w_in)


def _inproj_kernel(h_ref, w_ref, wt_ref, qt_ref, k_ref, vt_ref, g_ref):
    k0 = D_MODEL
    g0 = D_MODEL + 2 * KV_WIDTH + POOL_WIDTH

    def project(hf):
        rows = _half(hf)
        x = h_ref[rows, :].astype(BF16)
        gates = _dot(x, w_ref[:, g0:g0 + GATE_WIDTH])
        g_ref[rows, :] = (0.5 * jnp.tanh(0.5 * gates) + 0.5).astype(BF16)
        qt = lax.dot_general(wt_ref[0:D_MODEL, :], x, NT_DIMS, preferred_element_type=F32)
        qt_ref[:, rows] = (qt * (HEAD_DIM ** -0.5 * LOG2E)).astype(BF16)
        vt = lax.dot_general(wt_ref[D_MODEL:D_MODEL + KV_WIDTH, :], x, NT_DIMS, preferred_element_type=F32)
        vt_ref[:, rows] = vt.astype(BF16)
        k_ref[rows, :] = _dot(x, w_ref[:, k0:k0 + KV_WIDTH]).astype(BF16)

    project(0)
    project(1)


def _inproj(h, w_in, w_qvt, layer):
    t = h.shape[0]
    row = lambda i: (i, 0)
    col = lambda i: (0, i)
    return pl.pallas_call(
        _inproj_kernel,
        grid=(t // TOKEN_TILE,),
        in_specs=[pl.BlockSpec((TOKEN_TILE, D_MODEL), row),
                  _layer_spec(w_in, layer), _layer_spec(w_qvt, layer)],
        out_specs=[pl.BlockSpec((D_MODEL, TOKEN_TILE), col),
                   pl.BlockSpec((TOKEN_TILE, KV_WIDTH), row),
                   pl.BlockSpec((KV_WIDTH, TOKEN_TILE), col),
                   pl.BlockSpec((TOKEN_TILE, GATE_WIDTH), row)],
        out_shape=[jax.ShapeDtypeStruct((D_MODEL, t), BF16),
                   jax.ShapeDtypeStruct((t, KV_WIDTH), BF16),
                   jax.ShapeDtypeStruct((KV_WIDTH, t), BF16),
                   jax.ShapeDtypeStruct((t, GATE_WIDTH), BF16)],
        compiler_params=pltpu.CompilerParams(vmem_limit_bytes=VMEM_LIMIT),
        name="in_proj",
    )(h, w_in, w_qvt)


def _attn_kernel(meta_t, var_t, sink_ref, qt_ref, *refs, n_cast):
    del meta_t
    kp_ref, ko_ref, kn_ref = refs[0:3]
    kh_refs = refs[3:3 + ATTN_BLOCKS]
    vp_ref, vo_ref, vn_ref = refs[3 + ATTN_BLOCKS:6 + ATTN_BLOCKS]
    vh_refs = refs[6 + ATTN_BLOCKS:6 + 2 * ATTN_BLOCKS]
    n_in = 7 + 2 * ATTN_BLOCKS + n_cast
    bias_ref, o_ref = refs[6 + 2 * ATTN_BLOCKS], refs[n_in]
    for src_ref, dst_ref in zip(refs[n_in - n_cast:n_in], refs[n_in + 1:]):
        dst_ref[...] = src_ref[...].astype(BF16)
    step = pl.program_id(0)
    width = GQA_GROUP * BLOCK
    zero_q = jnp.zeros((HEAD_DIM, width), BF16)
    zero_p = jnp.zeros((META_ROW0, width), BF16)
    ones_band = jnp.ones((ONES_ROWS, BAND_KEYS), BF16)
    ones_meta = jnp.ones((ONES_ROWS, BLOCK), BF16)

    def window(blk, before, own, after, axis):
        parts = []
        for j in range(blk - 1, blk + 2):
            sel = [slice(None), slice(None)]
            if j < 0:
                parts.append(before)
            elif j >= ATTN_BLOCKS:
                parts.append(after)
            else:
                sel[axis] = slice(BLOCK * j, BLOCK * (j + 1))
                parts.append(own[tuple(sel)])
        return parts

    def scores(blk, k):
        pair = slice(BLOCK * (k // 2), BLOCK * (k // 2 + 1))
        band = window(blk, kp_ref[:, pair], ko_ref[:, pair], kn_ref[:, pair], 0)
        kc = jnp.concatenate(band + [kh_refs[blk][META_ROW0:BLOCK, pair]], axis=0)
        q0 = GQA_GROUP * HEAD_DIM * k
        qs = slice(BLOCK * blk, BLOCK * (blk + 1))
        qt4 = jnp.concatenate([qt_ref[q0 + HEAD_DIM * j:q0 + HEAD_DIM * (j + 1), qs]
                               for j in range(GQA_GROUP)], axis=1)
        rhs = jnp.concatenate([qt4, zero_q] if k % 2 == 0 else [zero_q, qt4], axis=0)
        return _dot(kc, rhs) + bias_ref[var_t[ATTN_BLOCKS * step + blk], k]

    def finish(blk, k, s):
        sink_row = jnp.concatenate([jnp.full((1, BLOCK), sink_ref[GQA_GROUP * k + j] * LOG2E, F32)
                                    for j in range(GQA_GROUP)], axis=1)
        m = jnp.maximum(jnp.max(s, axis=0, keepdims=True), sink_row)
        pb = jnp.exp2(s - m).astype(BF16)
        vs = slice(HEAD_DIM * k, HEAD_DIM * (k + 1))
        band = window(blk, vp_ref[vs, :], vo_ref[vs, :], vn_ref[vs, :], 1)
        vc = jnp.concatenate([jnp.concatenate(band, axis=1), ones_band], axis=0)
        vh = jnp.concatenate([vh_refs[blk][vs, :], ones_meta], axis=0)
        p_meta = jnp.concatenate([zero_p, pb[BAND_KEYS:KEYS]], axis=0)
        acc = _dot(vc, pb[0:BAND_KEYS]) + _dot(vh, p_meta)
        denom = acc[HEAD_DIM:HEAD_DIM + 1] + jnp.exp2(sink_row - m)
        ot = acc[0:HEAD_DIM] * (1.0 / denom)
        rows = slice(BLOCK * blk, BLOCK * (blk + 1))
        for j in range(GQA_GROUP // 2):
            two = jnp.concatenate([ot[:, 2 * BLOCK * j:2 * BLOCK * j + BLOCK],
                                   ot[:, 2 * BLOCK * j + BLOCK:2 * BLOCK * (j + 1)]], axis=0)
            c0 = GQA_GROUP * HEAD_DIM * k + BLOCK * j
            o_ref[rows, c0:c0 + BLOCK] = two.T.astype(BF16)

    units = [(blk, k) for blk in range(ATTN_BLOCKS) for k in range(N_KV_HEADS)]
    pending = [scores(*u) for u in units[:SCORE_LOOKAHEAD]]
    for idx, unit in enumerate(units):
        if idx + SCORE_LOOKAHEAD < len(units):
            pending.append(scores(*units[idx + SCORE_LOOKAHEAD]))
        finish(*unit, pending.pop(0))


def _attention(qt, kk, vt, sink, bias, lay, casts):
    t = kk.shape[0]
    nb = lay["nb"]
    n = ATTN_BLOCKS
    steps = nb // n
    cast_in, cast_out, cast_shape = [], [], []
    for w, layer in casts:
        rows, cols = w.shape[1:]
        chunks = min(1 << (steps.bit_length() - 1), rows // CAST_MIN_ROWS)
        assert rows % chunks == 0 and (rows // chunks) % CAST_MIN_ROWS == 0
        chunk = lambda i, m, v, c=chunks: jnp.minimum(i, c - 1)
        cast_in.append(pl.BlockSpec((None, rows // chunks, cols),
                                    lambda i, m, v, chunk=chunk, layer=layer: (layer, chunk(i, m, v), 0)))
        cast_out.append(pl.BlockSpec((None, rows // chunks, cols),
                                     lambda i, m, v, chunk=chunk: (0, chunk(i, m, v), 0)))
        cast_shape.append(jax.ShapeDtypeStruct((1, rows, cols), BF16))
    kspec = lambda rows, sel: pl.BlockSpec((rows, KV_WIDTH), sel)
    vspec = lambda cols, sel: pl.BlockSpec((KV_WIDTH, cols), sel)
    before = lambda i: jnp.maximum(n * i - 1, 0)
    after = lambda i: jnp.minimum(n * i + n, nb - 1)
    k_specs = [kspec(BLOCK, lambda i, m, v: (before(i), 0)),
               kspec(n * BLOCK, lambda i, m, v: (i, 0)),
               kspec(BLOCK, lambda i, m, v: (after(i), 0))]
    k_specs += [kspec(BLOCK, lambda i, m, v, j=j: (m[n * i + j], 0)) for j in range(n)]
    v_specs = [vspec(BLOCK, lambda i, m, v: (0, before(i))),
               vspec(n * BLOCK, lambda i, m, v: (0, i)),
               vspec(BLOCK, lambda i, m, v: (0, after(i)))]
    v_specs += [vspec(BLOCK, lambda i, m, v, j=j: (0, m[n * i + j])) for j in range(n)]
    grid_spec = pltpu.PrefetchScalarGridSpec(
        num_scalar_prefetch=2,
        grid=(nb // n,),
        in_specs=[pl.BlockSpec(memory_space=pltpu.SMEM),
                  pl.BlockSpec((D_MODEL, n * BLOCK), lambda i, m, v: (0, i))] + k_specs + v_specs
                 + [_const_spec((len(BLOCK_FLAG_VARIANTS), N_KV_HEADS, KEYS, GQA_GROUP * BLOCK),
                                lambda i, m, v: (0, 0, 0, 0))] + cast_in,
        out_specs=[pl.BlockSpec((n * BLOCK, D_MODEL), lambda i, m, v: (i, 0))] + cast_out,
    )
    outs = pl.pallas_call(
        functools.partial(_attn_kernel, n_cast=len(casts)),
        grid_spec=grid_spec,
        out_shape=[jax.ShapeDtypeStruct((t, D_MODEL), BF16)] + cast_shape,
        compiler_params=pltpu.CompilerParams(vmem_limit_bytes=VMEM_LIMIT,
                                             dimension_semantics=("arbitrary",)),
        name="band_attn",
    )(lay["meta"], lay["variant"], sink, qt, *([kk] * (3 + n)), *([vt] * (3 + n)), bias,
      *[w for w, _ in casts])
    return outs[0], outs[1:]


def _pool_diff(u_ext, t_ext, l_ext):
    return [_pool_group(u_ext, t_ext, l_ext, gi) for gi in range(len(POOL_WINDOWS))]


def _pool_group(u_ext, t_ext, l_ext, gi):
    w = POOL_WINDOWS[gi]
    n = u_ext.shape[0]
    rows = n - 2 * HALO
    valid = (t_ext >= 0) & (t_ext < l_ext)
    t = t_ext[HALO:HALO + rows]
    ln = l_ext[HALO:HALO + rows]
    x = jnp.where(valid, u_ext[:, gi * POOL_GROUP_WIDTH:(gi + 1) * POOL_GROUP_WIDTH], 0.0)
    acc, k = x, 1
    while k < w // 2:
        acc = acc + pltpu.roll(acc, n - k, 0)
        k *= 2
    win = acc + pltpu.roll(acc, w // 2, 0)
    cnt = jnp.minimum(t + w // 2, ln) - jnp.maximum(t - w // 2, 0)
    cnt = jnp.maximum(cnt, 1).astype(F32)
    return win[HALO:HALO + rows] / cnt - x[HALO:HALO + rows]


def _mix_kernel(t0_t, ln_t,
                attn_ref, g_ref, h_ref, hp_ref, hn_ref,
                wu_ref, wa_ref, wp_ref, ps_ref, wb_ref, wo_ref, lg_ref, lb_ref, o_ref, *, alpha):
    b0 = pl.program_id(0) * BLOCKS_PER_TILE

    def rows_of(entry, first, count):
        r = lax.broadcasted_iota(jnp.int32, (count, POOL_GROUP_WIDTH), 0) + first
        return r + t0_t[entry], jnp.zeros((count, POOL_GROUP_WIDTH), jnp.int32) + ln_t[entry]

    parts = [rows_of(b0, BLOCK - HALO, HALO)]
    parts += [rows_of(b0 + 1 + j, 0, BLOCK) for j in range(BLOCKS_PER_TILE)]
    parts += [rows_of(b0 + 1 + BLOCKS_PER_TILE, 0, HALO)]
    t_ext = jnp.concatenate([p[0] for p in parts], axis=0)
    l_ext = jnp.concatenate([p[1] for p in parts], axis=0)

    u_halves = [_dot(h_ref[_half(hf), :].astype(BF16), wu_ref[...]) for hf in range(2)]
    h_halo = jnp.concatenate([hp_ref[...], hn_ref[...]], axis=0).astype(BF16)
    u_halo = _dot(h_halo, wu_ref[...])
    u_ext = jnp.concatenate([u_halo[0:HALO]] + u_halves + [u_halo[HALO:2 * HALO]], axis=0)

    def pool(hf):
        ext = slice(HALF_TILE * hf, HALF_TILE * (hf + 1) + 2 * HALO)
        return _pool_diff(u_ext[ext], t_ext[ext], l_ext[ext])

    def attn_branch(hf):
        return _dot(attn_ref[_half(hf), :], wa_ref[...])

    def pool_branch(diffs):
        y = jnp.concatenate([_dot(d.astype(BF16), wp_ref[gi]) for gi, d in enumerate(diffs)], axis=1)
        return _dot((y * ps_ref[...]).astype(BF16), wb_ref[...])

    def gate(hf, ya, yb):
        g = g_ref[_half(hf), :].astype(F32)
        return (g[:, 0:D_MODEL] * ya + g[:, D_MODEL:2 * D_MODEL] * yb).astype(BF16)

    def norm(hf, mixed):
        o_ref[_half(hf), :] = _layer_norm(alpha * h_ref[_half(hf), :] + mixed, lg_ref[...], lb_ref[...])

    ya_a = attn_branch(0)
    pool_a = pool(0)
    ya_b = attn_branch(1)
    pool_b = pool(1)
    yb_a = pool_branch(pool_a)
    yb_b = pool_branch(pool_b)
    gated_a = gate(0, ya_a, yb_a)
    mixed_a = _dot(gated_a, wo_ref[...])
    gated_b = gate(1, ya_b, yb_b)
    mixed_b = _dot(gated_b, wo_ref[...])
    norm(0, mixed_a)
    norm(1, mixed_b)


def _mix(attn, g, h, w_in, weights, layer, lay, alpha):
    t = h.shape[0]
    n_halo = t // HALO
    per = TOKEN_TILE // HALO
    u_block = (D_MODEL + 2 * KV_WIDTH) // POOL_WIDTH
    row = lambda i, a, b: (i, 0)
    grid_spec = pltpu.PrefetchScalarGridSpec(
        num_scalar_prefetch=2,
        grid=(t // TOKEN_TILE,),
        in_specs=[pl.BlockSpec((TOKEN_TILE, D_MODEL), row),
                  pl.BlockSpec((TOKEN_TILE, GATE_WIDTH), row),
                  pl.BlockSpec((TOKEN_TILE, D_MODEL), row),
                  pl.BlockSpec((HALO, D_MODEL), lambda i, a, b: (jnp.maximum(i * per - 1, 0), 0)),
                  pl.BlockSpec((HALO, D_MODEL), lambda i, a, b: (jnp.minimum((i + 1) * per, n_halo - 1), 0)),
                  _const_spec((None, D_MODEL, POOL_WIDTH),
                              lambda i, a, b: (_layer_of(w_in, layer), 0, u_block))]
                 + [_layer_spec(w, layer) for w in weights],
        out_specs=pl.BlockSpec((TOKEN_TILE, D_MODEL), row),
    )
    return pl.pallas_call(
        functools.partial(_mix_kernel, alpha=alpha),
        grid_spec=grid_spec,
        out_shape=jax.ShapeDtypeStruct((t, D_MODEL), F32),
        compiler_params=pltpu.CompilerParams(vmem_limit_bytes=VMEM_LIMIT),
        name="mix_ln",
    )(lay["t0"], lay["ln"], attn, g, h, h, h, w_in, *weights)


def _mlp_body(x_ref, w1_ref, b1_ref, w2_ref, b2_ref, lg_ref, lb_ref, store, alpha):
    def ff(xb, chunks):
        acc = None
        for c in chunks:
            cs = slice(c * FF_CHUNK, (c + 1) * FF_CHUNK)
            a = jnp.maximum(_dot(xb, w1_ref[:, cs]) + b1_ref[:, cs], 0.0)
            part = _dot((a * a).astype(BF16), w2_ref[cs, :])
            acc = part if acc is None else acc + part
        return acc

    def norm(x, acc):
        return _layer_norm(alpha * x + (acc + b2_ref[...]), lg_ref[...], lb_ref[...])

    first, rest = (0,), tuple(range(1, D_FF // FF_CHUNK))
    x_a = x_ref[_half(0), :]
    x_b = x_ref[_half(1), :]
    xb_a, xb_b = x_a.astype(BF16), x_b.astype(BF16)
    acc_a = ff(xb_a, first + rest)
    acc_b = ff(xb_b, first)
    store(0, norm(x_a, acc_a))
    acc_b = acc_b + ff(xb_b, rest)
    store(1, norm(x_b, acc_b))


def _mlp_kernel(kind_t, dsta_t, dstb_t, *refs, alpha):
    del kind_t, dsta_t, dstb_t
    ins, o_ref = refs[:-1], refs[-1]

    def store(hf, value):
        o_ref[_half(hf), :] = value

    _mlp_body(*ins, store, alpha)


def _mlp_final_kernel(kind_t, dsta_t, dstb_t, *refs, alpha):
    ins, (ya_ref, yb_ref, buf_ref, sem_ref) = refs[:-4], refs[-4:]
    step = pl.program_id(0)
    slot = step % 2

    def store(hf, value):
        buf_ref[slot, _half(hf), :] = value

    _mlp_body(*ins, store, alpha)

    def copies(of_step, of_slot, act):
        for j in range(BLOCKS_PER_TILE):
            b = of_step * BLOCKS_PER_TILE + j
            for gi, (y_ref, dst_t) in enumerate(((ya_ref, dsta_t), (yb_ref, dstb_t))):
                @pl.when(kind_t[b] == gi)
                def _():
                    src = buf_ref.at[of_slot, pl.ds(BLOCK * j, BLOCK), :]
                    dst = y_ref.at[pl.ds(pl.multiple_of(dst_t[b] * BLOCK, BLOCK), BLOCK), :]
                    act(pltpu.make_async_copy(src, dst, sem_ref.at[of_slot, j]))

    copies(step, slot, lambda c: c.start())

    @pl.when(step > 0)
    def _():
        copies(step - 1, 1 - slot, lambda c: c.wait())

    @pl.when(step == pl.num_programs(0) - 1)
    def _():
        copies(step, slot, lambda c: c.wait())


def _mlp(x, weights, layer, lay, alpha, out_rows=None):
    t = x.shape[0]
    row = lambda i, *_: (i, 0)
    in_specs = [pl.BlockSpec((TOKEN_TILE, D_MODEL), row)] + [_layer_spec(w, layer) for w in weights]
    if out_rows is None:
        body, name = _mlp_kernel, "mlp_ln"
        out_specs = pl.BlockSpec((TOKEN_TILE, D_MODEL), row)
        out_shape = jax.ShapeDtypeStruct((t, D_MODEL), F32)
        scratch = []
    else:
        body, name = _mlp_final_kernel, "mlp_ln_out"
        out_specs = [pl.BlockSpec(memory_space=pl.ANY)] * 2
        out_shape = [jax.ShapeDtypeStruct((r, D_MODEL), F32) for r in out_rows]
        scratch = [pltpu.VMEM((2, TOKEN_TILE, D_MODEL), F32),
                   pltpu.SemaphoreType.DMA((2, BLOCKS_PER_TILE))]
    grid_spec = pltpu.PrefetchScalarGridSpec(
        num_scalar_prefetch=3, grid=(t // TOKEN_TILE,),
        in_specs=in_specs, out_specs=out_specs, scratch_shapes=scratch)
    return pl.pallas_call(
        functools.partial(body, alpha=alpha),
        grid_spec=grid_spec,
        out_shape=out_shape,
        compiler_params=pltpu.CompilerParams(vmem_limit_bytes=VMEM_LIMIT,
                                             dimension_semantics=("arbitrary",)),
        name=name,
    )(lay["kind"], lay["src"][0], lay["src"][1], x, *weights)


def kernel(x_prompt, x_sample, meta_tokens, ln_emb_g, ln_emb_b, w_in, sink, w_pool, pool_scale, w_bo_attn,
           w_bo_pool, w_out, ln1_g, ln1_b, w_mlp1, b_mlp1, w_mlp2, b_mlp2, ln2_g, ln2_b):
    depth = w_in.shape[0]
    alpha = float((2 * depth) ** 0.25)
    groups = (x_prompt, x_sample)
    lay = _layout([x.shape[:2] for x in groups])
    header = jnp.concatenate([jnp.zeros((META_ROW0, D_MODEL), F32), meta_tokens.astype(F32)], axis=0)

    row3 = lambda a: a.reshape(a.shape[0], 1, a.shape[-1])
    w_qvt = _qv_transposed(w_in)
    w_pool_rows = w_pool.reshape(depth, -1, POOL_GROUP_WIDTH)
    bias = jnp.asarray(_alibi_bias_table())

    h = _embed(x_prompt.reshape(-1, D_MODEL), x_sample.reshape(-1, D_MODEL), header,
               ln_emb_g.reshape(1, -1), ln_emb_b.reshape(1, -1), lay)
    out_rows = [x.shape[0] * x.shape[1] for x in groups]
    w_in_b = w_in[0:1].astype(BF16)
    for l in range(depth):
        qt, kk, vt, g = _inproj(h, w_in_b, w_qvt, l)
        casts = [(w_bo_attn, l), (w_pool_rows, l), (w_bo_pool, l), (w_out, l), (w_mlp1, l), (w_mlp2, l)]
        if l + 1 < depth:
            casts.append((w_in, l + 1))
        attn, cast = _attention(qt, kk, vt, sink[l].astype(F32), bias, lay, casts)
        wa_b, wp_b, wb_b, wo_b, w1_b, w2_b = cast[:6]
        mix_w = (wa_b, wp_b.reshape((1,) + w_pool.shape[1:]), row3(pool_scale), wb_b, wo_b,
                 row3(ln1_g), row3(ln1_b))
        x1 = _mix(attn, g, h, w_in_b, mix_w, l, lay, alpha)
        mlp_w = (w1_b, row3(b_mlp1), w2_b, row3(b_mlp2), row3(ln2_g), row3(ln2_b))
        h = _mlp(x1, mlp_w, l, lay, alpha, out_rows if l == depth - 1 else None)
        if l + 1 < depth:
            w_in_b = cast[6]
    return tuple(y.reshape(x.shape) for y, x in zip(h, groups))
```

```python
import functools

import numpy as np
import jax
import jax.numpy as jnp
from jax import lax
from jax.experimental import pallas as pl
from jax.experimental.pallas import tpu as pltpu

D_MODEL = 1024
N_META = 16
N_HEADS = 16
N_KV_HEADS = 4
HEAD_DIM = 64
GQA_GROUP = N_HEADS // N_KV_HEADS
WINDOW = 128
BLOCK = 128
POOL_WINDOWS = (2, 4, 8, 16)
POOL_GROUP_WIDTH = 128
POOL_WIDTH = 512
KV_WIDTH = N_KV_HEADS * HEAD_DIM
GATE_WIDTH = 2 * D_MODEL
D_FF = 4 * D_MODEL
FF_CHUNK = 1024
LN_EPS = 1e-5
HALO = 8
META_ROW0 = BLOCK - N_META
BAND_KEYS = 3 * BLOCK
KEYS = BAND_KEYS + N_META
NEG = -1e30
LOG2E = 1.4426950408889634
ONES_ROWS = 16
BLOCK_FLAG_VARIANTS = (7, 6, 3, 4, 2)
TOKEN_TILE = 1024
HALF_TILE = TOKEN_TILE // 2
BLOCKS_PER_TILE = TOKEN_TILE // BLOCK
ATTN_BLOCKS = 8
SCORE_LOOKAHEAD = 2
CAST_MIN_ROWS = 16
VMEM_LIMIT = 56 * 1024 * 1024

F32 = jnp.float32
BF16 = jnp.bfloat16
NT_DIMS = (((1,), (1,)), ((), ()))


def _layout(group_shapes):
    seqs = []
    n = 0
    for gi, (bsz, s) in enumerate(group_shapes):
        assert s % BLOCK == 0 and s >= BLOCK
        for bi in range(bsz):
            seqs.append((gi, bi, s // BLOCK, n))
            n += 1 + s // BLOCK
    nb = -(-n // BLOCKS_PER_TILE) * BLOCKS_PER_TILE
    meta_i = np.arange(nb, dtype=np.int32)
    flags = np.full(nb, 2, dtype=np.int32)
    t0 = np.zeros(nb + 2, dtype=np.int32)
    ln = np.zeros(nb + 2, dtype=np.int32)
    kind = np.full(nb, -2, dtype=np.int32)
    src = np.zeros((len(group_shapes), nb), dtype=np.int32)
    for gi, bi, nreal, b0 in seqs:
        for j in range(nreal + 1):
            b = b0 + j
            meta_i[b] = b0
            t0[b + 1] = -META_ROW0 + BLOCK * j
            ln[b + 1] = N_META + BLOCK * nreal
            flags[b] = (0 if j == 0 else 2) | (1 if j >= 2 else 0) | (4 if j < nreal else 0)
            kind[b] = -1 if j == 0 else gi
            if j > 0:
                src[gi, b:] = bi * nreal + (j - 1)
    variant = np.asarray([BLOCK_FLAG_VARIANTS.index(int(f)) for f in flags], dtype=np.int32)
    return dict(nb=nb, meta=meta_i, variant=variant, t0=t0, ln=ln, kind=kind, src=src)


def _alibi_bias_table():
    nv = len(BLOCK_FLAG_VARIANTS)
    slopes = 2.0 ** (-8.0 * np.arange(1, N_HEADS + 1) / N_HEADS)
    key = np.arange(BAND_KEYS)[:, None]
    qry = np.arange(BLOCK)[None, :]
    rel = np.abs(key - BLOCK - qry).astype(np.float32)[None, :, None, :]
    scaled = (slopes * LOG2E).astype(np.float32).reshape(N_KV_HEADS, 1, GQA_GROUP, 1)
    band = np.where(rel <= WINDOW, -(scaled * rel), np.float32(NEG))
    flags = np.asarray(BLOCK_FLAG_VARIANTS)[:, None]
    usable = (flags >> (np.arange(BAND_KEYS)[None, :] // BLOCK)) & 1
    band = np.where(usable[:, None, :, None, None] != 0, band[None], np.float32(NEG))
    band = band.reshape(nv, N_KV_HEADS, BAND_KEYS, GQA_GROUP * BLOCK)
    meta = np.zeros((nv, N_KV_HEADS, N_META, GQA_GROUP * BLOCK), np.float32)
    return np.concatenate([band, meta], axis=2).astype(np.float32)


def _const_spec(shape, index_map):
    return pl.BlockSpec(shape, index_map, pipeline_mode=pl.Buffered(1))


def _layer_of(stacked, layer):
    return layer if stacked.shape[0] > 1 else 0


def _layer_spec(stacked, layer):
    rest = stacked.shape[1:]
    index = (_layer_of(stacked, layer),) + (0,) * len(rest)
    return _const_spec((None,) + rest, lambda *_: index)


def _layer_norm(x, g, b):
    mu = jnp.mean(x, axis=-1, keepdims=True)
    xc = x - mu
    var = jnp.mean(xc * xc, axis=-1, keepdims=True)
    return xc * lax.rsqrt(var + LN_EPS) * g + b


def _dot(a, b):
    return jnp.dot(a, b, preferred_element_type=F32)


def _half(hf):
    return slice(HALF_TILE * hf, HALF_TILE * (hf + 1))


def _qvt_kernel(w_ref, o_ref):
    o_ref[...] = w_ref[...].T.astype(BF16)


def _qv_transposed(w_in):
    depth = w_in.shape[0]
    q_blocks = D_MODEL // KV_WIDTH
    v_block0 = (D_MODEL + KV_WIDTH) // KV_WIDTH
    n_blocks = q_blocks + 1
    src_col = lambda c: jnp.where(c < q_blocks, c, c - q_blocks + v_block0)
    return pl.pallas_call(
        _qvt_kernel,
        grid=(depth, n_blocks),
        in_specs=[pl.BlockSpec((None, D_MODEL, KV_WIDTH), lambda l, c: (l, 0, src_col(c)))],
        out_specs=pl.BlockSpec((None, KV_WIDTH, D_MODEL), lambda l, c: (l, c, 0)),
        out_shape=jax.ShapeDtypeStruct((depth, D_MODEL + KV_WIDTH, D_MODEL), BF16),
        name="qv_transpose",
    )(w_in)


def _project(hf, h_ref, w_ref, wt_ref, qt_ref, k_ref, vt_ref, g_ref):
    k0 = D_MODEL
    g0 = D_MODEL + 2 * KV_WIDTH + POOL_WIDTH
    rows = _half(hf)
    x = h_ref[rows, :].astype(BF16)
    gates = _dot(x, w_ref[:, g0:g0 + GATE_WIDTH])
    g_ref[rows, :] = (0.5 * jnp.tanh(0.5 * gates) + 0.5).astype(BF16)
    qt = lax.dot_general(wt_ref[0:D_MODEL, :], x, NT_DIMS, preferred_element_type=F32)
    qt_ref[:, rows] = (qt * (HEAD_DIM ** -0.5 * LOG2E)).astype(BF16)
    vt = lax.dot_general(wt_ref[D_MODEL:D_MODEL + KV_WIDTH, :], x, NT_DIMS, preferred_element_type=F32)
    vt_ref[:, rows] = vt.astype(BF16)
    k_ref[rows, :] = _dot(x, w_ref[:, k0:k0 + KV_WIDTH]).astype(BF16)


def _inproj_kernel(h_ref, *refs):
    _project(0, h_ref, *refs)
    _project(1, h_ref, *refs)


def _inproj_embed_kernel(kind_t, srca_t, srcb_t, *refs):
    del srca_t, srcb_t
    xa_refs, xb_refs = refs[:BLOCKS_PER_TILE], refs[BLOCKS_PER_TILE:2 * BLOCKS_PER_TILE]
    hdr_ref, eg_ref, eb_ref = refs[2 * BLOCKS_PER_TILE:2 * BLOCKS_PER_TILE + 3]
    w_ref, wt_ref, h_ref, qt_ref, k_ref, vt_ref, g_ref = refs[2 * BLOCKS_PER_TILE + 3:]
    for hf in range(2):
        for j in range(hf * BLOCKS_PER_TILE // 2, (hf + 1) * BLOCKS_PER_TILE // 2):
            kind = kind_t[pl.program_id(0) * BLOCKS_PER_TILE + j]
            x = jnp.where(kind == 0, xa_refs[j][...], jnp.where(kind == 1, xb_refs[j][...], 0.0))
            x = jnp.where(kind == -1, hdr_ref[...], x)
            h_ref[BLOCK * j:BLOCK * (j + 1), :] = _layer_norm(x, eg_ref[...], eb_ref[...])
        _project(hf, h_ref, w_ref, wt_ref, qt_ref, k_ref, vt_ref, g_ref)


def _inproj_out(t, row, col):
    specs = [pl.BlockSpec((D_MODEL, TOKEN_TILE), col),
             pl.BlockSpec((TOKEN_TILE, KV_WIDTH), row),
             pl.BlockSpec((KV_WIDTH, TOKEN_TILE), col),
             pl.BlockSpec((TOKEN_TILE, GATE_WIDTH), row)]
    shapes = [jax.ShapeDtypeStruct((D_MODEL, t), BF16),
              jax.ShapeDtypeStruct((t, KV_WIDTH), BF16),
              jax.ShapeDtypeStruct((KV_WIDTH, t), BF16),
              jax.ShapeDtypeStruct((t, GATE_WIDTH), BF16)]
    return specs, shapes


def _inproj_embed(xa, xb, header, eg, eb, w_in, w_qvt, lay):
    t = lay["nb"] * BLOCK
    n = BLOCKS_PER_TILE
    blk = lambda sel: pl.BlockSpec((BLOCK, D_MODEL), sel)
    from_a = [blk(lambda i, k, sa, sb, j=j: (sa[i * n + j], 0)) for j in range(n)]
    from_b = [blk(lambda i, k, sa, sb, j=j: (sb[i * n + j], 0)) for j in range(n)]
    const = lambda i, *_: (0, 0)
    row = lambda i, *_: (i, 0)
    col = lambda i, *_: (0, i)
    out_specs, out_shape = _inproj_out(t, row, col)
    grid_spec = pltpu.PrefetchScalarGridSpec(
        num_scalar_prefetch=3,
        grid=(t // TOKEN_TILE,),
        in_specs=from_a + from_b + [blk(const), pl.BlockSpec((1, D_MODEL), const),
                                    pl.BlockSpec((1, D_MODEL), const),
                                    _layer_spec(w_in, 0), _layer_spec(w_qvt, 0)],
        out_specs=[pl.BlockSpec((TOKEN_TILE, D_MODEL), row)] + out_specs,
    )
    return pl.pallas_call(
        _inproj_embed_kernel,
        grid_spec=grid_spec,
        out_shape=[jax.ShapeDtypeStruct((t, D_MODEL), F32)] + out_shape,
        compiler_params=pltpu.CompilerParams(vmem_limit_bytes=VMEM_LIMIT),
        name="embed_in_proj",
    )(lay["kind"], lay["src"][0], lay["src"][1], *([xa] * n), *([xb] * n), header, eg, eb, w_in, w_qvt)


def _inproj(h, w_in, w_qvt, layer):
    t = h.shape[0]
    row = lambda i: (i, 0)
    col = lambda i: (0, i)
    out_specs, out_shape = _inproj_out(t, row, col)
    return pl.pallas_call(
        _inproj_kernel,
        grid=(t // TOKEN_TILE,),
        in_specs=[pl.BlockSpec((TOKEN_TILE, D_MODEL), row),
                  _layer_spec(w_in, layer), _layer_spec(w_qvt, layer)],
        out_specs=out_specs,
        out_shape=out_shape,
        compiler_params=pltpu.CompilerParams(vmem_limit_bytes=VMEM_LIMIT),
        name="in_proj",
    )(h, w_in, w_qvt)


def _attn_kernel(meta_t, var_t, sink_ref, qt_ref, *refs, n_cast):
    del meta_t
    kp_ref, ko_ref, kn_ref = refs[0:3]
    kh_refs = refs[3:3 + ATTN_BLOCKS]
    vp_ref, vo_ref, vn_ref = refs[3 + ATTN_BLOCKS:6 + ATTN_BLOCKS]
    vh_refs = refs[6 + ATTN_BLOCKS:6 + 2 * ATTN_BLOCKS]
    n_in = 7 + 2 * ATTN_BLOCKS + n_cast
    bias_ref, o_ref = refs[6 + 2 * ATTN_BLOCKS], refs[n_in]
    for src_ref, dst_ref in zip(refs[n_in - n_cast:n_in], refs[n_in + 1:]):
        dst_ref[...] = src_ref[...].astype(BF16)
    step = pl.program_id(0)
    width = GQA_GROUP * BLOCK
    zero_q = jnp.zeros((HEAD_DIM, width), BF16)
    zero_p = jnp.zeros((META_ROW0, width), BF16)
    ones_band = jnp.ones((ONES_ROWS, BAND_KEYS), BF16)
    ones_meta = jnp.ones((ONES_ROWS, BLOCK), BF16)

    def window(blk, before, own, after, axis):
        parts = []
        for j in range(blk - 1, blk + 2):
            sel = [slice(None), slice(None)]
            if j < 0:
                parts.append(before)
            elif j >= ATTN_BLOCKS:
                parts.append(after)
            else:
                sel[axis] = slice(BLOCK * j, BLOCK * (j + 1))
                parts.append(own[tuple(sel)])
        return parts

    def scores(blk, k):
        pair = slice(BLOCK * (k // 2), BLOCK * (k // 2 + 1))
        band = window(blk, kp_ref[:, pair], ko_ref[:, pair], kn_ref[:, pair], 0)
        kc = jnp.concatenate(band + [kh_refs[blk][META_ROW0:BLOCK, pair]], axis=0)
        q0 = GQA_GROUP * HEAD_DIM * k
        qs = slice(BLOCK * blk, BLOCK * (blk + 1))
        qt4 = jnp.concatenate([qt_ref[q0 + HEAD_DIM * j:q0 + HEAD_DIM * (j + 1), qs]
                               for j in range(GQA_GROUP)], axis=1)
        rhs = jnp.concatenate([qt4, zero_q] if k % 2 == 0 else [zero_q, qt4], axis=0)
        return _dot(kc, rhs) + bias_ref[var_t[ATTN_BLOCKS * step + blk], k]

    def finish(blk, k, s):
        sink_row = jnp.concatenate([jnp.full((1, BLOCK), sink_ref[GQA_GROUP * k + j] * LOG2E, F32)
                                    for j in range(GQA_GROUP)], axis=1)
        m = jnp.maximum(jnp.max(s, axis=0, keepdims=True), sink_row)
        pb = jnp.exp2(s - m).astype(BF16)
        vs = slice(HEAD_DIM * k, HEAD_DIM * (k + 1))
        band = window(blk, vp_ref[vs, :], vo_ref[vs, :], vn_ref[vs, :], 1)
        vc = jnp.concatenate([jnp.concatenate(band, axis=1), ones_band], axis=0)
        vh = jnp.concatenate([vh_refs[blk][vs, :], ones_meta], axis=0)
        p_meta = jnp.concatenate([zero_p, pb[BAND_KEYS:KEYS]], axis=0)
        acc = _dot(vc, pb[0:BAND_KEYS]) + _dot(vh, p_meta)
        denom = acc[HEAD_DIM:HEAD_DIM + 1] + jnp.exp2(sink_row - m)
        ot = acc[0:HEAD_DIM] * (1.0 / denom)
        rows = slice(BLOCK * blk, BLOCK * (blk + 1))
        for j in range(GQA_GROUP // 2):
            two = jnp.concatenate([ot[:, 2 * BLOCK * j:2 * BLOCK * j + BLOCK],
                                   ot[:, 2 * BLOCK * j + BLOCK:2 * BLOCK * (j + 1)]], axis=0)
            c0 = GQA_GROUP * HEAD_DIM * k + BLOCK * j
            o_ref[rows, c0:c0 + BLOCK] = two.T.astype(BF16)

    units = [(blk, k) for blk in range(ATTN_BLOCKS) for k in range(N_KV_HEADS)]
    pending = [scores(*u) for u in units[:SCORE_LOOKAHEAD]]
    for idx, unit in enumerate(units):
        if idx + SCORE_LOOKAHEAD < len(units):
            pending.append(scores(*units[idx + SCORE_LOOKAHEAD]))
        finish(*unit, pending.pop(0))


def _attention(qt, kk, vt, sink, bias, lay, casts):
    t = kk.shape[0]
    nb = lay["nb"]
    n = ATTN_BLOCKS
    steps = nb // n
    cast_in, cast_out, cast_shape = [], [], []
    for w, layer in casts:
        rows, cols = w.shape[1:]
        chunks = min(1 << (steps.bit_length() - 1), rows // CAST_MIN_ROWS)
        assert rows % chunks == 0 and (rows // chunks) % CAST_MIN_ROWS == 0
        chunk = lambda i, m, v, c=chunks: jnp.minimum(i, c - 1)
        cast_in.append(pl.BlockSpec((None, rows // chunks, cols),
                                    lambda i, m, v, chunk=chunk, layer=layer: (layer, chunk(i, m, v), 0)))
        cast_out.append(pl.BlockSpec((None, rows // chunks, cols),
                                     lambda i, m, v, chunk=chunk: (0, chunk(i, m, v), 0)))
        cast_shape.append(jax.ShapeDtypeStruct((1, rows, cols), BF16))
    kspec = lambda rows, sel: pl.BlockSpec((rows, KV_WIDTH), sel)
    vspec = lambda cols, sel: pl.BlockSpec((KV_WIDTH, cols), sel)
    before = lambda i: jnp.maximum(n * i - 1, 0)
    after = lambda i: jnp.minimum(n * i + n, nb - 1)
    k_specs = [kspec(BLOCK, lambda i, m, v: (before(i), 0)),
               kspec(n * BLOCK, lambda i, m, v: (i, 0)),
               kspec(BLOCK, lambda i, m, v: (after(i), 0))]
    k_specs += [kspec(BLOCK, lambda i, m, v, j=j: (m[n * i + j], 0)) for j in range(n)]
    v_specs = [vspec(BLOCK, lambda i, m, v: (0, before(i))),
               vspec(n * BLOCK, lambda i, m, v: (0, i)),
               vspec(BLOCK, lambda i, m, v: (0, after(i)))]
    v_specs += [vspec(BLOCK, lambda i, m, v, j=j: (0, m[n * i + j])) for j in range(n)]
    grid_spec = pltpu.PrefetchScalarGridSpec(
        num_scalar_prefetch=2,
        grid=(nb // n,),
        in_specs=[pl.BlockSpec(memory_space=pltpu.SMEM),
                  pl.BlockSpec((D_MODEL, n * BLOCK), lambda i, m, v: (0, i))] + k_specs + v_specs
                 + [_const_spec((len(BLOCK_FLAG_VARIANTS), N_KV_HEADS, KEYS, GQA_GROUP * BLOCK),
                                lambda i, m, v: (0, 0, 0, 0))] + cast_in,
        out_specs=[pl.BlockSpec((n * BLOCK, D_MODEL), lambda i, m, v: (i, 0))] + cast_out,
    )
    outs = pl.pallas_call(
        functools.partial(_attn_kernel, n_cast=len(casts)),
        grid_spec=grid_spec,
        out_shape=[jax.ShapeDtypeStruct((t, D_MODEL), BF16)] + cast_shape,
        compiler_params=pltpu.CompilerParams(vmem_limit_bytes=VMEM_LIMIT,
                                             dimension_semantics=("arbitrary",)),
        name="band_attn",
    )(lay["meta"], lay["variant"], sink, qt, *([kk] * (3 + n)), *([vt] * (3 + n)), bias,
      *[w for w, _ in casts])
    return outs[0], outs[1:]


def _pool_diff(u_ext, t_ext, l_ext):
    return [_pool_group(u_ext, t_ext, l_ext, gi) for gi in range(len(POOL_WINDOWS))]


def _pool_group(u_ext, t_ext, l_ext, gi):
    w = POOL_WINDOWS[gi]
    n = u_ext.shape[0]
    rows = n - 2 * HALO
    valid = (t_ext >= 0) & (t_ext < l_ext)
    t = t_ext[HALO:HALO + rows]
    ln = l_ext[HALO:HALO + rows]
    x = jnp.where(valid, u_ext[:, gi * POOL_GROUP_WIDTH:(gi + 1) * POOL_GROUP_WIDTH], 0.0)
    acc, k = x, 1
    while k < w // 2:
        acc = acc + pltpu.roll(acc, n - k, 0)
        k *= 2
    win = acc + pltpu.roll(acc, w // 2, 0)
    cnt = jnp.minimum(t + w // 2, ln) - jnp.maximum(t - w // 2, 0)
    cnt = jnp.maximum(cnt, 1).astype(F32)
    return win[HALO:HALO + rows] / cnt - x[HALO:HALO + rows]


def _mix_kernel(t0_t, ln_t,
                attn_ref, g_ref, h_ref, hp_ref, hn_ref,
                wu_ref, wa_ref, wp_ref, ps_ref, wb_ref, wo_ref, lg_ref, lb_ref, o_ref, *, alpha):
    b0 = pl.program_id(0) * BLOCKS_PER_TILE

    def rows_of(entry, first, count):
        r = lax.broadcasted_iota(jnp.int32, (count, POOL_GROUP_WIDTH), 0) + first
        return r + t0_t[entry], jnp.zeros((count, POOL_GROUP_WIDTH), jnp.int32) + ln_t[entry]

    parts = [rows_of(b0, BLOCK - HALO, HALO)]
    parts += [rows_of(b0 + 1 + j, 0, BLOCK) for j in range(BLOCKS_PER_TILE)]
    parts += [rows_of(b0 + 1 + BLOCKS_PER_TILE, 0, HALO)]
    t_ext = jnp.concatenate([p[0] for p in parts], axis=0)
    l_ext = jnp.concatenate([p[1] for p in parts], axis=0)

    u_halves = [_dot(h_ref[_half(hf), :].astype(BF16), wu_ref[...]) for hf in range(2)]
    h_halo = jnp.concatenate([hp_ref[...], hn_ref[...]], axis=0).astype(BF16)
    u_halo = _dot(h_halo, wu_ref[...])
    u_ext = jnp.concatenate([u_halo[0:HALO]] + u_halves + [u_halo[HALO:2 * HALO]], axis=0)

    def pool(hf):
        ext = slice(HALF_TILE * hf, HALF_TILE * (hf + 1) + 2 * HALO)
        return _pool_diff(u_ext[ext], t_ext[ext], l_ext[ext])

    def attn_branch(hf):
        return _dot(attn_ref[_half(hf), :], wa_ref[...])

    def pool_branch(diffs):
        y = jnp.concatenate([_dot(d.astype(BF16), wp_ref[gi]) for gi, d in enumerate(diffs)], axis=1)
        return _dot((y * ps_ref[...]).astype(BF16), wb_ref[...])

    def gate(hf, ya, yb):
        g = g_ref[_half(hf), :].astype(F32)
        return (g[:, 0:D_MODEL] * ya + g[:, D_MODEL:2 * D_MODEL] * yb).astype(BF16)

    def norm(hf, mixed):
        o_ref[_half(hf), :] = _layer_norm(alpha * h_ref[_half(hf), :] + mixed, lg_ref[...], lb_ref[...])

    ya_a = attn_branch(0)
    pool_a = pool(0)
    ya_b = attn_branch(1)
    pool_b = pool(1)
    yb_a = pool_branch(pool_a)
    yb_b = pool_branch(pool_b)
    gated_a = gate(0, ya_a, yb_a)
    mixed_a = _dot(gated_a, wo_ref[...])
    gated_b = gate(1, ya_b, yb_b)
    mixed_b = _dot(gated_b, wo_ref[...])
    norm(0, mixed_a)
    norm(1, mixed_b)


def _mix(attn, g, h, w_in, weights, layer, lay, alpha):
    t = h.shape[0]
    n_halo = t // HALO
    per = TOKEN_TILE // HALO
    u_block = (D_MODEL + 2 * KV_WIDTH) // POOL_WIDTH
    row = lambda i, a, b: (i, 0)
    grid_spec = pltpu.PrefetchScalarGridSpec(
        num_scalar_prefetch=2,
        grid=(t // TOKEN_TILE,),
        in_specs=[pl.BlockSpec((TOKEN_TILE, D_MODEL), row),
                  pl.BlockSpec((TOKEN_TILE, GATE_WIDTH), row),
                  pl.BlockSpec((TOKEN_TILE, D_MODEL), row),
                  pl.BlockSpec((HALO, D_MODEL), lambda i, a, b: (jnp.maximum(i * per - 1, 0), 0)),
                  pl.BlockSpec((HALO, D_MODEL), lambda i, a, b: (jnp.minimum((i + 1) * per, n_halo - 1), 0)),
                  _const_spec((None, D_MODEL, POOL_WIDTH),
                              lambda i, a, b: (_layer_of(w_in, layer), 0, u_block))]
                 + [_layer_spec(w, layer) for w in weights],
        out_specs=pl.BlockSpec((TOKEN_TILE, D_MODEL), row),
    )
    return pl.pallas_call(
        functools.partial(_mix_kernel, alpha=alpha),
        grid_spec=grid_spec,
        out_shape=jax.ShapeDtypeStruct((t, D_MODEL), F32),
        compiler_params=pltpu.CompilerParams(vmem_limit_bytes=VMEM_LIMIT),
        name="mix_ln",
    )(lay["t0"], lay["ln"], attn, g, h, h, h, w_in, *weights)


def _mlp_body(x_ref, w1_ref, b1_ref, w2_ref, b2_ref, lg_ref, lb_ref, store, alpha):
    def ff(xb, chunks):
        acc = None
        for c in chunks:
            cs = slice(c * FF_CHUNK, (c + 1) * FF_CHUNK)
            a = jnp.maximum(_dot(xb, w1_ref[:, cs]) + b1_ref[:, cs], 0.0)
            part = _dot((a * a).astype(BF16), w2_ref[cs, :])
            acc = part if acc is None else acc + part
        return acc

    def norm(x, acc):
        return _layer_norm(alpha * x + (acc + b2_ref[...]), lg_ref[...], lb_ref[...])

    first, rest = (0,), tuple(range(1, D_FF // FF_CHUNK))
    x_a = x_ref[_half(0), :]
    x_b = x_ref[_half(1), :]
    xb_a, xb_b = x_a.astype(BF16), x_b.astype(BF16)
    acc_a = ff(xb_a, first + rest)
    acc_b = ff(xb_b, first)
    store(0, norm(x_a, acc_a))
    acc_b = acc_b + ff(xb_b, rest)
    store(1, norm(x_b, acc_b))


def _mlp_kernel(kind_t, dsta_t, dstb_t, *refs, alpha):
    del kind_t, dsta_t, dstb_t
    ins, o_ref = refs[:-1], refs[-1]

    def store(hf, value):
        o_ref[_half(hf), :] = value

    _mlp_body(*ins, store, alpha)


def _mlp_final_kernel(kind_t, dsta_t, dstb_t, *refs, alpha):
    ins, (ya_ref, yb_ref, buf_ref, sem_ref) = refs[:-4], refs[-4:]
    step = pl.program_id(0)
    slot = step % 2

    def store(hf, value):
        buf_ref[slot, _half(hf), :] = value

    _mlp_body(*ins, store, alpha)

    def copies(of_step, of_slot, act):
        for j in range(BLOCKS_PER_TILE):
            b = of_step * BLOCKS_PER_TILE + j
            for gi, (y_ref, dst_t) in enumerate(((ya_ref, dsta_t), (yb_ref, dstb_t))):
                @pl.when(kind_t[b] == gi)
                def _():
                    src = buf_ref.at[of_slot, pl.ds(BLOCK * j, BLOCK), :]
                    dst = y_ref.at[pl.ds(pl.multiple_of(dst_t[b] * BLOCK, BLOCK), BLOCK), :]
                    act(pltpu.make_async_copy(src, dst, sem_ref.at[of_slot, j]))

    copies(step, slot, lambda c: c.start())

    @pl.when(step > 0)
    def _():
        copies(step - 1, 1 - slot, lambda c: c.wait())

    @pl.when(step == pl.num_programs(0) - 1)
    def _():
        copies(step, slot, lambda c: c.wait())


def _mlp(x, weights, layer, lay, alpha, out_rows=None):
    t = x.shape[0]
    row = lambda i, *_: (i, 0)
    in_specs = [pl.BlockSpec((TOKEN_TILE, D_MODEL), row)] + [_layer_spec(w, layer) for w in weights]
    if out_rows is None:
        body, name = _mlp_kernel, "mlp_ln"
        out_specs = pl.BlockSpec((TOKEN_TILE, D_MODEL), row)
        out_shape = jax.ShapeDtypeStruct((t, D_MODEL), F32)
        scratch = []
    else:
        body, name = _mlp_final_kernel, "mlp_ln_out"
        out_specs = [pl.BlockSpec(memory_space=pl.ANY)] * 2
        out_shape = [jax.ShapeDtypeStruct((r, D_MODEL), F32) for r in out_rows]
        scratch = [pltpu.VMEM((2, TOKEN_TILE, D_MODEL), F32),
                   pltpu.SemaphoreType.DMA((2, BLOCKS_PER_TILE))]
    grid_spec = pltpu.PrefetchScalarGridSpec(
        num_scalar_prefetch=3, grid=(t // TOKEN_TILE,),
        in_specs=in_specs, out_specs=out_specs, scratch_shapes=scratch)
    return pl.pallas_call(
        functools.partial(body, alpha=alpha),
        grid_spec=grid_spec,
        out_shape=out_shape,
        compiler_params=pltpu.CompilerParams(vmem_limit_bytes=VMEM_LIMIT,
                                             dimension_semantics=("arbitrary",)),
        name=name,
    )(lay["kind"], lay["src"][0], lay["src"][1], x, *weights)


def kernel(x_prompt, x_sample, meta_tokens, ln_emb_g, ln_emb_b, w_in, sink, w_pool, pool_scale, w_bo_attn,
           w_bo_pool, w_out, ln1_g, ln1_b, w_mlp1, b_mlp1, w_mlp2, b_mlp2, ln2_g, ln2_b):
    depth = w_in.shape[0]
    alpha = float((2 * depth) ** 0.25)
    groups = (x_prompt, x_sample)
    lay = _layout([x.shape[:2] for x in groups])
    header = jnp.concatenate([jnp.zeros((META_ROW0, D_MODEL), F32), meta_tokens.astype(F32)], axis=0)

    row3 = lambda a: a.reshape(a.shape[0], 1, a.shape[-1])
    w_qvt = _qv_transposed(w_in)
    w_pool_rows = w_pool.reshape(depth, -1, POOL_GROUP_WIDTH)
    bias = jnp.asarray(_alibi_bias_table())

    out_rows = [x.shape[0] * x.shape[1] for x in groups]
    w_in_b = w_in[0:1].astype(BF16)
    for l in range(depth):
        if l == 0:
            h, qt, kk, vt, g = _inproj_embed(
                x_prompt.reshape(-1, D_MODEL), x_sample.reshape(-1, D_MODEL), header,
                ln_emb_g.reshape(1, -1), ln_emb_b.reshape(1, -1), w_in_b, w_qvt, lay)
        else:
            qt, kk, vt, g = _inproj(h, w_in_b, w_qvt, l)
        casts = [(w_bo_attn, l), (w_pool_rows, l), (w_bo_pool, l), (w_out, l), (w_mlp1, l), (w_mlp2, l)]
        if l + 1 < depth:
            casts.append((w_in, l + 1))
        attn, cast = _attention(qt, kk, vt, sink[l].astype(F32), bias, lay, casts)
        wa_b, wp_b, wb_b, wo_b, w1_b, w2_b = cast[:6]
        mix_w = (wa_b, wp_b.reshape((1,) + w_pool.shape[1:]), row3(pool_scale), wb_b, wo_b,
                 row3(ln1_g), row3(ln1_b))
        x1 = _mix(attn, g, h, w_in_b, mix_w, l, lay, alpha)
        mlp_w = (w1_b, row3(b_mlp1), w2_b, row3(b_mlp2), row3(ln2_g), row3(ln2_b))
        h = _mlp(x1, mlp_w, l, lay, alpha, out_rows if l == depth - 1 else None)
        if l + 1 < depth:
            w_in_b = cast[6]
    return tuple(y.reshape(x.shape) for y, x in zip(h, groups))
```

```python
import functools

import numpy as np
import jax
import jax.numpy as jnp
from jax import lax
from jax.experimental import pallas as pl
from jax.experimental.pallas import tpu as pltpu

D_MODEL = 1024
N_META = 16
N_HEADS = 16
N_KV_HEADS = 4
HEAD_DIM = 64
GQA_GROUP = N_HEADS // N_KV_HEADS
WINDOW = 128
BLOCK = 128
POOL_WINDOWS = (2, 4, 8, 16)
POOL_GROUP_WIDTH = 128
POOL_WIDTH = 512
KV_WIDTH = N_KV_HEADS * HEAD_DIM
GATE_WIDTH = 2 * D_MODEL
D_FF = 4 * D_MODEL
FF_CHUNK = 1024
LN_EPS = 1e-5
HALO = 8
META_ROW0 = BLOCK - N_META
BAND_KEYS = 3 * BLOCK
KEYS = BAND_KEYS + N_META
NEG = -1e30
LOG2E = 1.4426950408889634
ONES_ROWS = 16
BLOCK_FLAG_VARIANTS = (7, 6, 3, 4, 2)
TOKEN_TILE = 1024
HALF_TILE = TOKEN_TILE // 2
BLOCKS_PER_TILE = TOKEN_TILE // BLOCK
TAIL_ROWS = 256
ATTN_BLOCKS = 8
SCORE_LOOKAHEAD = 2
CAST_MIN_ROWS = 16
VMEM_LIMIT = 56 * 1024 * 1024

F32 = jnp.float32
BF16 = jnp.bfloat16
NT_DIMS = (((1,), (1,)), ((), ()))


def _layout(group_shapes):
    seqs = []
    n = 0
    for gi, (bsz, s) in enumerate(group_shapes):
        assert s % BLOCK == 0 and s >= BLOCK
        for bi in range(bsz):
            seqs.append((gi, bi, s // BLOCK, n))
            n += 1 + s // BLOCK
    nb = -(-n // BLOCKS_PER_TILE) * BLOCKS_PER_TILE
    meta_i = np.arange(nb, dtype=np.int32)
    flags = np.full(nb, 2, dtype=np.int32)
    t0 = np.zeros(nb + 2, dtype=np.int32)
    ln = np.zeros(nb + 2, dtype=np.int32)
    kind = np.full(nb, -2, dtype=np.int32)
    src = np.zeros((len(group_shapes), nb), dtype=np.int32)
    for gi, bi, nreal, b0 in seqs:
        for j in range(nreal + 1):
            b = b0 + j
            meta_i[b] = b0
            t0[b + 1] = -META_ROW0 + BLOCK * j
            ln[b + 1] = N_META + BLOCK * nreal
            flags[b] = (0 if j == 0 else 2) | (1 if j >= 2 else 0) | (4 if j < nreal else 0)
            kind[b] = -1 if j == 0 else gi
            if j > 0:
                src[gi, b:] = bi * nreal + (j - 1)
    variant = np.asarray([BLOCK_FLAG_VARIANTS.index(int(f)) for f in flags], dtype=np.int32)
    return dict(nb=nb, meta=meta_i, variant=variant, t0=t0, ln=ln, kind=kind, src=src)


def _alibi_bias_table():
    nv = len(BLOCK_FLAG_VARIANTS)
    slopes = 2.0 ** (-8.0 * np.arange(1, N_HEADS + 1) / N_HEADS)
    key = np.arange(BAND_KEYS)[:, None]
    qry = np.arange(BLOCK)[None, :]
    rel = np.abs(key - BLOCK - qry).astype(np.float32)[None, :, None, :]
    scaled = (slopes * LOG2E).astype(np.float32).reshape(N_KV_HEADS, 1, GQA_GROUP, 1)
    band = np.where(rel <= WINDOW, -(scaled * rel), np.float32(NEG))
    flags = np.asarray(BLOCK_FLAG_VARIANTS)[:, None]
    usable = (flags >> (np.arange(BAND_KEYS)[None, :] // BLOCK)) & 1
    band = np.where(usable[:, None, :, None, None] != 0, band[None], np.float32(NEG))
    band = band.reshape(nv, N_KV_HEADS, BAND_KEYS, GQA_GROUP * BLOCK)
    meta = np.zeros((nv, N_KV_HEADS, N_META, GQA_GROUP * BLOCK), np.float32)
    return np.concatenate([band, meta], axis=2).astype(np.float32)


def _const_spec(shape, index_map):
    return pl.BlockSpec(shape, index_map, pipeline_mode=pl.Buffered(1))


def _layer_of(stacked, layer):
    return layer if stacked.shape[0] > 1 else 0


def _layer_spec(stacked, layer):
    rest = stacked.shape[1:]
    index = (_layer_of(stacked, layer),) + (0,) * len(rest)
    return _const_spec((None,) + rest, lambda *_: index)


def _layer_norm(x, g, b):
    mu = jnp.mean(x, axis=-1, keepdims=True)
    xc = x - mu
    var = jnp.mean(xc * xc, axis=-1, keepdims=True)
    return xc * lax.rsqrt(var + LN_EPS) * g + b


def _dot(a, b):
    return jnp.dot(a, b, preferred_element_type=F32)


def _half(hf):
    return slice(HALF_TILE * hf, HALF_TILE * (hf + 1))


def _qvt_kernel(w_ref, o_ref):
    o_ref[...] = w_ref[...].T.astype(BF16)


def _qv_transposed(w_in):
    depth = w_in.shape[0]
    q_blocks = D_MODEL // KV_WIDTH
    v_block0 = (D_MODEL + KV_WIDTH) // KV_WIDTH
    n_blocks = q_blocks + 1
    src_col = lambda c: jnp.where(c < q_blocks, c, c - q_blocks + v_block0)
    return pl.pallas_call(
        _qvt_kernel,
        grid=(depth, n_blocks),
        in_specs=[pl.BlockSpec((None, D_MODEL, KV_WIDTH), lambda l, c: (l, 0, src_col(c)))],
        out_specs=pl.BlockSpec((None, KV_WIDTH, D_MODEL), lambda l, c: (l, c, 0)),
        out_shape=jax.ShapeDtypeStruct((depth, D_MODEL + KV_WIDTH, D_MODEL), BF16),
        name="qv_transpose",
    )(w_in)


def _project(hf, h_ref, w_ref, wt_ref, qt_ref, k_ref, vt_ref, g_ref):
    k0 = D_MODEL
    g0 = D_MODEL + 2 * KV_WIDTH + POOL_WIDTH
    rows = _half(hf)
    x = h_ref[rows, :].astype(BF16)
    gates = _dot(x, w_ref[:, g0:g0 + GATE_WIDTH])
    g_ref[rows, :] = (0.5 * jnp.tanh(0.5 * gates) + 0.5).astype(BF16)
    qt = lax.dot_general(wt_ref[0:D_MODEL, :], x, NT_DIMS, preferred_element_type=F32)
    qt_ref[:, rows] = (qt * (HEAD_DIM ** -0.5 * LOG2E)).astype(BF16)
    vt = lax.dot_general(wt_ref[D_MODEL:D_MODEL + KV_WIDTH, :], x, NT_DIMS, preferred_element_type=F32)
    vt_ref[:, rows] = vt.astype(BF16)
    k_ref[rows, :] = _dot(x, w_ref[:, k0:k0 + KV_WIDTH]).astype(BF16)


def _inproj_kernel(h_ref, *refs):
    _project(0, h_ref, *refs)
    _project(1, h_ref, *refs)


def _inproj_embed_kernel(kind_t, srca_t, srcb_t, *refs):
    del srca_t, srcb_t
    xa_refs, xb_refs = refs[:BLOCKS_PER_TILE], refs[BLOCKS_PER_TILE:2 * BLOCKS_PER_TILE]
    hdr_ref, eg_ref, eb_ref = refs[2 * BLOCKS_PER_TILE:2 * BLOCKS_PER_TILE + 3]
    w_ref, wt_ref, h_ref, qt_ref, k_ref, vt_ref, g_ref = refs[2 * BLOCKS_PER_TILE + 3:]
    for hf in range(2):
        for j in range(hf * BLOCKS_PER_TILE // 2, (hf + 1) * BLOCKS_PER_TILE // 2):
            kind = kind_t[pl.program_id(0) * BLOCKS_PER_TILE + j]
            x = jnp.where(kind == 0, xa_refs[j][...], jnp.where(kind == 1, xb_refs[j][...], 0.0))
            x = jnp.where(kind == -1, hdr_ref[...], x)
            h_ref[BLOCK * j:BLOCK * (j + 1), :] = _layer_norm(x, eg_ref[...], eb_ref[...])
        _project(hf, h_ref, w_ref, wt_ref, qt_ref, k_ref, vt_ref, g_ref)


def _inproj_out(t, row, col):
    specs = [pl.BlockSpec((D_MODEL, TOKEN_TILE), col),
             pl.BlockSpec((TOKEN_TILE, KV_WIDTH), row),
             pl.BlockSpec((KV_WIDTH, TOKEN_TILE), col),
             pl.BlockSpec((TOKEN_TILE, GATE_WIDTH), row)]
    shapes = [jax.ShapeDtypeStruct((D_MODEL, t), BF16),
              jax.ShapeDtypeStruct((t, KV_WIDTH), BF16),
              jax.ShapeDtypeStruct((KV_WIDTH, t), BF16),
              jax.ShapeDtypeStruct((t, GATE_WIDTH), BF16)]
    return specs, shapes


def _inproj_embed(xa, xb, header, eg, eb, w_in, w_qvt, lay):
    t = lay["nb"] * BLOCK
    n = BLOCKS_PER_TILE
    blk = lambda sel: pl.BlockSpec((BLOCK, D_MODEL), sel)
    from_a = [blk(lambda i, k, sa, sb, j=j: (sa[i * n + j], 0)) for j in range(n)]
    from_b = [blk(lambda i, k, sa, sb, j=j: (sb[i * n + j], 0)) for j in range(n)]
    const = lambda i, *_: (0, 0)
    row = lambda i, *_: (i, 0)
    col = lambda i, *_: (0, i)
    out_specs, out_shape = _inproj_out(t, row, col)
    grid_spec = pltpu.PrefetchScalarGridSpec(
        num_scalar_prefetch=3,
        grid=(t // TOKEN_TILE,),
        in_specs=from_a + from_b + [blk(const), pl.BlockSpec((1, D_MODEL), const),
                                    pl.BlockSpec((1, D_MODEL), const),
                                    _layer_spec(w_in, 0), _layer_spec(w_qvt, 0)],
        out_specs=[pl.BlockSpec((TOKEN_TILE, D_MODEL), row)] + out_specs,
    )
    return pl.pallas_call(
        _inproj_embed_kernel,
        grid_spec=grid_spec,
        out_shape=[jax.ShapeDtypeStruct((t, D_MODEL), F32)] + out_shape,
        compiler_params=pltpu.CompilerParams(vmem_limit_bytes=VMEM_LIMIT),
        name="embed_in_proj",
    )(lay["kind"], lay["src"][0], lay["src"][1], *([xa] * n), *([xb] * n), header, eg, eb, w_in, w_qvt)


def _inproj(h, w_in, w_qvt, layer):
    t = h.shape[0]
    row = lambda i: (i, 0)
    col = lambda i: (0, i)
    out_specs, out_shape = _inproj_out(t, row, col)
    return pl.pallas_call(
        _inproj_kernel,
        grid=(t // TOKEN_TILE,),
        in_specs=[pl.BlockSpec((TOKEN_TILE, D_MODEL), row),
                  _layer_spec(w_in, layer), _layer_spec(w_qvt, layer)],
        out_specs=out_specs,
        out_shape=out_shape,
        compiler_params=pltpu.CompilerParams(vmem_limit_bytes=VMEM_LIMIT),
        name="in_proj",
    )(h, w_in, w_qvt)


def _attn_kernel(meta_t, var_t, sink_ref, qt_ref, *refs, n_cast):
    del meta_t
    kp_ref, ko_ref, kn_ref = refs[0:3]
    kh_refs = refs[3:3 + ATTN_BLOCKS]
    vp_ref, vo_ref, vn_ref = refs[3 + ATTN_BLOCKS:6 + ATTN_BLOCKS]
    vh_refs = refs[6 + ATTN_BLOCKS:6 + 2 * ATTN_BLOCKS]
    n_in = 7 + 2 * ATTN_BLOCKS + n_cast
    bias_ref, o_ref = refs[6 + 2 * ATTN_BLOCKS], refs[n_in]
    for src_ref, dst_ref in zip(refs[n_in - n_cast:n_in], refs[n_in + 1:]):
        dst_ref[...] = src_ref[...].astype(BF16)
    step = pl.program_id(0)
    width = GQA_GROUP * BLOCK
    zero_q = jnp.zeros((HEAD_DIM, width), BF16)
    zero_p = jnp.zeros((META_ROW0, width), BF16)
    ones_band = jnp.ones((ONES_ROWS, BAND_KEYS), BF16)
    ones_meta = jnp.ones((ONES_ROWS, BLOCK), BF16)

    def window(blk, before, own, after, axis):
        parts = []
        for j in range(blk - 1, blk + 2):
            sel = [slice(None), slice(None)]
            if j < 0:
                parts.append(before)
            elif j >= ATTN_BLOCKS:
                parts.append(after)
            else:
                sel[axis] = slice(BLOCK * j, BLOCK * (j + 1))
                parts.append(own[tuple(sel)])
        return parts

    def scores(blk, k):
        pair = slice(BLOCK * (k // 2), BLOCK * (k // 2 + 1))
        band = window(blk, kp_ref[:, pair], ko_ref[:, pair], kn_ref[:, pair], 0)
        kc = jnp.concatenate(band + [kh_refs[blk][META_ROW0:BLOCK, pair]], axis=0)
        q0 = GQA_GROUP * HEAD_DIM * k
        qs = slice(BLOCK * blk, BLOCK * (blk + 1))
        qt4 = jnp.concatenate([qt_ref[q0 + HEAD_DIM * j:q0 + HEAD_DIM * (j + 1), qs]
                               for j in range(GQA_GROUP)], axis=1)
        rhs = jnp.concatenate([qt4, zero_q] if k % 2 == 0 else [zero_q, qt4], axis=0)
        return _dot(kc, rhs) + bias_ref[var_t[ATTN_BLOCKS * step + blk], k]

    def finish(blk, k, s):
        sink_row = jnp.concatenate([jnp.full((1, BLOCK), sink_ref[GQA_GROUP * k + j] * LOG2E, F32)
                                    for j in range(GQA_GROUP)], axis=1)
        m = jnp.maximum(jnp.max(s, axis=0, keepdims=True), sink_row)
        pb = jnp.exp2(s - m).astype(BF16)
        vs = slice(HEAD_DIM * k, HEAD_DIM * (k + 1))
        band = window(blk, vp_ref[vs, :], vo_ref[vs, :], vn_ref[vs, :], 1)
        vc = jnp.concatenate([jnp.concatenate(band, axis=1), ones_band], axis=0)
        vh = jnp.concatenate([vh_refs[blk][vs, :], ones_meta], axis=0)
        p_meta = jnp.concatenate([zero_p, pb[BAND_KEYS:KEYS]], axis=0)
        acc = _dot(vc, pb[0:BAND_KEYS]) + _dot(vh, p_meta)
        denom = acc[HEAD_DIM:HEAD_DIM + 1] + jnp.exp2(sink_row - m)
        ot = acc[0:HEAD_DIM] * (1.0 / denom)
        rows = slice(BLOCK * blk, BLOCK * (blk + 1))
        for j in range(GQA_GROUP // 2):
            two = jnp.concatenate([ot[:, 2 * BLOCK * j:2 * BLOCK * j + BLOCK],
                                   ot[:, 2 * BLOCK * j + BLOCK:2 * BLOCK * (j + 1)]], axis=0)
            c0 = GQA_GROUP * HEAD_DIM * k + BLOCK * j
            o_ref[rows, c0:c0 + BLOCK] = two.T.astype(BF16)

    units = [(blk, k) for blk in range(ATTN_BLOCKS) for k in range(N_KV_HEADS)]
    pending = [scores(*u) for u in units[:SCORE_LOOKAHEAD]]
    for idx, unit in enumerate(units):
        if idx + SCORE_LOOKAHEAD < len(units):
            pending.append(scores(*units[idx + SCORE_LOOKAHEAD]))
        finish(*unit, pending.pop(0))


def _attention(qt, kk, vt, sink, bias, lay, casts):
    t = kk.shape[0]
    nb = lay["nb"]
    n = ATTN_BLOCKS
    steps = nb // n
    cast_in, cast_out, cast_shape = [], [], []
    for w, layer in casts:
        rows, cols = w.shape[1:]
        chunks = min(1 << (steps.bit_length() - 1), rows // CAST_MIN_ROWS)
        assert rows % chunks == 0 and (rows // chunks) % CAST_MIN_ROWS == 0
        chunk = lambda i, m, v, c=chunks: jnp.minimum(i, c - 1)
        cast_in.append(pl.BlockSpec((None, rows // chunks, cols),
                                    lambda i, m, v, chunk=chunk, layer=layer: (layer, chunk(i, m, v), 0)))
        cast_out.append(pl.BlockSpec((None, rows // chunks, cols),
                                     lambda i, m, v, chunk=chunk: (0, chunk(i, m, v), 0)))
        cast_shape.append(jax.ShapeDtypeStruct((1, rows, cols), BF16))
    kspec = lambda rows, sel: pl.BlockSpec((rows, KV_WIDTH), sel)
    vspec = lambda cols, sel: pl.BlockSpec((KV_WIDTH, cols), sel)
    before = lambda i: jnp.maximum(n * i - 1, 0)
    after = lambda i: jnp.minimum(n * i + n, nb - 1)
    k_specs = [kspec(BLOCK, lambda i, m, v: (before(i), 0)),
               kspec(n * BLOCK, lambda i, m, v: (i, 0)),
               kspec(BLOCK, lambda i, m, v: (after(i), 0))]
    k_specs += [kspec(BLOCK, lambda i, m, v, j=j: (m[n * i + j], 0)) for j in range(n)]
    v_specs = [vspec(BLOCK, lambda i, m, v: (0, before(i))),
               vspec(n * BLOCK, lambda i, m, v: (0, i)),
               vspec(BLOCK, lambda i, m, v: (0, after(i)))]
    v_specs += [vspec(BLOCK, lambda i, m, v, j=j: (0, m[n * i + j])) for j in range(n)]
    grid_spec = pltpu.PrefetchScalarGridSpec(
        num_scalar_prefetch=2,
        grid=(nb // n,),
        in_specs=[pl.BlockSpec(memory_space=pltpu.SMEM),
                  pl.BlockSpec((D_MODEL, n * BLOCK), lambda i, m, v: (0, i))] + k_specs + v_specs
                 + [_const_spec((len(BLOCK_FLAG_VARIANTS), N_KV_HEADS, KEYS, GQA_GROUP * BLOCK),
                                lambda i, m, v: (0, 0, 0, 0))] + cast_in,
        out_specs=[pl.BlockSpec((n * BLOCK, D_MODEL), lambda i, m, v: (i, 0))] + cast_out,
    )
    outs = pl.pallas_call(
        functools.partial(_attn_kernel, n_cast=len(casts)),
        grid_spec=grid_spec,
        out_shape=[jax.ShapeDtypeStruct((t, D_MODEL), BF16)] + cast_shape,
        compiler_params=pltpu.CompilerParams(vmem_limit_bytes=VMEM_LIMIT,
                                             dimension_semantics=("arbitrary",)),
        name="band_attn",
    )(lay["meta"], lay["variant"], sink, qt, *([kk] * (3 + n)), *([vt] * (3 + n)), bias,
      *[w for w, _ in casts])
    return outs[0], outs[1:]


def _pool_diff(u_ext, t_ext, l_ext):
    return [_pool_group(u_ext, t_ext, l_ext, gi) for gi in range(len(POOL_WINDOWS))]


def _pool_group(u_ext, t_ext, l_ext, gi):
    w = POOL_WINDOWS[gi]
    n = u_ext.shape[0]
    rows = n - 2 * HALO
    valid = (t_ext >= 0) & (t_ext < l_ext)
    t = t_ext[HALO:HALO + rows]
    ln = l_ext[HALO:HALO + rows]
    x = jnp.where(valid, u_ext[:, gi * POOL_GROUP_WIDTH:(gi + 1) * POOL_GROUP_WIDTH], 0.0)
    acc, k = x, 1
    while k < w // 2:
        acc = acc + pltpu.roll(acc, n - k, 0)
        k *= 2
    win = acc + pltpu.roll(acc, w // 2, 0)
    cnt = jnp.minimum(t + w // 2, ln) - jnp.maximum(t - w // 2, 0)
    cnt = jnp.maximum(cnt, 1).astype(F32)
    return win[HALO:HALO + rows] / cnt - x[HALO:HALO + rows]


def _mix_kernel(t0_t, ln_t,
                attn_ref, g_ref, h_ref, hp_ref, hn_ref,
                wu_ref, wa_ref, wp_ref, ps_ref, wb_ref, wo_ref, lg_ref, lb_ref, o_ref, *, alpha):
    b0 = pl.program_id(0) * BLOCKS_PER_TILE

    def rows_of(entry, first, count):
        r = lax.broadcasted_iota(jnp.int32, (count, POOL_GROUP_WIDTH), 0) + first
        return r + t0_t[entry], jnp.zeros((count, POOL_GROUP_WIDTH), jnp.int32) + ln_t[entry]

    parts = [rows_of(b0, BLOCK - HALO, HALO)]
    parts += [rows_of(b0 + 1 + j, 0, BLOCK) for j in range(BLOCKS_PER_TILE)]
    parts += [rows_of(b0 + 1 + BLOCKS_PER_TILE, 0, HALO)]
    t_ext = jnp.concatenate([p[0] for p in parts], axis=0)
    l_ext = jnp.concatenate([p[1] for p in parts], axis=0)

    u_halves = [_dot(h_ref[_half(hf), :].astype(BF16), wu_ref[...]) for hf in range(2)]
    h_halo = jnp.concatenate([hp_ref[...], hn_ref[...]], axis=0).astype(BF16)
    u_halo = _dot(h_halo, wu_ref[...])
    u_ext = jnp.concatenate([u_halo[0:HALO]] + u_halves + [u_halo[HALO:2 * HALO]], axis=0)

    def pool(hf):
        ext = slice(HALF_TILE * hf, HALF_TILE * (hf + 1) + 2 * HALO)
        return _pool_diff(u_ext[ext], t_ext[ext], l_ext[ext])

    def attn_branch(hf):
        return _dot(attn_ref[_half(hf), :], wa_ref[...])

    def pool_branch(diffs):
        y = jnp.concatenate([_dot(d.astype(BF16), wp_ref[gi]) for gi, d in enumerate(diffs)], axis=1)
        return _dot((y * ps_ref[...]).astype(BF16), wb_ref[...])

    def gate(hf, ya, yb):
        g = g_ref[_half(hf), :].astype(F32)
        return (g[:, 0:D_MODEL] * ya + g[:, D_MODEL:2 * D_MODEL] * yb).astype(BF16)

    def norm(rows, mixed):
        o_ref[rows, :] = _layer_norm(alpha * h_ref[rows, :] + mixed, lg_ref[...], lb_ref[...])

    ya_a = attn_branch(0)
    pool_a = pool(0)
    ya_b = attn_branch(1)
    pool_b = pool(1)
    yb_a = pool_branch(pool_a)
    yb_b = pool_branch(pool_b)
    for hf, (ya, yb) in enumerate(((ya_a, yb_a), (ya_b, yb_b))):
        gated = gate(hf, ya, yb)
        for piece in range(HALF_TILE // TAIL_ROWS):
            local = slice(TAIL_ROWS * piece, TAIL_ROWS * (piece + 1))
            norm(slice(HALF_TILE * hf + local.start, HALF_TILE * hf + local.stop),
                 _dot(gated[local], wo_ref[...]))


def _mix(attn, g, h, w_in, weights, layer, lay, alpha):
    t = h.shape[0]
    n_halo = t // HALO
    per = TOKEN_TILE // HALO
    u_block = (D_MODEL + 2 * KV_WIDTH) // POOL_WIDTH
    row = lambda i, a, b: (i, 0)
    grid_spec = pltpu.PrefetchScalarGridSpec(
        num_scalar_prefetch=2,
        grid=(t // TOKEN_TILE,),
        in_specs=[pl.BlockSpec((TOKEN_TILE, D_MODEL), row),
                  pl.BlockSpec((TOKEN_TILE, GATE_WIDTH), row),
                  pl.BlockSpec((TOKEN_TILE, D_MODEL), row),
                  pl.BlockSpec((HALO, D_MODEL), lambda i, a, b: (jnp.maximum(i * per - 1, 0), 0)),
                  pl.BlockSpec((HALO, D_MODEL), lambda i, a, b: (jnp.minimum((i + 1) * per, n_halo - 1), 0)),
                  _const_spec((None, D_MODEL, POOL_WIDTH),
                              lambda i, a, b: (_layer_of(w_in, layer), 0, u_block))]
                 + [_layer_spec(w, layer) for w in weights],
        out_specs=pl.BlockSpec((TOKEN_TILE, D_MODEL), row),
    )
    return pl.pallas_call(
        functools.partial(_mix_kernel, alpha=alpha),
        grid_spec=grid_spec,
        out_shape=jax.ShapeDtypeStruct((t, D_MODEL), F32),
        compiler_params=pltpu.CompilerParams(vmem_limit_bytes=VMEM_LIMIT),
        name="mix_ln",
    )(lay["t0"], lay["ln"], attn, g, h, h, h, w_in, *weights)


def _mlp_body(x_ref, w1_ref, b1_ref, w2_ref, b2_ref, lg_ref, lb_ref, store, alpha):
    def ff(xb, chunks):
        acc = None
        for c in chunks:
            cs = slice(c * FF_CHUNK, (c + 1) * FF_CHUNK)
            a = jnp.maximum(_dot(xb, w1_ref[:, cs]) + b1_ref[:, cs], 0.0)
            part = _dot((a * a).astype(BF16), w2_ref[cs, :])
            acc = part if acc is None else acc + part
        return acc

    def norm(x, acc):
        return _layer_norm(alpha * x + (acc + b2_ref[...]), lg_ref[...], lb_ref[...])

    n_chunks = D_FF // FF_CHUNK
    x = x_ref[...]
    xb = x.astype(BF16)
    acc = ff(xb, tuple(range(n_chunks - 1)))
    for piece in range(TOKEN_TILE // TAIL_ROWS):
        rows = slice(TAIL_ROWS * piece, TAIL_ROWS * (piece + 1))
        store(rows, norm(x[rows], acc[rows] + ff(xb[rows], (n_chunks - 1,))))


def _mlp_kernel(kind_t, dsta_t, dstb_t, *refs, alpha):
    del kind_t, dsta_t, dstb_t
    ins, o_ref = refs[:-1], refs[-1]

    def store(rows, value):
        o_ref[rows, :] = value

    _mlp_body(*ins, store, alpha)


def _mlp_final_kernel(kind_t, dsta_t, dstb_t, *refs, alpha):
    ins, (ya_ref, yb_ref, buf_ref, sem_ref) = refs[:-4], refs[-4:]
    step = pl.program_id(0)
    slot = step % 2

    def store(rows, value):
        buf_ref[slot, rows, :] = value

    _mlp_body(*ins, store, alpha)

    def copies(of_step, of_slot, act):
        for j in range(BLOCKS_PER_TILE):
            b = of_step * BLOCKS_PER_TILE + j
            for gi, (y_ref, dst_t) in enumerate(((ya_ref, dsta_t), (yb_ref, dstb_t))):
                @pl.when(kind_t[b] == gi)
                def _():
                    src = buf_ref.at[of_slot, pl.ds(BLOCK * j, BLOCK), :]
                    dst = y_ref.at[pl.ds(pl.multiple_of(dst_t[b] * BLOCK, BLOCK), BLOCK), :]
                    act(pltpu.make_async_copy(src, dst, sem_ref.at[of_slot, j]))

    copies(step, slot, lambda c: c.start())

    @pl.when(step > 0)
    def _():
        copies(step - 1, 1 - slot, lambda c: c.wait())

    @pl.when(step == pl.num_programs(0) - 1)
    def _():
        copies(step, slot, lambda c: c.wait())


def _mlp(x, weights, layer, lay, alpha, out_rows=None):
    t = x.shape[0]
    row = lambda i, *_: (i, 0)
    in_specs = [pl.BlockSpec((TOKEN_TILE, D_MODEL), row)] + [_layer_spec(w, layer) for w in weights]
    if out_rows is None:
        body, name = _mlp_kernel, "mlp_ln"
        out_specs = pl.BlockSpec((TOKEN_TILE, D_MODEL), row)
        out_shape = jax.ShapeDtypeStruct((t, D_MODEL), F32)
        scratch = []
    else:
        body, name = _mlp_final_kernel, "mlp_ln_out"
        out_specs = [pl.BlockSpec(memory_space=pl.ANY)] * 2
        out_shape = [jax.ShapeDtypeStruct((r, D_MODEL), F32) for r in out_rows]
        scratch = [pltpu.VMEM((2, TOKEN_TILE, D_MODEL), F32),
                   pltpu.SemaphoreType.DMA((2, BLOCKS_PER_TILE))]
    grid_spec = pltpu.PrefetchScalarGridSpec(
        num_scalar_prefetch=3, grid=(t // TOKEN_TILE,),
        in_specs=in_specs, out_specs=out_specs, scratch_shapes=scratch)
    return pl.pallas_call(
        functools.partial(body, alpha=alpha),
        grid_spec=grid_spec,
        out_shape=out_shape,
        compiler_params=pltpu.CompilerParams(vmem_limit_bytes=VMEM_LIMIT,
                                             dimension_semantics=("arbitrary",)),
        name=name,
    )(lay["kind"], lay["src"][0], lay["src"][1], x, *weights)


def kernel(x_prompt, x_sample, meta_tokens, ln_emb_g, ln_emb_b, w_in, sink, w_pool, pool_scale, w_bo_attn,
           w_bo_pool, w_out, ln1_g, ln1_b, w_mlp1, b_mlp1, w_mlp2, b_mlp2, ln2_g, ln2_b):
    depth = w_in.shape[0]
    alpha = float((2 * depth) ** 0.25)
    groups = (x_prompt, x_sample)
    lay = _layout([x.shape[:2] for x in groups])
    header = jnp.concatenate([jnp.zeros((META_ROW0, D_MODEL), F32), meta_tokens.astype(F32)], axis=0)

    row3 = lambda a: a.reshape(a.shape[0], 1, a.shape[-1])
    w_qvt = _qv_transposed(w_in)
    w_pool_rows = w_pool.reshape(depth, -1, POOL_GROUP_WIDTH)
    bias = jnp.asarray(_alibi_bias_table())

    out_rows = [x.shape[0] * x.shape[1] for x in groups]
    w_in_b = w_in[0:1].astype(BF16)
    for l in range(depth):
        if l == 0:
            h, qt, kk, vt, g = _inproj_embed(
                x_prompt.reshape(-1, D_MODEL), x_sample.reshape(-1, D_MODEL), header,
                ln_emb_g.reshape(1, -1), ln_emb_b.reshape(1, -1), w_in_b, w_qvt, lay)
        else:
            qt, kk, vt, g = _inproj(h, w_in_b, w_qvt, l)
        casts = [(w_bo_attn, l), (w_pool_rows, l), (w_bo_pool, l), (w_out, l), (w_mlp1, l), (w_mlp2, l)]
        if l + 1 < depth:
            casts.append((w_in, l + 1))
        attn, cast = _attention(qt, kk, vt, sink[l].astype(F32), bias, lay, casts)
        wa_b, wp_b, wb_b, wo_b, w1_b, w2_b = cast[:6]
        mix_w = (wa_b, wp_b.reshape((1,) + w_pool.shape[1:]), row3(pool_scale), wb_b, wo_b,
                 row3(ln1_g), row3(ln1_b))
        x1 = _mix(attn, g, h, w_in_b, mix_w, l, lay, alpha)
        mlp_w = (w1_b, row3(b_mlp1), w2_b, row3(b_mlp2), row3(ln2_g), row3(ln2_b))
        h = _mlp(x1, mlp_w, l, lay, alpha, out_rows if l == depth - 1 else None)
        if l + 1 < depth:
            w_in_b = cast[6]
    return tuple(y.reshape(x.shape) for y, x in zip(h, groups))
```

```python
import functools

import numpy as np
import jax
import jax.numpy as jnp
from jax import lax
from jax.experimental import pallas as pl
from jax.experimental.pallas import tpu as pltpu

D_MODEL = 1024
N_META = 16
N_HEADS = 16
N_KV_HEADS = 4
HEAD_DIM = 64
GQA_GROUP = N_HEADS // N_KV_HEADS
WINDOW = 128
BLOCK = 128
POOL_WINDOWS = (2, 4, 8, 16)
POOL_GROUP_WIDTH = 128
POOL_WIDTH = 512
KV_WIDTH = N_KV_HEADS * HEAD_DIM
GATE_WIDTH = 2 * D_MODEL
D_FF = 4 * D_MODEL
FF_CHUNK = 1024
LN_EPS = 1e-5
HALO = 8
META_ROW0 = BLOCK - N_META
BAND_KEYS = 3 * BLOCK
KEYS = BAND_KEYS + N_META
NEG = -1e30
LOG2E = 1.4426950408889634
ONES_ROWS = 16
BLOCK_FLAG_VARIANTS = (7, 6, 3, 4, 2)
TOKEN_TILE = 1024
HALF_TILE = TOKEN_TILE // 2
BLOCKS_PER_TILE = TOKEN_TILE // BLOCK
TAIL_ROWS = 256
ATTN_BLOCKS = 8
SCORE_LOOKAHEAD = 2
CAST_MIN_ROWS = 16
VMEM_LIMIT = 56 * 1024 * 1024

F32 = jnp.float32
BF16 = jnp.bfloat16
NT_DIMS = (((1,), (1,)), ((), ()))


def _layout(group_shapes):
    seqs = []
    n = 0
    for gi, (bsz, s) in enumerate(group_shapes):
        assert s % BLOCK == 0 and s >= BLOCK
        for bi in range(bsz):
            seqs.append((gi, bi, s // BLOCK, n))
            n += 1 + s // BLOCK
    nb = -(-n // BLOCKS_PER_TILE) * BLOCKS_PER_TILE
    meta_i = np.arange(nb, dtype=np.int32)
    flags = np.full(nb, 2, dtype=np.int32)
    t0 = np.zeros(nb + 2, dtype=np.int32)
    ln = np.zeros(nb + 2, dtype=np.int32)
    kind = np.full(nb, -2, dtype=np.int32)
    src = np.zeros((len(group_shapes), nb), dtype=np.int32)
    for gi, bi, nreal, b0 in seqs:
        for j in range(nreal + 1):
            b = b0 + j
            meta_i[b] = b0
            t0[b + 1] = -META_ROW0 + BLOCK * j
            ln[b + 1] = N_META + BLOCK * nreal
            flags[b] = (0 if j == 0 else 2) | (1 if j >= 2 else 0) | (4 if j < nreal else 0)
            kind[b] = -1 if j == 0 else gi
            if j > 0:
                src[gi, b:] = bi * nreal + (j - 1)
    variant = np.asarray([BLOCK_FLAG_VARIANTS.index(int(f)) for f in flags], dtype=np.int32)
    return dict(nb=nb, meta=meta_i, variant=variant, t0=t0, ln=ln, kind=kind, src=src)


def _alibi_bias_table():
    nv = len(BLOCK_FLAG_VARIANTS)
    slopes = 2.0 ** (-8.0 * np.arange(1, N_HEADS + 1) / N_HEADS)
    key = np.arange(BAND_KEYS)[:, None]
    qry = np.arange(BLOCK)[None, :]
    rel = np.abs(key - BLOCK - qry).astype(np.float32)[None, :, None, :]
    scaled = (slopes * LOG2E).astype(np.float32).reshape(N_KV_HEADS, 1, GQA_GROUP, 1)
    band = np.where(rel <= WINDOW, -(scaled * rel), np.float32(NEG))
    flags = np.asarray(BLOCK_FLAG_VARIANTS)[:, None]
    usable = (flags >> (np.arange(BAND_KEYS)[None, :] // BLOCK)) & 1
    band = np.where(usable[:, None, :, None, None] != 0, band[None], np.float32(NEG))
    band = band.reshape(nv, N_KV_HEADS, BAND_KEYS, GQA_GROUP * BLOCK)
    meta = np.zeros((nv, N_KV_HEADS, N_META, GQA_GROUP * BLOCK), np.float32)
    return np.concatenate([band, meta], axis=2).astype(np.float32)


def _const_spec(shape, index_map):
    return pl.BlockSpec(shape, index_map, pipeline_mode=pl.Buffered(1))


def _layer_of(stacked, layer):
    return layer if stacked.shape[0] > 1 else 0


def _layer_spec(stacked, layer):
    rest = stacked.shape[1:]
    index = (_layer_of(stacked, layer),) + (0,) * len(rest)
    return _const_spec((None,) + rest, lambda *_: index)


def _layer_norm(x, g, b):
    mu = jnp.mean(x, axis=-1, keepdims=True)
    xc = x - mu
    var = jnp.mean(xc * xc, axis=-1, keepdims=True)
    return xc * lax.rsqrt(var + LN_EPS) * g + b


def _dot(a, b):
    return jnp.dot(a, b, preferred_element_type=F32)


def _half(hf):
    return slice(HALF_TILE * hf, HALF_TILE * (hf + 1))


def _qvt_kernel(w_ref, o_ref):
    o_ref[...] = w_ref[...].T.astype(BF16)


def _qv_transposed(w_in):
    depth = w_in.shape[0]
    q_blocks = D_MODEL // KV_WIDTH
    v_block0 = (D_MODEL + KV_WIDTH) // KV_WIDTH
    n_blocks = q_blocks + 1
    src_col = lambda c: jnp.where(c < q_blocks, c, c - q_blocks + v_block0)
    return pl.pallas_call(
        _qvt_kernel,
        grid=(depth, n_blocks),
        in_specs=[pl.BlockSpec((None, D_MODEL, KV_WIDTH), lambda l, c: (l, 0, src_col(c)))],
        out_specs=pl.BlockSpec((None, KV_WIDTH, D_MODEL), lambda l, c: (l, c, 0)),
        out_shape=jax.ShapeDtypeStruct((depth, D_MODEL + KV_WIDTH, D_MODEL), BF16),
        name="qv_transpose",
    )(w_in)


def _project(hf, h_ref, w_ref, wt_ref, qt_ref, k_ref, vt_ref, g_ref):
    k0 = D_MODEL
    g0 = D_MODEL + 2 * KV_WIDTH + POOL_WIDTH
    rows = _half(hf)
    x = h_ref[rows, :].astype(BF16)
    gates = _dot(x, w_ref[:, g0:g0 + GATE_WIDTH])
    g_ref[rows, :] = (0.5 * jnp.tanh(0.5 * gates) + 0.5).astype(BF16)
    qt = lax.dot_general(wt_ref[0:D_MODEL, :], x, NT_DIMS, preferred_element_type=F32)
    qt_ref[:, rows] = (qt * (HEAD_DIM ** -0.5 * LOG2E)).astype(BF16)
    vt = lax.dot_general(wt_ref[D_MODEL:D_MODEL + KV_WIDTH, :], x, NT_DIMS, preferred_element_type=F32)
    vt_ref[:, rows] = vt.astype(BF16)
    k_ref[rows, :] = _dot(x, w_ref[:, k0:k0 + KV_WIDTH]).astype(BF16)


def _inproj_kernel(h_ref, *refs):
    _project(0, h_ref, *refs)
    _project(1, h_ref, *refs)


def _inproj_embed_kernel(kind_t, srca_t, srcb_t, *refs):
    del srca_t, srcb_t
    xa_refs, xb_refs = refs[:BLOCKS_PER_TILE], refs[BLOCKS_PER_TILE:2 * BLOCKS_PER_TILE]
    hdr_ref, eg_ref, eb_ref = refs[2 * BLOCKS_PER_TILE:2 * BLOCKS_PER_TILE + 3]
    w_ref, wt_ref, h_ref, qt_ref, k_ref, vt_ref, g_ref = refs[2 * BLOCKS_PER_TILE + 3:]
    for hf in range(2):
        for j in range(hf * BLOCKS_PER_TILE // 2, (hf + 1) * BLOCKS_PER_TILE // 2):
            kind = kind_t[pl.program_id(0) * BLOCKS_PER_TILE + j]
            x = jnp.where(kind == 0, xa_refs[j][...], jnp.where(kind == 1, xb_refs[j][...], 0.0))
            x = jnp.where(kind == -1, hdr_ref[...], x)
            h_ref[BLOCK * j:BLOCK * (j + 1), :] = _layer_norm(x, eg_ref[...], eb_ref[...])
        _project(hf, h_ref, w_ref, wt_ref, qt_ref, k_ref, vt_ref, g_ref)


def _inproj_out(t, row, col):
    specs = [pl.BlockSpec((D_MODEL, TOKEN_TILE), col),
             pl.BlockSpec((TOKEN_TILE, KV_WIDTH), row),
             pl.BlockSpec((KV_WIDTH, TOKEN_TILE), col),
             pl.BlockSpec((TOKEN_TILE, GATE_WIDTH), row)]
    shapes = [jax.ShapeDtypeStruct((D_MODEL, t), BF16),
              jax.ShapeDtypeStruct((t, KV_WIDTH), BF16),
              jax.ShapeDtypeStruct((KV_WIDTH, t), BF16),
              jax.ShapeDtypeStruct((t, GATE_WIDTH), BF16)]
    return specs, shapes


def _inproj_embed(xa, xb, header, eg, eb, w_in, w_qvt, lay):
    t = lay["nb"] * BLOCK
    n = BLOCKS_PER_TILE
    blk = lambda sel: pl.BlockSpec((BLOCK, D_MODEL), sel)
    from_a = [blk(lambda i, k, sa, sb, j=j: (sa[i * n + j], 0)) for j in range(n)]
    from_b = [blk(lambda i, k, sa, sb, j=j: (sb[i * n + j], 0)) for j in range(n)]
    const = lambda i, *_: (0, 0)
    row = lambda i, *_: (i, 0)
    col = lambda i, *_: (0, i)
    out_specs, out_shape = _inproj_out(t, row, col)
    grid_spec = pltpu.PrefetchScalarGridSpec(
        num_scalar_prefetch=3,
        grid=(t // TOKEN_TILE,),
        in_specs=from_a + from_b + [blk(const), pl.BlockSpec((1, D_MODEL), const),
                                    pl.BlockSpec((1, D_MODEL), const),
                                    _layer_spec(w_in, 0), _layer_spec(w_qvt, 0)],
        out_specs=[pl.BlockSpec((TOKEN_TILE, D_MODEL), row)] + out_specs,
    )
    return pl.pallas_call(
        _inproj_embed_kernel,
        grid_spec=grid_spec,
        out_shape=[jax.ShapeDtypeStruct((t, D_MODEL), F32)] + out_shape,
        compiler_params=pltpu.CompilerParams(vmem_limit_bytes=VMEM_LIMIT),
        name="embed_in_proj",
    )(lay["kind"], lay["src"][0], lay["src"][1], *([xa] * n), *([xb] * n), header, eg, eb, w_in, w_qvt)


def _inproj(h, w_in, w_qvt, layer):
    t = h.shape[0]
    row = lambda i: (i, 0)
    col = lambda i: (0, i)
    out_specs, out_shape = _inproj_out(t, row, col)
    return pl.pallas_call(
        _inproj_kernel,
        grid=(t // TOKEN_TILE,),
        in_specs=[pl.BlockSpec((TOKEN_TILE, D_MODEL), row),
                  _layer_spec(w_in, layer), _layer_spec(w_qvt, layer)],
        out_specs=out_specs,
        out_shape=out_shape,
        compiler_params=pltpu.CompilerParams(vmem_limit_bytes=VMEM_LIMIT),
        name="in_proj",
    )(h, w_in, w_qvt)


def _attn_kernel(meta_t, var_t, sink_ref, qt_ref, *refs, n_cast):
    del meta_t
    kp_ref, ko_ref, kn_ref = refs[0:3]
    kh_refs = refs[3:3 + ATTN_BLOCKS]
    vp_ref, vo_ref, vn_ref = refs[3 + ATTN_BLOCKS:6 + ATTN_BLOCKS]
    vh_refs = refs[6 + ATTN_BLOCKS:6 + 2 * ATTN_BLOCKS]
    n_in = 7 + 2 * ATTN_BLOCKS + n_cast
    bias_ref, o_ref = refs[6 + 2 * ATTN_BLOCKS], refs[n_in]
    for src_ref, dst_ref in zip(refs[n_in - n_cast:n_in], refs[n_in + 1:]):
        dst_ref[...] = src_ref[...].astype(BF16)
    step = pl.program_id(0)
    ones_band = jnp.ones((ONES_ROWS, BAND_KEYS), BF16)
    ones_meta = jnp.ones((ONES_ROWS, BLOCK), BF16)
    lane = lax.broadcasted_iota(jnp.int32, (1, BLOCK), 1)
    head_lanes = (lane < HEAD_DIM, lane >= HEAD_DIM)
    meta_lanes = lane >= META_ROW0

    def window(blk, before, own, after, axis):
        parts = []
        for j in range(blk - 1, blk + 2):
            sel = [slice(None), slice(None)]
            if j < 0:
                parts.append(before)
            elif j >= ATTN_BLOCKS:
                parts.append(after)
            else:
                sel[axis] = slice(BLOCK * j, BLOCK * (j + 1))
                parts.append(own[tuple(sel)])
        return parts

    def scores(blk, k):
        pair = slice(BLOCK * (k // 2), BLOCK * (k // 2 + 1))
        band = window(blk, kp_ref[:, pair], ko_ref[:, pair], kn_ref[:, pair], 0)
        kc = jnp.concatenate(band + [kh_refs[blk][META_ROW0:BLOCK, pair]], axis=0)
        kc = jnp.where(head_lanes[k % 2], kc, jnp.zeros_like(kc))
        q0 = GQA_GROUP * HEAD_DIM * k - HEAD_DIM * (k % 2)
        qs = slice(BLOCK * blk, BLOCK * (blk + 1))
        rhs = jnp.concatenate([qt_ref[q0 + HEAD_DIM * j:q0 + HEAD_DIM * (j + 2), qs]
                               for j in range(GQA_GROUP)], axis=1)
        return _dot(kc, rhs) + bias_ref[var_t[ATTN_BLOCKS * step + blk], k]

    def finish(blk, k, s):
        sink_row = jnp.concatenate([jnp.full((1, BLOCK), sink_ref[GQA_GROUP * k + j] * LOG2E, F32)
                                    for j in range(GQA_GROUP)], axis=1)
        m = jnp.maximum(jnp.max(s, axis=0, keepdims=True), sink_row)
        pb = jnp.exp2(s - m).astype(BF16)
        vs = slice(HEAD_DIM * k, HEAD_DIM * (k + 1))
        band = window(blk, vp_ref[vs, :], vo_ref[vs, :], vn_ref[vs, :], 1)
        vc = jnp.concatenate([jnp.concatenate(band, axis=1), ones_band], axis=0)
        vh = jnp.concatenate([vh_refs[blk][vs, :], ones_meta], axis=0)
        vh = jnp.where(meta_lanes, vh, jnp.zeros_like(vh))
        acc = _dot(vc, pb[0:BAND_KEYS]) + _dot(vh, pb[KEYS - BLOCK:KEYS])
        denom = acc[HEAD_DIM:HEAD_DIM + 1] + jnp.exp2(sink_row - m)
        ot = acc[0:HEAD_DIM] * (1.0 / denom)
        rows = slice(BLOCK * blk, BLOCK * (blk + 1))
        for j in range(GQA_GROUP // 2):
            two = jnp.concatenate([ot[:, 2 * BLOCK * j:2 * BLOCK * j + BLOCK],
                                   ot[:, 2 * BLOCK * j + BLOCK:2 * BLOCK * (j + 1)]], axis=0)
            c0 = GQA_GROUP * HEAD_DIM * k + BLOCK * j
            o_ref[rows, c0:c0 + BLOCK] = two.T.astype(BF16)

    units = [(blk, k) for blk in range(ATTN_BLOCKS) for k in range(N_KV_HEADS)]
    pending = [scores(*u) for u in units[:SCORE_LOOKAHEAD]]
    for idx, unit in enumerate(units):
        if idx + SCORE_LOOKAHEAD < len(units):
            pending.append(scores(*units[idx + SCORE_LOOKAHEAD]))
        finish(*unit, pending.pop(0))


def _attention(qt, kk, vt, sink, bias, lay, casts):
    t = kk.shape[0]
    nb = lay["nb"]
    n = ATTN_BLOCKS
    steps = nb // n
    cast_in, cast_out, cast_shape = [], [], []
    for w, layer in casts:
        rows, cols = w.shape[1:]
        chunks = min(1 << (steps.bit_length() - 1), rows // CAST_MIN_ROWS)
        assert rows % chunks == 0 and (rows // chunks) % CAST_MIN_ROWS == 0
        chunk = lambda i, m, v, c=chunks: jnp.minimum(i, c - 1)
        cast_in.append(pl.BlockSpec((None, rows // chunks, cols),
                                    lambda i, m, v, chunk=chunk, layer=layer: (layer, chunk(i, m, v), 0)))
        cast_out.append(pl.BlockSpec((None, rows // chunks, cols),
                                     lambda i, m, v, chunk=chunk: (0, chunk(i, m, v), 0)))
        cast_shape.append(jax.ShapeDtypeStruct((1, rows, cols), BF16))
    kspec = lambda rows, sel: pl.BlockSpec((rows, KV_WIDTH), sel)
    vspec = lambda cols, sel: pl.BlockSpec((KV_WIDTH, cols), sel)
    before = lambda i: jnp.maximum(n * i - 1, 0)
    after = lambda i: jnp.minimum(n * i + n, nb - 1)
    k_specs = [kspec(BLOCK, lambda i, m, v: (before(i), 0)),
               kspec(n * BLOCK, lambda i, m, v: (i, 0)),
               kspec(BLOCK, lambda i, m, v: (after(i), 0))]
    k_specs += [kspec(BLOCK, lambda i, m, v, j=j: (m[n * i + j], 0)) for j in range(n)]
    v_specs = [vspec(BLOCK, lambda i, m, v: (0, before(i))),
               vspec(n * BLOCK, lambda i, m, v: (0, i)),
               vspec(BLOCK, lambda i, m, v: (0, after(i)))]
    v_specs += [vspec(BLOCK, lambda i, m, v, j=j: (0, m[n * i + j])) for j in range(n)]
    grid_spec = pltpu.PrefetchScalarGridSpec(
        num_scalar_prefetch=2,
        grid=(nb // n,),
        in_specs=[pl.BlockSpec(memory_space=pltpu.SMEM),
                  pl.BlockSpec((D_MODEL, n * BLOCK), lambda i, m, v: (0, i))] + k_specs + v_specs
                 + [_const_spec((len(BLOCK_FLAG_VARIANTS), N_KV_HEADS, KEYS, GQA_GROUP * BLOCK),
                                lambda i, m, v: (0, 0, 0, 0))] + cast_in,
        out_specs=[pl.BlockSpec((n * BLOCK, D_MODEL), lambda i, m, v: (i, 0))] + cast_out,
    )
    outs = pl.pallas_call(
        functools.partial(_attn_kernel, n_cast=len(casts)),
        grid_spec=grid_spec,
        out_shape=[jax.ShapeDtypeStruct((t, D_MODEL), BF16)] + cast_shape,
        compiler_params=pltpu.CompilerParams(vmem_limit_bytes=VMEM_LIMIT,
                                             dimension_semantics=("arbitrary",)),
        name="band_attn",
    )(lay["meta"], lay["variant"], sink, qt, *([kk] * (3 + n)), *([vt] * (3 + n)), bias,
      *[w for w, _ in casts])
    return outs[0], outs[1:]


def _pool_diff(u_ext, t_ext, l_ext):
    return [_pool_group(u_ext, t_ext, l_ext, gi) for gi in range(len(POOL_WINDOWS))]


def _pool_group(u_ext, t_ext, l_ext, gi):
    w = POOL_WINDOWS[gi]
    n = u_ext.shape[0]
    rows = n - 2 * HALO
    valid = (t_ext >= 0) & (t_ext < l_ext)
    t = t_ext[HALO:HALO + rows]
    ln = l_ext[HALO:HALO + rows]
    x = jnp.where(valid, u_ext[:, gi * POOL_GROUP_WIDTH:(gi + 1) * POOL_GROUP_WIDTH], 0.0)
    acc, k = x, 1
    while k < w // 2:
        acc = acc + pltpu.roll(acc, n - k, 0)
        k *= 2
    win = acc + pltpu.roll(acc, w // 2, 0)
    cnt = jnp.minimum(t + w // 2, ln) - jnp.maximum(t - w // 2, 0)
    cnt = jnp.maximum(cnt, 1).astype(F32)
    return win[HALO:HALO + rows] / cnt - x[HALO:HALO + rows]


def _mix_kernel(t0_t, ln_t,
                attn_ref, g_ref, h_ref, hp_ref, hn_ref,
                wu_ref, wa_ref, wp_ref, ps_ref, wb_ref, wo_ref, lg_ref, lb_ref, o_ref, *, alpha):
    b0 = pl.program_id(0) * BLOCKS_PER_TILE

    def rows_of(entry, first, count):
        r = lax.broadcasted_iota(jnp.int32, (count, POOL_GROUP_WIDTH), 0) + first
        return r + t0_t[entry], jnp.zeros((count, POOL_GROUP_WIDTH), jnp.int32) + ln_t[entry]

    parts = [rows_of(b0, BLOCK - HALO, HALO)]
    parts += [rows_of(b0 + 1 + j, 0, BLOCK) for j in range(BLOCKS_PER_TILE)]
    parts += [rows_of(b0 + 1 + BLOCKS_PER_TILE, 0, HALO)]
    t_ext = jnp.concatenate([p[0] for p in parts], axis=0)
    l_ext = jnp.concatenate([p[1] for p in parts], axis=0)

    u_halves = [_dot(h_ref[_half(hf), :].astype(BF16), wu_ref[...]) for hf in range(2)]
    h_halo = jnp.concatenate([hp_ref[...], hn_ref[...]], axis=0).astype(BF16)
    u_halo = _dot(h_halo, wu_ref[...])
    u_ext = jnp.concatenate([u_halo[0:HALO]] + u_halves + [u_halo[HALO:2 * HALO]], axis=0)

    def pool(hf):
        ext = slice(HALF_TILE * hf, HALF_TILE * (hf + 1) + 2 * HALO)
        return _pool_diff(u_ext[ext], t_ext[ext], l_ext[ext])

    def attn_branch(hf):
        return _dot(attn_ref[_half(hf), :], wa_ref[...])

    def pool_branch(diffs):
        y = jnp.concatenate([_dot(d.astype(BF16), wp_ref[gi]) for gi, d in enumerate(diffs)], axis=1)
        return _dot((y * ps_ref[...]).astype(BF16), wb_ref[...])

    def gate(hf, ya, yb):
        g = g_ref[_half(hf), :].astype(F32)
        return (g[:, 0:D_MODEL] * ya + g[:, D_MODEL:2 * D_MODEL] * yb).astype(BF16)

    def norm(rows, mixed):
        o_ref[rows, :] = _layer_norm(alpha * h_ref[rows, :] + mixed, lg_ref[...], lb_ref[...])

    ya_a = attn_branch(0)
    pool_a = pool(0)
    ya_b = attn_branch(1)
    pool_b = pool(1)
    yb_a = pool_branch(pool_a)
    yb_b = pool_branch(pool_b)
    for hf, (ya, yb) in enumerate(((ya_a, yb_a), (ya_b, yb_b))):
        gated = gate(hf, ya, yb)
        for piece in range(HALF_TILE // TAIL_ROWS):
            local = slice(TAIL_ROWS * piece, TAIL_ROWS * (piece + 1))
            norm(slice(HALF_TILE * hf + local.start, HALF_TILE * hf + local.stop),
                 _dot(gated[local], wo_ref[...]))


def _mix(attn, g, h, w_in, weights, layer, lay, alpha):
    t = h.shape[0]
    n_halo = t // HALO
    per = TOKEN_TILE // HALO
    u_block = (D_MODEL + 2 * KV_WIDTH) // POOL_WIDTH
    row = lambda i, a, b: (i, 0)
    grid_spec = pltpu.PrefetchScalarGridSpec(
        num_scalar_prefetch=2,
        grid=(t // TOKEN_TILE,),
        in_specs=[pl.BlockSpec((TOKEN_TILE, D_MODEL), row),
                  pl.BlockSpec((TOKEN_TILE, GATE_WIDTH), row),
                  pl.BlockSpec((TOKEN_TILE, D_MODEL), row),
                  pl.BlockSpec((HALO, D_MODEL), lambda i, a, b: (jnp.maximum(i * per - 1, 0), 0)),
                  pl.BlockSpec((HALO, D_MODEL), lambda i, a, b: (jnp.minimum((i + 1) * per, n_halo - 1), 0)),
                  _const_spec((None, D_MODEL, POOL_WIDTH),
                              lambda i, a, b: (_layer_of(w_in, layer), 0, u_block))]
                 + [_layer_spec(w, layer) for w in weights],
        out_specs=pl.BlockSpec((TOKEN_TILE, D_MODEL), row),
    )
    return pl.pallas_call(
        functools.partial(_mix_kernel, alpha=alpha),
        grid_spec=grid_spec,
        out_shape=jax.ShapeDtypeStruct((t, D_MODEL), F32),
        compiler_params=pltpu.CompilerParams(vmem_limit_bytes=VMEM_LIMIT),
        name="mix_ln",
    )(lay["t0"], lay["ln"], attn, g, h, h, h, w_in, *weights)


def _mlp_body(x_ref, w1_ref, b1_ref, w2_ref, b2_ref, lg_ref, lb_ref, store, alpha):
    def ff(xb, chunks):
        acc = None
        for c in chunks:
            cs = slice(c * FF_CHUNK, (c + 1) * FF_CHUNK)
            a = jnp.maximum(_dot(xb, w1_ref[:, cs]) + b1_ref[:, cs], 0.0)
            part = _dot((a * a).astype(BF16), w2_ref[cs, :])
            acc = part if acc is None else acc + part
        return acc

    def norm(x, acc):
        return _layer_norm(alpha * x + (acc + b2_ref[...]), lg_ref[...], lb_ref[...])

    n_chunks = D_FF // FF_CHUNK
    x = x_ref[...]
    xb = x.astype(BF16)
    acc = ff(xb, tuple(range(n_chunks - 1)))
    for piece in range(TOKEN_TILE // TAIL_ROWS):
        rows = slice(TAIL_ROWS * piece, TAIL_ROWS * (piece + 1))
        store(rows, norm(x[rows], acc[rows] + ff(xb[rows], (n_chunks - 1,))))


def _mlp_kernel(kind_t, dsta_t, dstb_t, *refs, alpha):
    del kind_t, dsta_t, dstb_t
    ins, o_ref = refs[:-1], refs[-1]

    def store(rows, value):
        o_ref[rows, :] = value

    _mlp_body(*ins, store, alpha)


def _mlp_final_kernel(kind_t, dsta_t, dstb_t, *refs, alpha):
    ins, (ya_ref, yb_ref, buf_ref, sem_ref) = refs[:-4], refs[-4:]
    step = pl.program_id(0)
    slot = step % 2

    def store(rows, value):
        buf_ref[slot, rows, :] = value

    _mlp_body(*ins, store, alpha)

    def copies(of_step, of_slot, act):
        for j in range(BLOCKS_PER_TILE):
            b = of_step * BLOCKS_PER_TILE + j
            for gi, (y_ref, dst_t) in enumerate(((ya_ref, dsta_t), (yb_ref, dstb_t))):
                @pl.when(kind_t[b] == gi)
                def _():
                    src = buf_ref.at[of_slot, pl.ds(BLOCK * j, BLOCK), :]
                    dst = y_ref.at[pl.ds(pl.multiple_of(dst_t[b] * BLOCK, BLOCK), BLOCK), :]
                    act(pltpu.make_async_copy(src, dst, sem_ref.at[of_slot, j]))

    copies(step, slot, lambda c: c.start())

    @pl.when(step > 0)
    def _():
        copies(step - 1, 1 - slot, lambda c: c.wait())

    @pl.when(step == pl.num_programs(0) - 1)
    def _():
        copies(step, slot, lambda c: c.wait())


def _mlp(x, weights, layer, lay, alpha, out_rows=None):
    t = x.shape[0]
    row = lambda i, *_: (i, 0)
    in_specs = [pl.BlockSpec((TOKEN_TILE, D_MODEL), row)] + [_layer_spec(w, layer) for w in weights]
    if out_rows is None:
        body, name = _mlp_kernel, "mlp_ln"
        out_specs = pl.BlockSpec((TOKEN_TILE, D_MODEL), row)
        out_shape = jax.ShapeDtypeStruct((t, D_MODEL), F32)
        scratch = []
    else:
        body, name = _mlp_final_kernel, "mlp_ln_out"
        out_specs = [pl.BlockSpec(memory_space=pl.ANY)] * 2
        out_shape = [jax.ShapeDtypeStruct((r, D_MODEL), F32) for r in out_rows]
        scratch = [pltpu.VMEM((2, TOKEN_TILE, D_MODEL), F32),
                   pltpu.SemaphoreType.DMA((2, BLOCKS_PER_TILE))]
    grid_spec = pltpu.PrefetchScalarGridSpec(
        num_scalar_prefetch=3, grid=(t // TOKEN_TILE,),
        in_specs=in_specs, out_specs=out_specs, scratch_shapes=scratch)
    return pl.pallas_call(
        functools.partial(body, alpha=alpha),
        grid_spec=grid_spec,
        out_shape=out_shape,
        compiler_params=pltpu.CompilerParams(vmem_limit_bytes=VMEM_LIMIT,
                                             dimension_semantics=("arbitrary",)),
        name=name,
    )(lay["kind"], lay["src"][0], lay["src"][1], x, *weights)


def kernel(x_prompt, x_sample, meta_tokens, ln_emb_g, ln_emb_b, w_in, sink, w_pool, pool_scale, w_bo_attn,
           w_bo_pool, w_out, ln1_g, ln1_b, w_mlp1, b_mlp1, w_mlp2, b_mlp2, ln2_g, ln2_b):
    depth = w_in.shape[0]
    alpha = float((2 * depth) ** 0.25)
    groups = (x_prompt, x_sample)
    lay = _layout([x.shape[:2] for x in groups])
    header = jnp.concatenate([jnp.zeros((META_ROW0, D_MODEL), F32), meta_tokens.astype(F32)], axis=0)

    row3 = lambda a: a.reshape(a.shape[0], 1, a.shape[-1])
    w_qvt = _qv_transposed(w_in)
    w_pool_rows = w_pool.reshape(depth, -1, POOL_GROUP_WIDTH)
    bias = jnp.asarray(_alibi_bias_table())

    out_rows = [x.shape[0] * x.shape[1] for x in groups]
    w_in_b = w_in[0:1].astype(BF16)
    for l in range(depth):
        if l == 0:
            h, qt, kk, vt, g = _inproj_embed(
                x_prompt.reshape(-1, D_MODEL), x_sample.reshape(-1, D_MODEL), header,
                ln_emb_g.reshape(1, -1), ln_emb_b.reshape(1, -1), w_in_b, w_qvt, lay)
        else:
            qt, kk, vt, g = _inproj(h, w_in_b, w_qvt, l)
        casts = [(w_bo_attn, l), (w_pool_rows, l), (w_bo_pool, l), (w_out, l), (w_mlp1, l), (w_mlp2, l)]
        if l + 1 < depth:
            casts.append((w_in, l + 1))
        attn, cast = _attention(qt, kk, vt, sink[l].astype(F32), bias, lay, casts)
        wa_b, wp_b, wb_b, wo_b, w1_b, w2_b = cast[:6]
        mix_w = (wa_b, wp_b.reshape((1,) + w_pool.shape[1:]), row3(pool_scale), wb_b, wo_b,
                 row3(ln1_g), row3(ln1_b))
        x1 = _mix(attn, g, h, w_in_b, mix_w, l, lay, alpha)
        mlp_w = (w1_b, row3(b_mlp1), w2_b, row3(b_mlp2), row3(ln2_g), row3(ln2_b))
        h = _mlp(x1, mlp_w, l, lay, alpha, out_rows if l == depth - 1 else None)
        if l + 1 < depth:
            w_in_b = cast[6]
    return tuple(y.reshape(x.shape) for y, x in zip(h, groups))
```

```python
import functools

import numpy as np
import jax
import jax.numpy as jnp
from jax import lax
from jax.experimental import pallas as pl
from jax.experimental.pallas import tpu as pltpu

D_MODEL = 1024
N_META = 16
N_HEADS = 16
N_KV_HEADS = 4
HEAD_DIM = 64
GQA_GROUP = N_HEADS // N_KV_HEADS
WINDOW = 128
BLOCK = 128
POOL_WINDOWS = (2, 4, 8, 16)
POOL_GROUP_WIDTH = 128
POOL_WIDTH = 512
KV_WIDTH = N_KV_HEADS * HEAD_DIM
GATE_WIDTH = 2 * D_MODEL
D_FF = 4 * D_MODEL
FF_CHUNK = 1024
LN_EPS = 1e-5
HALO = 8
META_ROW0 = BLOCK - N_META
BAND_KEYS = 3 * BLOCK
KEYS = BAND_KEYS + N_META
NEG = -1e30
LOG2E = 1.4426950408889634
ONES_ROWS = 16
BLOCK_FLAG_VARIANTS = (7, 6, 3, 4, 2)
TOKEN_TILE = 1024
HALF_TILE = TOKEN_TILE // 2
BLOCKS_PER_TILE = TOKEN_TILE // BLOCK
TAIL_ROWS = 256
ATTN_BLOCKS = 4
SCORE_LOOKAHEAD = 2
CAST_MIN_ROWS = 16
VMEM_LIMIT = 56 * 1024 * 1024

F32 = jnp.float32
BF16 = jnp.bfloat16
NT_DIMS = (((1,), (1,)), ((), ()))


def _layout(group_shapes):
    seqs = []
    n = 0
    for gi, (bsz, s) in enumerate(group_shapes):
        assert s % BLOCK == 0 and s >= BLOCK
        for bi in range(bsz):
            seqs.append((gi, bi, s // BLOCK, n))
            n += 1 + s // BLOCK
    nb = -(-n // BLOCKS_PER_TILE) * BLOCKS_PER_TILE
    meta_i = np.arange(nb, dtype=np.int32)
    flags = np.full(nb, 2, dtype=np.int32)
    t0 = np.zeros(nb + 2, dtype=np.int32)
    ln = np.zeros(nb + 2, dtype=np.int32)
    kind = np.full(nb, -2, dtype=np.int32)
    src = np.zeros((len(group_shapes), nb), dtype=np.int32)
    for gi, bi, nreal, b0 in seqs:
        for j in range(nreal + 1):
            b = b0 + j
            meta_i[b] = b0
            t0[b + 1] = -META_ROW0 + BLOCK * j
            ln[b + 1] = N_META + BLOCK * nreal
            flags[b] = (0 if j == 0 else 2) | (1 if j >= 2 else 0) | (4 if j < nreal else 0)
            kind[b] = -1 if j == 0 else gi
            if j > 0:
                src[gi, b:] = bi * nreal + (j - 1)
    variant = np.asarray([BLOCK_FLAG_VARIANTS.index(int(f)) for f in flags], dtype=np.int32)
    return dict(nb=nb, meta=meta_i, variant=variant, t0=t0, ln=ln, kind=kind, src=src)


def _alibi_bias_table():
    nv = len(BLOCK_FLAG_VARIANTS)
    slopes = 2.0 ** (-8.0 * np.arange(1, N_HEADS + 1) / N_HEADS)
    key = np.arange(BAND_KEYS)[:, None]
    qry = np.arange(BLOCK)[None, :]
    rel = np.abs(key - BLOCK - qry).astype(np.float32)[None, :, None, :]
    scaled = (slopes * LOG2E).astype(np.float32).reshape(N_KV_HEADS, 1, GQA_GROUP, 1)
    band = np.where(rel <= WINDOW, -(scaled * rel), np.float32(NEG))
    flags = np.asarray(BLOCK_FLAG_VARIANTS)[:, None]
    usable = (flags >> (np.arange(BAND_KEYS)[None, :] // BLOCK)) & 1
    band = np.where(usable[:, None, :, None, None] != 0, band[None], np.float32(NEG))
    band = band.reshape(nv, N_KV_HEADS, BAND_KEYS, GQA_GROUP * BLOCK)
    meta = np.zeros((nv, N_KV_HEADS, N_META, GQA_GROUP * BLOCK), np.float32)
    return np.concatenate([band, meta], axis=2).astype(np.float32)


def _const_spec(shape, index_map):
    return pl.BlockSpec(shape, index_map, pipeline_mode=pl.Buffered(1))


def _layer_of(stacked, layer):
    return layer if stacked.shape[0] > 1 else 0


def _layer_spec(stacked, layer):
    rest = stacked.shape[1:]
    index = (_layer_of(stacked, layer),) + (0,) * len(rest)
    return _const_spec((None,) + rest, lambda *_: index)


def _layer_norm(x, g, b):
    mu = jnp.mean(x, axis=-1, keepdims=True)
    xc = x - mu
    var = jnp.mean(xc * xc, axis=-1, keepdims=True)
    return xc * lax.rsqrt(var + LN_EPS) * g + b


def _dot(a, b):
    return jnp.dot(a, b, preferred_element_type=F32)


def _half(hf):
    return slice(HALF_TILE * hf, HALF_TILE * (hf + 1))


def _qvt_kernel(w_ref, o_ref):
    o_ref[...] = w_ref[...].T.astype(BF16)


def _qv_transposed(w_in):
    depth = w_in.shape[0]
    q_blocks = D_MODEL // KV_WIDTH
    v_block0 = (D_MODEL + KV_WIDTH) // KV_WIDTH
    n_blocks = q_blocks + 1
    src_col = lambda c: jnp.where(c < q_blocks, c, c - q_blocks + v_block0)
    return pl.pallas_call(
        _qvt_kernel,
        grid=(depth, n_blocks),
        in_specs=[pl.BlockSpec((None, D_MODEL, KV_WIDTH), lambda l, c: (l, 0, src_col(c)))],
        out_specs=pl.BlockSpec((None, KV_WIDTH, D_MODEL), lambda l, c: (l, c, 0)),
        out_shape=jax.ShapeDtypeStruct((depth, D_MODEL + KV_WIDTH, D_MODEL), BF16),
        name="qv_transpose",
    )(w_in)


def _project(hf, h_ref, w_ref, wt_ref, qt_ref, k_ref, vt_ref, g_ref):
    k0 = D_MODEL
    g0 = D_MODEL + 2 * KV_WIDTH + POOL_WIDTH
    rows = _half(hf)
    x = h_ref[rows, :].astype(BF16)
    gates = _dot(x, w_ref[:, g0:g0 + GATE_WIDTH])
    g_ref[rows, :] = (0.5 * jnp.tanh(0.5 * gates) + 0.5).astype(BF16)
    qt = lax.dot_general(wt_ref[0:D_MODEL, :], x, NT_DIMS, preferred_element_type=F32)
    qt_ref[:, rows] = (qt * (HEAD_DIM ** -0.5 * LOG2E)).astype(BF16)
    vt = lax.dot_general(wt_ref[D_MODEL:D_MODEL + KV_WIDTH, :], x, NT_DIMS, preferred_element_type=F32)
    vt_ref[:, rows] = vt.astype(BF16)
    k_ref[rows, :] = _dot(x, w_ref[:, k0:k0 + KV_WIDTH]).astype(BF16)


def _inproj_kernel(h_ref, *refs):
    _project(0, h_ref, *refs)
    _project(1, h_ref, *refs)


def _inproj_embed_kernel(kind_t, srca_t, srcb_t, *refs):
    del srca_t, srcb_t
    xa_refs, xb_refs = refs[:BLOCKS_PER_TILE], refs[BLOCKS_PER_TILE:2 * BLOCKS_PER_TILE]
    hdr_ref, eg_ref, eb_ref = refs[2 * BLOCKS_PER_TILE:2 * BLOCKS_PER_TILE + 3]
    w_ref, wt_ref, h_ref, qt_ref, k_ref, vt_ref, g_ref = refs[2 * BLOCKS_PER_TILE + 3:]
    for hf in range(2):
        for j in range(hf * BLOCKS_PER_TILE // 2, (hf + 1) * BLOCKS_PER_TILE // 2):
            kind = kind_t[pl.program_id(0) * BLOCKS_PER_TILE + j]
            x = jnp.where(kind == 0, xa_refs[j][...], jnp.where(kind == 1, xb_refs[j][...], 0.0))
            x = jnp.where(kind == -1, hdr_ref[...], x)
            h_ref[BLOCK * j:BLOCK * (j + 1), :] = _layer_norm(x, eg_ref[...], eb_ref[...])
        _project(hf, h_ref, w_ref, wt_ref, qt_ref, k_ref, vt_ref, g_ref)


def _inproj_out(t, row, col):
    specs = [pl.BlockSpec((D_MODEL, TOKEN_TILE), col),
             pl.BlockSpec((TOKEN_TILE, KV_WIDTH), row),
             pl.BlockSpec((KV_WIDTH, TOKEN_TILE), col),
             pl.BlockSpec((TOKEN_TILE, GATE_WIDTH), row)]
    shapes = [jax.ShapeDtypeStruct((D_MODEL, t), BF16),
              jax.ShapeDtypeStruct((t, KV_WIDTH), BF16),
              jax.ShapeDtypeStruct((KV_WIDTH, t), BF16),
              jax.ShapeDtypeStruct((t, GATE_WIDTH), BF16)]
    return specs, shapes


def _inproj_embed(xa, xb, header, eg, eb, w_in, w_qvt, lay):
    t = lay["nb"] * BLOCK
    n = BLOCKS_PER_TILE
    blk = lambda sel: pl.BlockSpec((BLOCK, D_MODEL), sel)
    from_a = [blk(lambda i, k, sa, sb, j=j: (sa[i * n + j], 0)) for j in range(n)]
    from_b = [blk(lambda i, k, sa, sb, j=j: (sb[i * n + j], 0)) for j in range(n)]
    const = lambda i, *_: (0, 0)
    row = lambda i, *_: (i, 0)
    col = lambda i, *_: (0, i)
    out_specs, out_shape = _inproj_out(t, row, col)
    grid_spec = pltpu.PrefetchScalarGridSpec(
        num_scalar_prefetch=3,
        grid=(t // TOKEN_TILE,),
        in_specs=from_a + from_b + [blk(const), pl.BlockSpec((1, D_MODEL), const),
                                    pl.BlockSpec((1, D_MODEL), const),
                                    _layer_spec(w_in, 0), _layer_spec(w_qvt, 0)],
        out_specs=[pl.BlockSpec((TOKEN_TILE, D_MODEL), row)] + out_specs,
    )
    return pl.pallas_call(
        _inproj_embed_kernel,
        grid_spec=grid_spec,
        out_shape=[jax.ShapeDtypeStruct((t, D_MODEL), F32)] + out_shape,
        compiler_params=pltpu.CompilerParams(vmem_limit_bytes=VMEM_LIMIT),
        name="embed_in_proj",
    )(lay["kind"], lay["src"][0], lay["src"][1], *([xa] * n), *([xb] * n), header, eg, eb, w_in, w_qvt)


def _inproj(h, w_in, w_qvt, layer):
    t = h.shape[0]
    row = lambda i: (i, 0)
    col = lambda i: (0, i)
    out_specs, out_shape = _inproj_out(t, row, col)
    return pl.pallas_call(
        _inproj_kernel,
        grid=(t // TOKEN_TILE,),
        in_specs=[pl.BlockSpec((TOKEN_TILE, D_MODEL), row),
                  _layer_spec(w_in, layer), _layer_spec(w_qvt, layer)],
        out_specs=out_specs,
        out_shape=out_shape,
        compiler_params=pltpu.CompilerParams(vmem_limit_bytes=VMEM_LIMIT),
        name="in_proj",
    )(h, w_in, w_qvt)


def _attn_kernel(meta_t, var_t, sink_ref, qt_ref, *refs, n_cast):
    del meta_t
    kp_ref, ko_ref, kn_ref = refs[0:3]
    kh_refs = refs[3:3 + ATTN_BLOCKS]
    vp_ref, vo_ref, vn_ref = refs[3 + ATTN_BLOCKS:6 + ATTN_BLOCKS]
    vh_refs = refs[6 + ATTN_BLOCKS:6 + 2 * ATTN_BLOCKS]
    n_in = 7 + 2 * ATTN_BLOCKS + n_cast
    bias_ref, o_ref = refs[6 + 2 * ATTN_BLOCKS], refs[n_in]
    for src_ref, dst_ref in zip(refs[n_in - n_cast:n_in], refs[n_in + 1:]):
        dst_ref[...] = src_ref[...].astype(BF16)
    step = pl.program_id(0)
    width = GQA_GROUP * BLOCK
    zero_q = jnp.zeros((HEAD_DIM, width), BF16)
    zero_p = jnp.zeros((META_ROW0, width), BF16)
    ones_band = jnp.ones((ONES_ROWS, BAND_KEYS), BF16)
    ones_meta = jnp.ones((ONES_ROWS, BLOCK), BF16)

    def window(blk, before, own, after, axis):
        parts = []
        for j in range(blk - 1, blk + 2):
            sel = [slice(None), slice(None)]
            if j < 0:
                parts.append(before)
            elif j >= ATTN_BLOCKS:
                parts.append(after)
            else:
                sel[axis] = slice(BLOCK * j, BLOCK * (j + 1))
                parts.append(own[tuple(sel)])
        return parts

    def scores(blk, k):
        pair = slice(BLOCK * (k // 2), BLOCK * (k // 2 + 1))
        band = window(blk, kp_ref[:, pair], ko_ref[:, pair], kn_ref[:, pair], 0)
        kc = jnp.concatenate(band + [kh_refs[blk][META_ROW0:BLOCK, pair]], axis=0)
        q0 = GQA_GROUP * HEAD_DIM * k
        qs = slice(BLOCK * blk, BLOCK * (blk + 1))
        qt4 = jnp.concatenate([qt_ref[q0 + HEAD_DIM * j:q0 + HEAD_DIM * (j + 1), qs]
                               for j in range(GQA_GROUP)], axis=1)
        rhs = jnp.concatenate([qt4, zero_q] if k % 2 == 0 else [zero_q, qt4], axis=0)
        return _dot(kc, rhs) + bias_ref[var_t[ATTN_BLOCKS * step + blk], k]

    def finish(blk, k, s):
        sink_row = jnp.concatenate([jnp.full((1, BLOCK), sink_ref[GQA_GROUP * k + j] * LOG2E, F32)
                                    for j in range(GQA_GROUP)], axis=1)
        m = jnp.maximum(jnp.max(s, axis=0, keepdims=True), sink_row)
        pb = jnp.exp2(s - m).astype(BF16)
        vs = slice(HEAD_DIM * k, HEAD_DIM * (k + 1))
        band = window(blk, vp_ref[vs, :], vo_ref[vs, :], vn_ref[vs, :], 1)
        vc = jnp.concatenate([jnp.concatenate(band, axis=1), ones_band], axis=0)
        vh = jnp.concatenate([vh_refs[blk][vs, :], ones_meta], axis=0)
        p_meta = jnp.concatenate([zero_p, pb[BAND_KEYS:KEYS]], axis=0)
        acc = _dot(vc, pb[0:BAND_KEYS]) + _dot(vh, p_meta)
        denom = acc[HEAD_DIM:HEAD_DIM + 1] + jnp.exp2(sink_row - m)
        ot = acc[0:HEAD_DIM] * (1.0 / denom)
        rows = slice(BLOCK * blk, BLOCK * (blk + 1))
        for j in range(GQA_GROUP // 2):
            two = jnp.concatenate([ot[:, 2 * BLOCK * j:2 * BLOCK * j + BLOCK],
                                   ot[:, 2 * BLOCK * j + BLOCK:2 * BLOCK * (j + 1)]], axis=0)
            c0 = GQA_GROUP * HEAD_DIM * k + BLOCK * j
            o_ref[rows, c0:c0 + BLOCK] = two.T.astype(BF16)

    units = [(blk, k) for blk in range(ATTN_BLOCKS) for k in range(N_KV_HEADS)]
    pending = [scores(*u) for u in units[:SCORE_LOOKAHEAD]]
    for idx, unit in enumerate(units):
        if idx + SCORE_LOOKAHEAD < len(units):
            pending.append(scores(*units[idx + SCORE_LOOKAHEAD]))
        finish(*unit, pending.pop(0))


def _attention(qt, kk, vt, sink, bias, lay, casts):
    t = kk.shape[0]
    nb = lay["nb"]
    n = ATTN_BLOCKS
    steps = nb // n
    cast_in, cast_out, cast_shape = [], [], []
    for w, layer in casts:
        rows, cols = w.shape[1:]
        chunks = min(1 << (steps.bit_length() - 1), rows // CAST_MIN_ROWS)
        assert rows % chunks == 0 and (rows // chunks) % CAST_MIN_ROWS == 0
        chunk = lambda i, m, v, c=chunks: jnp.minimum(i, c - 1)
        cast_in.append(pl.BlockSpec((None, rows // chunks, cols),
                                    lambda i, m, v, chunk=chunk, layer=layer: (layer, chunk(i, m, v), 0)))
        cast_out.append(pl.BlockSpec((None, rows // chunks, cols),
                                     lambda i, m, v, chunk=chunk: (0, chunk(i, m, v), 0)))
        cast_shape.append(jax.ShapeDtypeStruct((1, rows, cols), BF16))
    kspec = lambda rows, sel: pl.BlockSpec((rows, KV_WIDTH), sel)
    vspec = lambda cols, sel: pl.BlockSpec((KV_WIDTH, cols), sel)
    before = lambda i: jnp.maximum(n * i - 1, 0)
    after = lambda i: jnp.minimum(n * i + n, nb - 1)
    k_specs = [kspec(BLOCK, lambda i, m, v: (before(i), 0)),
               kspec(n * BLOCK, lambda i, m, v: (i, 0)),
               kspec(BLOCK, lambda i, m, v: (after(i), 0))]
    k_specs += [kspec(BLOCK, lambda i, m, v, j=j: (m[n * i + j], 0)) for j in range(n)]
    v_specs = [vspec(BLOCK, lambda i, m, v: (0, before(i))),
               vspec(n * BLOCK, lambda i, m, v: (0, i)),
               vspec(BLOCK, lambda i, m, v: (0, after(i)))]
    v_specs += [vspec(BLOCK, lambda i, m, v, j=j: (0, m[n * i + j])) for j in range(n)]
    grid_spec = pltpu.PrefetchScalarGridSpec(
        num_scalar_prefetch=2,
        grid=(nb // n,),
        in_specs=[pl.BlockSpec(memory_space=pltpu.SMEM),
                  pl.BlockSpec((D_MODEL, n * BLOCK), lambda i, m, v: (0, i))] + k_specs + v_specs
                 + [_const_spec((len(BLOCK_FLAG_VARIANTS), N_KV_HEADS, KEYS, GQA_GROUP * BLOCK),
                                lambda i, m, v: (0, 0, 0, 0))] + cast_in,
        out_specs=[pl.BlockSpec((n * BLOCK, D_MODEL), lambda i, m, v: (i, 0))] + cast_out,
    )
    outs = pl.pallas_call(
        functools.partial(_attn_kernel, n_cast=len(casts)),
        grid_spec=grid_spec,
        out_shape=[jax.ShapeDtypeStruct((t, D_MODEL), BF16)] + cast_shape,
        compiler_params=pltpu.CompilerParams(vmem_limit_bytes=VMEM_LIMIT,
                                             dimension_semantics=("arbitrary",)),
        name="band_attn",
    )(lay["meta"], lay["variant"], sink, qt, *([kk] * (3 + n)), *([vt] * (3 + n)), bias,
      *[w for w, _ in casts])
    return outs[0], outs[1:]


def _pool_diff(u_ext, t_ext, l_ext):
    return [_pool_group(u_ext, t_ext, l_ext, gi) for gi in range(len(POOL_WINDOWS))]


def _pool_group(u_ext, t_ext, l_ext, gi):
    w = POOL_WINDOWS[gi]
    n = u_ext.shape[0]
    rows = n - 2 * HALO
    valid = (t_ext >= 0) & (t_ext < l_ext)
    t = t_ext[HALO:HALO + rows]
    ln = l_ext[HALO:HALO + rows]
    x = jnp.where(valid, u_ext[:, gi * POOL_GROUP_WIDTH:(gi + 1) * POOL_GROUP_WIDTH], 0.0)
    acc, k = x, 1
    while k < w // 2:
        acc = acc + pltpu.roll(acc, n - k, 0)
        k *= 2
    win = acc + pltpu.roll(acc, w // 2, 0)
    cnt = jnp.minimum(t + w // 2, ln) - jnp.maximum(t - w // 2, 0)
    cnt = jnp.maximum(cnt, 1).astype(F32)
    return win[HALO:HALO + rows] / cnt - x[HALO:HALO + rows]


def _mix_kernel(t0_t, ln_t,
                attn_ref, g_ref, h_ref, hp_ref, hn_ref,
                wu_ref, wa_ref, wp_ref, ps_ref, wb_ref, wo_ref, lg_ref, lb_ref, o_ref, *, alpha):
    b0 = pl.program_id(0) * BLOCKS_PER_TILE

    def rows_of(entry, first, count):
        r = lax.broadcasted_iota(jnp.int32, (count, POOL_GROUP_WIDTH), 0) + first
        return r + t0_t[entry], jnp.zeros((count, POOL_GROUP_WIDTH), jnp.int32) + ln_t[entry]

    parts = [rows_of(b0, BLOCK - HALO, HALO)]
    parts += [rows_of(b0 + 1 + j, 0, BLOCK) for j in range(BLOCKS_PER_TILE)]
    parts += [rows_of(b0 + 1 + BLOCKS_PER_TILE, 0, HALO)]
    t_ext = jnp.concatenate([p[0] for p in parts], axis=0)
    l_ext = jnp.concatenate([p[1] for p in parts], axis=0)

    u_halves = [_dot(h_ref[_half(hf), :].astype(BF16), wu_ref[...]) for hf in range(2)]
    h_halo = jnp.concatenate([hp_ref[...], hn_ref[...]], axis=0).astype(BF16)
    u_halo = _dot(h_halo, wu_ref[...])
    u_ext = jnp.concatenate([u_halo[0:HALO]] + u_halves + [u_halo[HALO:2 * HALO]], axis=0)

    def pool(hf):
        ext = slice(HALF_TILE * hf, HALF_TILE * (hf + 1) + 2 * HALO)
        return _pool_diff(u_ext[ext], t_ext[ext], l_ext[ext])

    def attn_branch(hf):
        return _dot(attn_ref[_half(hf), :], wa_ref[...])

    def pool_branch(diffs):
        y = jnp.concatenate([_dot(d.astype(BF16), wp_ref[gi]) for gi, d in enumerate(diffs)], axis=1)
        return _dot((y * ps_ref[...]).astype(BF16), wb_ref[...])

    def gate(hf, ya, yb):
        g = g_ref[_half(hf), :].astype(F32)
        return (g[:, 0:D_MODEL] * ya + g[:, D_MODEL:2 * D_MODEL] * yb).astype(BF16)

    def norm(rows, mixed):
        o_ref[rows, :] = _layer_norm(alpha * h_ref[rows, :] + mixed, lg_ref[...], lb_ref[...])

    ya_a = attn_branch(0)
    pool_a = pool(0)
    ya_b = attn_branch(1)
    pool_b = pool(1)
    yb_a = pool_branch(pool_a)
    yb_b = pool_branch(pool_b)
    for hf, (ya, yb) in enumerate(((ya_a, yb_a), (ya_b, yb_b))):
        gated = gate(hf, ya, yb)
        for piece in range(HALF_TILE // TAIL_ROWS):
            local = slice(TAIL_ROWS * piece, TAIL_ROWS * (piece + 1))
            norm(slice(HALF_TILE * hf + local.start, HALF_TILE * hf + local.stop),
                 _dot(gated[local], wo_ref[...]))


def _mix(attn, g, h, w_in, weights, layer, lay, alpha):
    t = h.shape[0]
    n_halo = t // HALO
    per = TOKEN_TILE // HALO
    u_block = (D_MODEL + 2 * KV_WIDTH) // POOL_WIDTH
    row = lambda i, a, b: (i, 0)
    grid_spec = pltpu.PrefetchScalarGridSpec(
        num_scalar_prefetch=2,
        grid=(t // TOKEN_TILE,),
        in_specs=[pl.BlockSpec((TOKEN_TILE, D_MODEL), row),
                  pl.BlockSpec((TOKEN_TILE, GATE_WIDTH), row),
                  pl.BlockSpec((TOKEN_TILE, D_MODEL), row),
                  pl.BlockSpec((HALO, D_MODEL), lambda i, a, b: (jnp.maximum(i * per - 1, 0), 0)),
                  pl.BlockSpec((HALO, D_MODEL), lambda i, a, b: (jnp.minimum((i + 1) * per, n_halo - 1), 0)),
                  _const_spec((None, D_MODEL, POOL_WIDTH),
                              lambda i, a, b: (_layer_of(w_in, layer), 0, u_block))]
                 + [_layer_spec(w, layer) for w in weights],
        out_specs=pl.BlockSpec((TOKEN_TILE, D_MODEL), row),
    )
    return pl.pallas_call(
        functools.partial(_mix_kernel, alpha=alpha),
        grid_spec=grid_spec,
        out_shape=jax.ShapeDtypeStruct((t, D_MODEL), F32),
        compiler_params=pltpu.CompilerParams(vmem_limit_bytes=VMEM_LIMIT),
        name="mix_ln",
    )(lay["t0"], lay["ln"], attn, g, h, h, h, w_in, *weights)


def _mlp_body(x_ref, w1_ref, b1_ref, w2_ref, b2_ref, lg_ref, lb_ref, store, alpha):
    def ff(xb, chunks):
        acc = None
        for c in chunks:
            cs = slice(c * FF_CHUNK, (c + 1) * FF_CHUNK)
            a = jnp.maximum(_dot(xb, w1_ref[:, cs]) + b1_ref[:, cs], 0.0)
            part = _dot((a * a).astype(BF16), w2_ref[cs, :])
            acc = part if acc is None else acc + part
        return acc

    def norm(x, acc):
        return _layer_norm(alpha * x + (acc + b2_ref[...]), lg_ref[...], lb_ref[...])

    n_chunks = D_FF // FF_CHUNK
    x = x_ref[...]
    xb = x.astype(BF16)
    acc = ff(xb, tuple(range(n_chunks - 1)))
    for piece in range(TOKEN_TILE // TAIL_ROWS):
        rows = slice(TAIL_ROWS * piece, TAIL_ROWS * (piece + 1))
        store(rows, norm(x[rows], acc[rows] + ff(xb[rows], (n_chunks - 1,))))


def _mlp_kernel(kind_t, dsta_t, dstb_t, *refs, alpha):
    del kind_t, dsta_t, dstb_t
    ins, o_ref = refs[:-1], refs[-1]

    def store(rows, value):
        o_ref[rows, :] = value

    _mlp_body(*ins, store, alpha)


def _mlp_final_kernel(kind_t, dsta_t, dstb_t, *refs, alpha):
    ins, (ya_ref, yb_ref, buf_ref, sem_ref) = refs[:-4], refs[-4:]
    step = pl.program_id(0)
    slot = step % 2

    def store(rows, value):
        buf_ref[slot, rows, :] = value

    _mlp_body(*ins, store, alpha)

    def copies(of_step, of_slot, act):
        for j in range(BLOCKS_PER_TILE):
            b = of_step * BLOCKS_PER_TILE + j
            for gi, (y_ref, dst_t) in enumerate(((ya_ref, dsta_t), (yb_ref, dstb_t))):
                @pl.when(kind_t[b] == gi)
                def _():
                    src = buf_ref.at[of_slot, pl.ds(BLOCK * j, BLOCK), :]
                    dst = y_ref.at[pl.ds(pl.multiple_of(dst_t[b] * BLOCK, BLOCK), BLOCK), :]
                    act(pltpu.make_async_copy(src, dst, sem_ref.at[of_slot, j]))

    copies(step, slot, lambda c: c.start())

    @pl.when(step > 0)
    def _():
        copies(step - 1, 1 - slot, lambda c: c.wait())

    @pl.when(step == pl.num_programs(0) - 1)
    def _():
        copies(step, slot, lambda c: c.wait())


def _mlp(x, weights, layer, lay, alpha, out_rows=None):
    t = x.shape[0]
    row = lambda i, *_: (i, 0)
    in_specs = [pl.BlockSpec((TOKEN_TILE, D_MODEL), row)] + [_layer_spec(w, layer) for w in weights]
    if out_rows is None:
        body, name = _mlp_kernel, "mlp_ln"
        out_specs = pl.BlockSpec((TOKEN_TILE, D_MODEL), row)
        out_shape = jax.ShapeDtypeStruct((t, D_MODEL), F32)
        scratch = []
    else:
        body, name = _mlp_final_kernel, "mlp_ln_out"
        out_specs = [pl.BlockSpec(memory_space=pl.ANY)] * 2
        out_shape = [jax.ShapeDtypeStruct((r, D_MODEL), F32) for r in out_rows]
        scratch = [pltpu.VMEM((2, TOKEN_TILE, D_MODEL), F32),
                   pltpu.SemaphoreType.DMA((2, BLOCKS_PER_TILE))]
    grid_spec = pltpu.PrefetchScalarGridSpec(
        num_scalar_prefetch=3, grid=(t // TOKEN_TILE,),
        in_specs=in_specs, out_specs=out_specs, scratch_shapes=scratch)
    return pl.pallas_call(
        functools.partial(body, alpha=alpha),
        grid_spec=grid_spec,
        out_shape=out_shape,
        compiler_params=pltpu.CompilerParams(vmem_limit_bytes=VMEM_LIMIT,
                                             dimension_semantics=("arbitrary",)),
        name=name,
    )(lay["kind"], lay["src"][0], lay["src"][1], x, *weights)


def kernel(x_prompt, x_sample, meta_tokens, ln_emb_g, ln_emb_b, w_in, sink, w_pool, pool_scale, w_bo_attn,
           w_bo_pool, w_out, ln1_g, ln1_b, w_mlp1, b_mlp1, w_mlp2, b_mlp2, ln2_g, ln2_b):
    depth = w_in.shape[0]
    alpha = float((2 * depth) ** 0.25)
    groups = (x_prompt, x_sample)
    lay = _layout([x.shape[:2] for x in groups])
    header = jnp.concatenate([jnp.zeros((META_ROW0, D_MODEL), F32), meta_tokens.astype(F32)], axis=0)

    row3 = lambda a: a.reshape(a.shape[0], 1, a.shape[-1])
    w_qvt = _qv_transposed(w_in)
    w_pool_rows = w_pool.reshape(depth, -1, POOL_GROUP_WIDTH)
    bias = jnp.asarray(_alibi_bias_table())

    out_rows = [x.shape[0] * x.shape[1] for x in groups]
    w_in_b = w_in[0:1].astype(BF16)
    for l in range(depth):
        if l == 0:
            h, qt, kk, vt, g = _inproj_embed(
                x_prompt.reshape(-1, D_MODEL), x_sample.reshape(-1, D_MODEL), header,
                ln_emb_g.reshape(1, -1), ln_emb_b.reshape(1, -1), w_in_b, w_qvt, lay)
        else:
            qt, kk, vt, g = _inproj(h, w_in_b, w_qvt, l)
        casts = [(w_bo_attn, l), (w_pool_rows, l), (w_bo_pool, l), (w_out, l), (w_mlp1, l), (w_mlp2, l)]
        if l + 1 < depth:
            casts.append((w_in, l + 1))
        attn, cast = _attention(qt, kk, vt, sink[l].astype(F32), bias, lay, casts)
        wa_b, wp_b, wb_b, wo_b, w1_b, w2_b = cast[:6]
        mix_w = (wa_b, wp_b.reshape((1,) + w_pool.shape[1:]), row3(pool_scale), wb_b, wo_b,
                 row3(ln1_g), row3(ln1_b))
        x1 = _mix(attn, g, h, w_in_b, mix_w, l, lay, alpha)
        mlp_w = (w1_b, row3(b_mlp1), w2_b, row3(b_mlp2), row3(ln2_g), row3(ln2_b))
        h = _mlp(x1, mlp_w, l, lay, alpha, out_rows if l == depth - 1 else None)
        if l + 1 < depth:
            w_in_b = cast[6]
    return tuple(y.reshape(x.shape) for y, x in zip(h, groups))
```

```python
import functools

import numpy as np
import jax
import jax.numpy as jnp
from jax import lax
from jax.experimental import pallas as pl
from jax.experimental.pallas import tpu as pltpu

D_MODEL = 1024
N_META = 16
N_HEADS = 16
N_KV_HEADS = 4
HEAD_DIM = 64
GQA_GROUP = N_HEADS // N_KV_HEADS
WINDOW = 128
BLOCK = 128
POOL_WINDOWS = (2, 4, 8, 16)
POOL_GROUP_WIDTH = 128
POOL_WIDTH = 512
KV_WIDTH = N_KV_HEADS * HEAD_DIM
GATE_WIDTH = 2 * D_MODEL
D_FF = 4 * D_MODEL
FF_CHUNK = 1024
LN_EPS = 1e-5
HALO = 8
META_ROW0 = BLOCK - N_META
BAND_KEYS = 3 * BLOCK
KEYS = BAND_KEYS + N_META
NEG = -1e30
LOG2E = 1.4426950408889634
ONES_ROWS = 16
BLOCK_FLAG_VARIANTS = (7, 6, 3, 4, 2)
TOKEN_TILE = 1024
HALF_TILE = TOKEN_TILE // 2
BLOCKS_PER_TILE = TOKEN_TILE // BLOCK
TAIL_ROWS = 256
ATTN_BLOCKS = 8
UNIT_WIDTH = 2 * BLOCK
SCORE_LOOKAHEAD = 4
CAST_MIN_ROWS = 16
VMEM_LIMIT = 56 * 1024 * 1024

F32 = jnp.float32
BF16 = jnp.bfloat16
NT_DIMS = (((1,), (1,)), ((), ()))


def _layout(group_shapes):
    seqs = []
    n = 0
    for gi, (bsz, s) in enumerate(group_shapes):
        assert s % BLOCK == 0 and s >= BLOCK
        for bi in range(bsz):
            seqs.append((gi, bi, s // BLOCK, n))
            n += 1 + s // BLOCK
    nb = -(-n // BLOCKS_PER_TILE) * BLOCKS_PER_TILE
    meta_i = np.arange(nb, dtype=np.int32)
    flags = np.full(nb, 2, dtype=np.int32)
    t0 = np.zeros(nb + 2, dtype=np.int32)
    ln = np.zeros(nb + 2, dtype=np.int32)
    kind = np.full(nb, -2, dtype=np.int32)
    src = np.zeros((len(group_shapes), nb), dtype=np.int32)
    for gi, bi, nreal, b0 in seqs:
        for j in range(nreal + 1):
            b = b0 + j
            meta_i[b] = b0
            t0[b + 1] = -META_ROW0 + BLOCK * j
            ln[b + 1] = N_META + BLOCK * nreal
            flags[b] = (0 if j == 0 else 2) | (1 if j >= 2 else 0) | (4 if j < nreal else 0)
            kind[b] = -1 if j == 0 else gi
            if j > 0:
                src[gi, b:] = bi * nreal + (j - 1)
    variant = np.asarray([BLOCK_FLAG_VARIANTS.index(int(f)) for f in flags], dtype=np.int32)
    return dict(nb=nb, meta=meta_i, variant=variant, t0=t0, ln=ln, kind=kind, src=src)


def _alibi_bias_table():
    nv = len(BLOCK_FLAG_VARIANTS)
    slopes = 2.0 ** (-8.0 * np.arange(1, N_HEADS + 1) / N_HEADS)
    key = np.arange(BAND_KEYS)[:, None]
    qry = np.arange(BLOCK)[None, :]
    rel = np.abs(key - BLOCK - qry).astype(np.float32)[None, :, None, :]
    scaled = (slopes * LOG2E).astype(np.float32).reshape(N_KV_HEADS, 1, GQA_GROUP, 1)
    band = np.where(rel <= WINDOW, -(scaled * rel), np.float32(NEG))
    flags = np.asarray(BLOCK_FLAG_VARIANTS)[:, None]
    usable = (flags >> (np.arange(BAND_KEYS)[None, :] // BLOCK)) & 1
    band = np.where(usable[:, None, :, None, None] != 0, band[None], np.float32(NEG))
    band = band.reshape(nv, N_KV_HEADS, BAND_KEYS, GQA_GROUP * BLOCK)
    meta = np.zeros((nv, N_KV_HEADS, N_META, GQA_GROUP * BLOCK), np.float32)
    return np.concatenate([band, meta], axis=2).astype(np.float32)


def _const_spec(shape, index_map):
    return pl.BlockSpec(shape, index_map, pipeline_mode=pl.Buffered(1))


def _layer_of(stacked, layer):
    return layer if stacked.shape[0] > 1 else 0


def _layer_spec(stacked, layer):
    rest = stacked.shape[1:]
    index = (_layer_of(stacked, layer),) + (0,) * len(rest)
    return _const_spec((None,) + rest, lambda *_: index)


def _layer_norm(x, g, b):
    mu = jnp.mean(x, axis=-1, keepdims=True)
    xc = x - mu
    var = jnp.mean(xc * xc, axis=-1, keepdims=True)
    return xc * lax.rsqrt(var + LN_EPS) * g + b


def _dot(a, b):
    return jnp.dot(a, b, preferred_element_type=F32)


def _half(hf):
    return slice(HALF_TILE * hf, HALF_TILE * (hf + 1))


def _qvt_kernel(w_ref, o_ref):
    o_ref[...] = w_ref[...].T.astype(BF16)


def _qv_transposed(w_in):
    depth = w_in.shape[0]
    q_blocks = D_MODEL // KV_WIDTH
    v_block0 = (D_MODEL + KV_WIDTH) // KV_WIDTH
    n_blocks = q_blocks + 1
    src_col = lambda c: jnp.where(c < q_blocks, c, c - q_blocks + v_block0)
    return pl.pallas_call(
        _qvt_kernel,
        grid=(depth, n_blocks),
        in_specs=[pl.BlockSpec((None, D_MODEL, KV_WIDTH), lambda l, c: (l, 0, src_col(c)))],
        out_specs=pl.BlockSpec((None, KV_WIDTH, D_MODEL), lambda l, c: (l, c, 0)),
        out_shape=jax.ShapeDtypeStruct((depth, D_MODEL + KV_WIDTH, D_MODEL), BF16),
        name="qv_transpose",
    )(w_in)


def _project(hf, h_ref, w_ref, wt_ref, qt_ref, k_ref, vt_ref, g_ref):
    k0 = D_MODEL
    g0 = D_MODEL + 2 * KV_WIDTH + POOL_WIDTH
    rows = _half(hf)
    x = h_ref[rows, :].astype(BF16)
    gates = _dot(x, w_ref[:, g0:g0 + GATE_WIDTH])
    g_ref[rows, :] = (0.5 * jnp.tanh(0.5 * gates) + 0.5).astype(BF16)
    qt = lax.dot_general(wt_ref[0:D_MODEL, :], x, NT_DIMS, preferred_element_type=F32)
    qt_ref[:, rows] = (qt * (HEAD_DIM ** -0.5 * LOG2E)).astype(BF16)
    vt = lax.dot_general(wt_ref[D_MODEL:D_MODEL + KV_WIDTH, :], x, NT_DIMS, preferred_element_type=F32)
    vt_ref[:, rows] = vt.astype(BF16)
    k_ref[rows, :] = _dot(x, w_ref[:, k0:k0 + KV_WIDTH]).astype(BF16)


def _inproj_kernel(h_ref, *refs):
    _project(0, h_ref, *refs)
    _project(1, h_ref, *refs)


def _inproj_embed_kernel(kind_t, srca_t, srcb_t, *refs):
    del srca_t, srcb_t
    xa_refs, xb_refs = refs[:BLOCKS_PER_TILE], refs[BLOCKS_PER_TILE:2 * BLOCKS_PER_TILE]
    hdr_ref, eg_ref, eb_ref = refs[2 * BLOCKS_PER_TILE:2 * BLOCKS_PER_TILE + 3]
    w_ref, wt_ref, h_ref, qt_ref, k_ref, vt_ref, g_ref = refs[2 * BLOCKS_PER_TILE + 3:]
    for hf in range(2):
        for j in range(hf * BLOCKS_PER_TILE // 2, (hf + 1) * BLOCKS_PER_TILE // 2):
            kind = kind_t[pl.program_id(0) * BLOCKS_PER_TILE + j]
            x = jnp.where(kind == 0, xa_refs[j][...], jnp.where(kind == 1, xb_refs[j][...], 0.0))
            x = jnp.where(kind == -1, hdr_ref[...], x)
            h_ref[BLOCK * j:BLOCK * (j + 1), :] = _layer_norm(x, eg_ref[...], eb_ref[...])
        _project(hf, h_ref, w_ref, wt_ref, qt_ref, k_ref, vt_ref, g_ref)


def _inproj_out(t, row, col):
    specs = [pl.BlockSpec((D_MODEL, TOKEN_TILE), col),
             pl.BlockSpec((TOKEN_TILE, KV_WIDTH), row),
             pl.BlockSpec((KV_WIDTH, TOKEN_TILE), col),
             pl.BlockSpec((TOKEN_TILE, GATE_WIDTH), row)]
    shapes = [jax.ShapeDtypeStruct((D_MODEL, t), BF16),
              jax.ShapeDtypeStruct((t, KV_WIDTH), BF16),
              jax.ShapeDtypeStruct((KV_WIDTH, t), BF16),
              jax.ShapeDtypeStruct((t, GATE_WIDTH), BF16)]
    return specs, shapes


def _inproj_embed(xa, xb, header, eg, eb, w_in, w_qvt, lay):
    t = lay["nb"] * BLOCK
    n = BLOCKS_PER_TILE
    blk = lambda sel: pl.BlockSpec((BLOCK, D_MODEL), sel)
    from_a = [blk(lambda i, k, sa, sb, j=j: (sa[i * n + j], 0)) for j in range(n)]
    from_b = [blk(lambda i, k, sa, sb, j=j: (sb[i * n + j], 0)) for j in range(n)]
    const = lambda i, *_: (0, 0)
    row = lambda i, *_: (i, 0)
    col = lambda i, *_: (0, i)
    out_specs, out_shape = _inproj_out(t, row, col)
    grid_spec = pltpu.PrefetchScalarGridSpec(
        num_scalar_prefetch=3,
        grid=(t // TOKEN_TILE,),
        in_specs=from_a + from_b + [blk(const), pl.BlockSpec((1, D_MODEL), const),
                                    pl.BlockSpec((1, D_MODEL), const),
                                    _layer_spec(w_in, 0), _layer_spec(w_qvt, 0)],
        out_specs=[pl.BlockSpec((TOKEN_TILE, D_MODEL), row)] + out_specs,
    )
    return pl.pallas_call(
        _inproj_embed_kernel,
        grid_spec=grid_spec,
        out_shape=[jax.ShapeDtypeStruct((t, D_MODEL), F32)] + out_shape,
        compiler_params=pltpu.CompilerParams(vmem_limit_bytes=VMEM_LIMIT),
        name="embed_in_proj",
    )(lay["kind"], lay["src"][0], lay["src"][1], *([xa] * n), *([xb] * n), header, eg, eb, w_in, w_qvt)


def _inproj(h, w_in, w_qvt, layer):
    t = h.shape[0]
    row = lambda i: (i, 0)
    col = lambda i: (0, i)
    out_specs, out_shape = _inproj_out(t, row, col)
    return pl.pallas_call(
        _inproj_kernel,
        grid=(t // TOKEN_TILE,),
        in_specs=[pl.BlockSpec((TOKEN_TILE, D_MODEL), row),
                  _layer_spec(w_in, layer), _layer_spec(w_qvt, layer)],
        out_specs=out_specs,
        out_shape=out_shape,
        compiler_params=pltpu.CompilerParams(vmem_limit_bytes=VMEM_LIMIT),
        name="in_proj",
    )(h, w_in, w_qvt)


def _attn_kernel(meta_t, var_t, sink_ref, qt_ref, *refs, n_cast):
    del meta_t
    kp_ref, ko_ref, kn_ref = refs[0:3]
    kh_refs = refs[3:3 + ATTN_BLOCKS]
    vp_ref, vo_ref, vn_ref = refs[3 + ATTN_BLOCKS:6 + ATTN_BLOCKS]
    vh_refs = refs[6 + ATTN_BLOCKS:6 + 2 * ATTN_BLOCKS]
    n_in = 7 + 2 * ATTN_BLOCKS + n_cast
    bias_ref, o_ref = refs[6 + 2 * ATTN_BLOCKS], refs[n_in]
    for src_ref, dst_ref in zip(refs[n_in - n_cast:n_in], refs[n_in + 1:]):
        dst_ref[...] = src_ref[...].astype(BF16)
    step = pl.program_id(0)
    zero_q = jnp.zeros((HEAD_DIM, UNIT_WIDTH), BF16)
    zero_p = jnp.zeros((META_ROW0, UNIT_WIDTH), BF16)
    ones_band = jnp.ones((ONES_ROWS, BAND_KEYS), BF16)
    ones_meta = jnp.ones((ONES_ROWS, BLOCK), BF16)

    def window(blk, before, own, after, axis):
        parts = []
        for j in range(blk - 1, blk + 2):
            sel = [slice(None), slice(None)]
            if j < 0:
                parts.append(before)
            elif j >= ATTN_BLOCKS:
                parts.append(after)
            else:
                sel[axis] = slice(BLOCK * j, BLOCK * (j + 1))
                parts.append(own[tuple(sel)])
        return parts

    def scores(blk, k, hp):
        pair = slice(BLOCK * (k // 2), BLOCK * (k // 2 + 1))
        band = window(blk, kp_ref[:, pair], ko_ref[:, pair], kn_ref[:, pair], 0)
        kc = jnp.concatenate(band + [kh_refs[blk][META_ROW0:BLOCK, pair]], axis=0)
        q0 = GQA_GROUP * HEAD_DIM * k
        qs = slice(BLOCK * blk, BLOCK * (blk + 1))
        qt2 = jnp.concatenate([qt_ref[q0 + HEAD_DIM * j:q0 + HEAD_DIM * (j + 1), qs]
                               for j in range(2 * hp, 2 * hp + 2)], axis=1)
        rhs = jnp.concatenate([qt2, zero_q] if k % 2 == 0 else [zero_q, qt2], axis=0)
        cols = slice(UNIT_WIDTH * hp, UNIT_WIDTH * (hp + 1))
        return _dot(kc, rhs) + bias_ref[var_t[ATTN_BLOCKS * step + blk], k, :, cols]

    def finish(blk, k, hp, s):
        sink_row = jnp.concatenate([jnp.full((1, BLOCK), sink_ref[GQA_GROUP * k + j] * LOG2E, F32)
                                    for j in range(2 * hp, 2 * hp + 2)], axis=1)
        m = jnp.maximum(jnp.max(s, axis=0, keepdims=True), sink_row)
        pb = jnp.exp2(s - m).astype(BF16)
        vs = slice(HEAD_DIM * k, HEAD_DIM * (k + 1))
        band = window(blk, vp_ref[vs, :], vo_ref[vs, :], vn_ref[vs, :], 1)
        vc = jnp.concatenate([jnp.concatenate(band, axis=1), ones_band], axis=0)
        vh = jnp.concatenate([vh_refs[blk][vs, :], ones_meta], axis=0)
        p_meta = jnp.concatenate([zero_p, pb[BAND_KEYS:KEYS]], axis=0)
        acc = _dot(vc, pb[0:BAND_KEYS]) + _dot(vh, p_meta)
        denom = acc[HEAD_DIM:HEAD_DIM + 1] + jnp.exp2(sink_row - m)
        ot = acc[0:HEAD_DIM] * (1.0 / denom)
        rows = slice(BLOCK * blk, BLOCK * (blk + 1))
        two = jnp.concatenate([ot[:, 0:BLOCK], ot[:, BLOCK:2 * BLOCK]], axis=0)
        c0 = GQA_GROUP * HEAD_DIM * k + BLOCK * hp
        o_ref[rows, c0:c0 + BLOCK] = two.T.astype(BF16)

    units = [(blk, k, hp) for blk in range(ATTN_BLOCKS) for k in range(N_KV_HEADS)
             for hp in range(GQA_GROUP // 2)]
    pending = [scores(*u) for u in units[:SCORE_LOOKAHEAD]]
    for idx, unit in enumerate(units):
        if idx + SCORE_LOOKAHEAD < len(units):
            pending.append(scores(*units[idx + SCORE_LOOKAHEAD]))
        finish(*unit, pending.pop(0))


def _attention(qt, kk, vt, sink, bias, lay, casts):
    t = kk.shape[0]
    nb = lay["nb"]
    n = ATTN_BLOCKS
    steps = nb // n
    cast_in, cast_out, cast_shape = [], [], []
    for w, layer in casts:
        rows, cols = w.shape[1:]
        chunks = min(1 << (steps.bit_length() - 1), rows // CAST_MIN_ROWS)
        assert rows % chunks == 0 and (rows // chunks) % CAST_MIN_ROWS == 0
        chunk = lambda i, m, v, c=chunks: jnp.minimum(i, c - 1)
        cast_in.append(pl.BlockSpec((None, rows // chunks, cols),
                                    lambda i, m, v, chunk=chunk, layer=layer: (layer, chunk(i, m, v), 0)))
        cast_out.append(pl.BlockSpec((None, rows // chunks, cols),
                                     lambda i, m, v, chunk=chunk: (0, chunk(i, m, v), 0)))
        cast_shape.append(jax.ShapeDtypeStruct((1, rows, cols), BF16))
    kspec = lambda rows, sel: pl.BlockSpec((rows, KV_WIDTH), sel)
    vspec = lambda cols, sel: pl.BlockSpec((KV_WIDTH, cols), sel)
    before = lambda i: jnp.maximum(n * i - 1, 0)
    after = lambda i: jnp.minimum(n * i + n, nb - 1)
    k_specs = [kspec(BLOCK, lambda i, m, v: (before(i), 0)),
               kspec(n * BLOCK, lambda i, m, v: (i, 0)),
               kspec(BLOCK, lambda i, m, v: (after(i), 0))]
    k_specs += [kspec(BLOCK, lambda i, m, v, j=j: (m[n * i + j], 0)) for j in range(n)]
    v_specs = [vspec(BLOCK, lambda i, m, v: (0, before(i))),
               vspec(n * BLOCK, lambda i, m, v: (0, i)),
               vspec(BLOCK, lambda i, m, v: (0, after(i)))]
    v_specs += [vspec(BLOCK, lambda i, m, v, j=j: (0, m[n * i + j])) for j in range(n)]
    grid_spec = pltpu.PrefetchScalarGridSpec(
        num_scalar_prefetch=2,
        grid=(nb // n,),
        in_specs=[pl.BlockSpec(memory_space=pltpu.SMEM),
                  pl.BlockSpec((D_MODEL, n * BLOCK), lambda i, m, v: (0, i))] + k_specs + v_specs
                 + [_const_spec((len(BLOCK_FLAG_VARIANTS), N_KV_HEADS, KEYS, GQA_GROUP * BLOCK),
                                lambda i, m, v: (0, 0, 0, 0))] + cast_in,
        out_specs=[pl.BlockSpec((n * BLOCK, D_MODEL), lambda i, m, v: (i, 0))] + cast_out,
    )
    outs = pl.pallas_call(
        functools.partial(_attn_kernel, n_cast=len(casts)),
        grid_spec=grid_spec,
        out_shape=[jax.ShapeDtypeStruct((t, D_MODEL), BF16)] + cast_shape,
        compiler_params=pltpu.CompilerParams(vmem_limit_bytes=VMEM_LIMIT,
                                             dimension_semantics=("arbitrary",)),
        name="band_attn",
    )(lay["meta"], lay["variant"], sink, qt, *([kk] * (3 + n)), *([vt] * (3 + n)), bias,
      *[w for w, _ in casts])
    return outs[0], outs[1:]


def _pool_diff(u_ext, t_ext, l_ext):
    return [_pool_group(u_ext, t_ext, l_ext, gi) for gi in range(len(POOL_WINDOWS))]


def _pool_group(u_ext, t_ext, l_ext, gi):
    w = POOL_WINDOWS[gi]
    n = u_ext.shape[0]
    rows = n - 2 * HALO
    valid = (t_ext >= 0) & (t_ext < l_ext)
    t = t_ext[HALO:HALO + rows]
    ln = l_ext[HALO:HALO + rows]
    x = jnp.where(valid, u_ext[:, gi * POOL_GROUP_WIDTH:(gi + 1) * POOL_GROUP_WIDTH], 0.0)
    acc, k = x, 1
    while k < w // 2:
        acc = acc + pltpu.roll(acc, n - k, 0)
        k *= 2
    win = acc + pltpu.roll(acc, w // 2, 0)
    cnt = jnp.minimum(t + w // 2, ln) - jnp.maximum(t - w // 2, 0)
    cnt = jnp.maximum(cnt, 1).astype(F32)
    return win[HALO:HALO + rows] / cnt - x[HALO:HALO + rows]


def _mix_kernel(t0_t, ln_t,
                attn_ref, g_ref, h_ref, hp_ref, hn_ref,
                wu_ref, wa_ref, wp_ref, ps_ref, wb_ref, wo_ref, lg_ref, lb_ref, o_ref, *, alpha):
    b0 = pl.program_id(0) * BLOCKS_PER_TILE

    def rows_of(entry, first, count):
        r = lax.broadcasted_iota(jnp.int32, (count, POOL_GROUP_WIDTH), 0) + first
        return r + t0_t[entry], jnp.zeros((count, POOL_GROUP_WIDTH), jnp.int32) + ln_t[entry]

    parts = [rows_of(b0, BLOCK - HALO, HALO)]
    parts += [rows_of(b0 + 1 + j, 0, BLOCK) for j in range(BLOCKS_PER_TILE)]
    parts += [rows_of(b0 + 1 + BLOCKS_PER_TILE, 0, HALO)]
    t_ext = jnp.concatenate([p[0] for p in parts], axis=0)
    l_ext = jnp.concatenate([p[1] for p in parts], axis=0)

    u_halves = [_dot(h_ref[_half(hf), :].astype(BF16), wu_ref[...]) for hf in range(2)]
    h_halo = jnp.concatenate([hp_ref[...], hn_ref[...]], axis=0).astype(BF16)
    u_halo = _dot(h_halo, wu_ref[...])
    u_ext = jnp.concatenate([u_halo[0:HALO]] + u_halves + [u_halo[HALO:2 * HALO]], axis=0)

    def pool(hf):
        ext = slice(HALF_TILE * hf, HALF_TILE * (hf + 1) + 2 * HALO)
        return _pool_diff(u_ext[ext], t_ext[ext], l_ext[ext])

    def attn_branch(hf):
        return _dot(attn_ref[_half(hf), :], wa_ref[...])

    def pool_branch(diffs):
        y = jnp.concatenate([_dot(d.astype(BF16), wp_ref[gi]) for gi, d in enumerate(diffs)], axis=1)
        return _dot((y * ps_ref[...]).astype(BF16), wb_ref[...])

    def gate(hf, ya, yb):
        g = g_ref[_half(hf), :].astype(F32)
        return (g[:, 0:D_MODEL] * ya + g[:, D_MODEL:2 * D_MODEL] * yb).astype(BF16)

    def norm(rows, mixed):
        o_ref[rows, :] = _layer_norm(alpha * h_ref[rows, :] + mixed, lg_ref[...], lb_ref[...])

    ya_a = attn_branch(0)
    pool_a = pool(0)
    ya_b = attn_branch(1)
    pool_b = pool(1)
    yb_a = pool_branch(pool_a)
    yb_b = pool_branch(pool_b)
    for hf, (ya, yb) in enumerate(((ya_a, yb_a), (ya_b, yb_b))):
        gated = gate(hf, ya, yb)
        for piece in range(HALF_TILE // TAIL_ROWS):
            local = slice(TAIL_ROWS * piece, TAIL_ROWS * (piece + 1))
            norm(slice(HALF_TILE * hf + local.start, HALF_TILE * hf + local.stop),
                 _dot(gated[local], wo_ref[...]))


def _mix(attn, g, h, w_in, weights, layer, lay, alpha):
    t = h.shape[0]
    n_halo = t // HALO
    per = TOKEN_TILE // HALO
    u_block = (D_MODEL + 2 * KV_WIDTH) // POOL_WIDTH
    row = lambda i, a, b: (i, 0)
    grid_spec = pltpu.PrefetchScalarGridSpec(
        num_scalar_prefetch=2,
        grid=(t // TOKEN_TILE,),
        in_specs=[pl.BlockSpec((TOKEN_TILE, D_MODEL), row),
                  pl.BlockSpec((TOKEN_TILE, GATE_WIDTH), row),
                  pl.BlockSpec((TOKEN_TILE, D_MODEL), row),
                  pl.BlockSpec((HALO, D_MODEL), lambda i, a, b: (jnp.maximum(i * per - 1, 0), 0)),
                  pl.BlockSpec((HALO, D_MODEL), lambda i, a, b: (jnp.minimum((i + 1) * per, n_halo - 1), 0)),
                  _const_spec((None, D_MODEL, POOL_WIDTH),
                              lambda i, a, b: (_layer_of(w_in, layer), 0, u_block))]
                 + [_layer_spec(w, layer) for w in weights],
        out_specs=pl.BlockSpec((TOKEN_TILE, D_MODEL), row),
    )
    return pl.pallas_call(
        functools.partial(_mix_kernel, alpha=alpha),
        grid_spec=grid_spec,
        out_shape=jax.ShapeDtypeStruct((t, D_MODEL), F32),
        compiler_params=pltpu.CompilerParams(vmem_limit_bytes=VMEM_LIMIT),
        name="mix_ln",
    )(lay["t0"], lay["ln"], attn, g, h, h, h, w_in, *weights)


def _mlp_body(x_ref, w1_ref, b1_ref, w2_ref, b2_ref, lg_ref, lb_ref, store, alpha):
    def ff(xb, chunks):
        acc = None
        for c in chunks:
            cs = slice(c * FF_CHUNK, (c + 1) * FF_CHUNK)
            a = jnp.maximum(_dot(xb, w1_ref[:, cs]) + b1_ref[:, cs], 0.0)
            part = _dot((a * a).astype(BF16), w2_ref[cs, :])
            acc = part if acc is None else acc + part
        return acc

    def norm(x, acc):
        return _layer_norm(alpha * x + (acc + b2_ref[...]), lg_ref[...], lb_ref[...])

    n_chunks = D_FF // FF_CHUNK
    x = x_ref[...]
    xb = x.astype(BF16)
    acc = ff(xb, tuple(range(n_chunks - 1)))
    for piece in range(TOKEN_TILE // TAIL_ROWS):
        rows = slice(TAIL_ROWS * piece, TAIL_ROWS * (piece + 1))
        store(rows, norm(x[rows], acc[rows] + ff(xb[rows], (n_chunks - 1,))))


def _mlp_kernel(kind_t, dsta_t, dstb_t, *refs, alpha):
    del kind_t, dsta_t, dstb_t
    ins, o_ref = refs[:-1], refs[-1]

    def store(rows, value):
        o_ref[rows, :] = value

    _mlp_body(*ins, store, alpha)


def _mlp_final_kernel(kind_t, dsta_t, dstb_t, *refs, alpha):
    ins, (ya_ref, yb_ref, buf_ref, sem_ref) = refs[:-4], refs[-4:]
    step = pl.program_id(0)
    slot = step % 2

    def store(rows, value):
        buf_ref[slot, rows, :] = value

    _mlp_body(*ins, store, alpha)

    def copies(of_step, of_slot, act):
        for j in range(BLOCKS_PER_TILE):
            b = of_step * BLOCKS_PER_TILE + j
            for gi, (y_ref, dst_t) in enumerate(((ya_ref, dsta_t), (yb_ref, dstb_t))):
                @pl.when(kind_t[b] == gi)
                def _():
                    src = buf_ref.at[of_slot, pl.ds(BLOCK * j, BLOCK), :]
                    dst = y_ref.at[pl.ds(pl.multiple_of(dst_t[b] * BLOCK, BLOCK), BLOCK), :]
                    act(pltpu.make_async_copy(src, dst, sem_ref.at[of_slot, j]))

    copies(step, slot, lambda c: c.start())

    @pl.when(step > 0)
    def _():
        copies(step - 1, 1 - slot, lambda c: c.wait())

    @pl.when(step == pl.num_programs(0) - 1)
    def _():
        copies(step, slot, lambda c: c.wait())


def _mlp(x, weights, layer, lay, alpha, out_rows=None):
    t = x.shape[0]
    row = lambda i, *_: (i, 0)
    in_specs = [pl.BlockSpec((TOKEN_TILE, D_MODEL), row)] + [_layer_spec(w, layer) for w in weights]
    if out_rows is None:
        body, name = _mlp_kernel, "mlp_ln"
        out_specs = pl.BlockSpec((TOKEN_TILE, D_MODEL), row)
        out_shape = jax.ShapeDtypeStruct((t, D_MODEL), F32)
        scratch = []
    else:
        body, name = _mlp_final_kernel, "mlp_ln_out"
        out_specs = [pl.BlockSpec(memory_space=pl.ANY)] * 2
        out_shape = [jax.ShapeDtypeStruct((r, D_MODEL), F32) for r in out_rows]
        scratch = [pltpu.VMEM((2, TOKEN_TILE, D_MODEL), F32),
                   pltpu.SemaphoreType.DMA((2, BLOCKS_PER_TILE))]
    grid_spec = pltpu.PrefetchScalarGridSpec(
        num_scalar_prefetch=3, grid=(t // TOKEN_TILE,),
        in_specs=in_specs, out_specs=out_specs, scratch_shapes=scratch)
    return pl.pallas_call(
        functools.partial(body, alpha=alpha),
        grid_spec=grid_spec,
        out_shape=out_shape,
        compiler_params=pltpu.CompilerParams(vmem_limit_bytes=VMEM_LIMIT,
                                             dimension_semantics=("arbitrary",)),
        name=name,
    )(lay["kind"], lay["src"][0], lay["src"][1], x, *weights)


def kernel(x_prompt, x_sample, meta_tokens, ln_emb_g, ln_emb_b, w_in, sink, w_pool, pool_scale, w_bo_attn,
           w_bo_pool, w_out, ln1_g, ln1_b, w_mlp1, b_mlp1, w_mlp2, b_mlp2, ln2_g, ln2_b):
    depth = w_in.shape[0]
    alpha = float((2 * depth) ** 0.25)
    groups = (x_prompt, x_sample)
    lay = _layout([x.shape[:2] for x in groups])
    header = jnp.concatenate([jnp.zeros((META_ROW0, D_MODEL), F32), meta_tokens.astype(F32)], axis=0)

    row3 = lambda a: a.reshape(a.shape[0], 1, a.shape[-1])
    w_qvt = _qv_transposed(w_in)
    w_pool_rows = w_pool.reshape(depth, -1, POOL_GROUP_WIDTH)
    bias = jnp.asarray(_alibi_bias_table())

    out_rows = [x.shape[0] * x.shape[1] for x in groups]
    w_in_b = w_in[0:1].astype(BF16)
    for l in range(depth):
        if l == 0:
            h, qt, kk, vt, g = _inproj_embed(
                x_prompt.reshape(-1, D_MODEL), x_sample.reshape(-1, D_MODEL), header,
                ln_emb_g.reshape(1, -1), ln_emb_b.reshape(1, -1), w_in_b, w_qvt, lay)
        else:
            qt, kk, vt, g = _inproj(h, w_in_b, w_qvt, l)
        casts = [(w_bo_attn, l), (w_pool_rows, l), (w_bo_pool, l), (w_out, l), (w_mlp1, l), (w_mlp2, l)]
        if l + 1 < depth:
            casts.append((w_in, l + 1))
        attn, cast = _attention(qt, kk, vt, sink[l].astype(F32), bias, lay, casts)
        wa_b, wp_b, wb_b, wo_b, w1_b, w2_b = cast[:6]
        mix_w = (wa_b, wp_b.reshape((1,) + w_pool.shape[1:]), row3(pool_scale), wb_b, wo_b,
                 row3(ln1_g), row3(ln1_b))
        x1 = _mix(attn, g, h, w_in_b, mix_w, l, lay, alpha)
        mlp_w = (w1_b, row3(b_mlp1), w2_b, row3(b_mlp2), row3(ln2_g), row3(ln2_b))
        h = _mlp(x1, mlp_w, l, lay, alpha, out_rows if l == depth - 1 else None)
        if l + 1 < depth:
            w_in_b = cast[6]
    return tuple(y.reshape(x.shape) for y, x in zip(h, groups))
```

```python
import functools

import numpy as np
import jax
import jax.numpy as jnp
from jax import lax
from jax.experimental import pallas as pl
from jax.experimental.pallas import tpu as pltpu

D_MODEL = 1024
N_META = 16
N_HEADS = 16
N_KV_HEADS = 4
HEAD_DIM = 64
GQA_GROUP = N_HEADS // N_KV_HEADS
WINDOW = 128
BLOCK = 128
POOL_WINDOWS = (2, 4, 8, 16)
POOL_GROUP_WIDTH = 128
POOL_WIDTH = 512
KV_WIDTH = N_KV_HEADS * HEAD_DIM
GATE_WIDTH = 2 * D_MODEL
D_FF = 4 * D_MODEL
FF_CHUNK = 1024
LN_EPS = 1e-5
HALO = 8
META_ROW0 = BLOCK - N_META
BAND_KEYS = 3 * BLOCK
KEYS = BAND_KEYS + N_META
NEG = -1e30
LOG2E = 1.4426950408889634
ONES_ROWS = 16
BLOCK_FLAG_VARIANTS = (7, 6, 3, 4, 2)
TOKEN_TILE = 1024
HALF_TILE = TOKEN_TILE // 2
BLOCKS_PER_TILE = TOKEN_TILE // BLOCK
TAIL_ROWS = 256
ATTN_BLOCKS = 8
UNIT_WIDTH = 2 * BLOCK
SCORE_LOOKAHEAD = 4
CAST_MIN_ROWS = 16
VMEM_LIMIT = 56 * 1024 * 1024

F32 = jnp.float32
BF16 = jnp.bfloat16
NT_DIMS = (((1,), (1,)), ((), ()))


def _layout(group_shapes):
    seqs = []
    n = 0
    for gi, (bsz, s) in enumerate(group_shapes):
        assert s % BLOCK == 0 and s >= BLOCK
        for bi in range(bsz):
            seqs.append((gi, bi, s // BLOCK, n))
            n += 1 + s // BLOCK
    nb = -(-n // BLOCKS_PER_TILE) * BLOCKS_PER_TILE
    meta_i = np.arange(nb, dtype=np.int32)
    flags = np.full(nb, 2, dtype=np.int32)
    t0 = np.zeros(nb + 2, dtype=np.int32)
    ln = np.zeros(nb + 2, dtype=np.int32)
    kind = np.full(nb, -2, dtype=np.int32)
    src = np.zeros((len(group_shapes), nb), dtype=np.int32)
    for gi, bi, nreal, b0 in seqs:
        for j in range(nreal + 1):
            b = b0 + j
            meta_i[b] = b0
            t0[b + 1] = -META_ROW0 + BLOCK * j
            ln[b + 1] = N_META + BLOCK * nreal
            flags[b] = (0 if j == 0 else 2) | (1 if j >= 2 else 0) | (4 if j < nreal else 0)
            kind[b] = -1 if j == 0 else gi
            if j > 0:
                src[gi, b:] = bi * nreal + (j - 1)
    variant = np.asarray([BLOCK_FLAG_VARIANTS.index(int(f)) for f in flags], dtype=np.int32)
    return dict(nb=nb, meta=meta_i, variant=variant, t0=t0, ln=ln, kind=kind, src=src)


def _alibi_bias_table():
    nv = len(BLOCK_FLAG_VARIANTS)
    slopes = 2.0 ** (-8.0 * np.arange(1, N_HEADS + 1) / N_HEADS)
    key = np.arange(BAND_KEYS)[:, None]
    qry = np.arange(BLOCK)[None, :]
    rel = np.abs(key - BLOCK - qry).astype(np.float32)[None, :, None, :]
    scaled = (slopes * LOG2E).astype(np.float32).reshape(N_KV_HEADS, 1, GQA_GROUP, 1)
    band = np.where(rel <= WINDOW, -(scaled * rel), np.float32(NEG))
    flags = np.asarray(BLOCK_FLAG_VARIANTS)[:, None]
    usable = (flags >> (np.arange(BAND_KEYS)[None, :] // BLOCK)) & 1
    band = np.where(usable[:, None, :, None, None] != 0, band[None], np.float32(NEG))
    band = band.reshape(nv, N_KV_HEADS, BAND_KEYS, GQA_GROUP * BLOCK)
    meta = np.zeros((nv, N_KV_HEADS, N_META, GQA_GROUP * BLOCK), np.float32)
    return np.concatenate([band, meta], axis=2).astype(np.float32)


def _const_spec(shape, index_map):
    return pl.BlockSpec(shape, index_map, pipeline_mode=pl.Buffered(1))


def _layer_of(stacked, layer):
    return layer if stacked.shape[0] > 1 else 0


def _layer_spec(stacked, layer):
    rest = stacked.shape[1:]
    index = (_layer_of(stacked, layer),) + (0,) * len(rest)
    return _const_spec((None,) + rest, lambda *_: index)


def _layer_norm(x, g, b):
    mu = jnp.mean(x, axis=-1, keepdims=True)
    xc = x - mu
    var = jnp.mean(xc * xc, axis=-1, keepdims=True)
    return xc * lax.rsqrt(var + LN_EPS) * g + b


def _dot(a, b):
    return jnp.dot(a, b, preferred_element_type=F32)


def _half(hf):
    return slice(HALF_TILE * hf, HALF_TILE * (hf + 1))


def _qvt_kernel(w_ref, o_ref):
    o_ref[...] = w_ref[...].T.astype(BF16)


def _qv_transposed(w_in):
    depth = w_in.shape[0]
    q_blocks = D_MODEL // KV_WIDTH
    v_block0 = (D_MODEL + KV_WIDTH) // KV_WIDTH
    n_blocks = q_blocks + 1
    src_col = lambda c: jnp.where(c < q_blocks, c, c - q_blocks + v_block0)
    return pl.pallas_call(
        _qvt_kernel,
        grid=(depth, n_blocks),
        in_specs=[pl.BlockSpec((None, D_MODEL, KV_WIDTH), lambda l, c: (l, 0, src_col(c)))],
        out_specs=pl.BlockSpec((None, KV_WIDTH, D_MODEL), lambda l, c: (l, c, 0)),
        out_shape=jax.ShapeDtypeStruct((depth, D_MODEL + KV_WIDTH, D_MODEL), BF16),
        name="qv_transpose",
    )(w_in)


def _project(hf, h_ref, w_ref, wt_ref, qt_ref, k_ref, vt_ref, g_ref):
    k0 = D_MODEL
    g0 = D_MODEL + 2 * KV_WIDTH + POOL_WIDTH
    rows = _half(hf)
    x = h_ref[rows, :].astype(BF16)
    gates = _dot(x, w_ref[:, g0:g0 + GATE_WIDTH])
    g_ref[rows, :] = (0.5 * jnp.tanh(0.5 * gates) + 0.5).astype(BF16)
    qt = lax.dot_general(wt_ref[0:D_MODEL, :], x, NT_DIMS, preferred_element_type=F32)
    qt_ref[:, rows] = (qt * (HEAD_DIM ** -0.5 * LOG2E)).astype(BF16)
    vt = lax.dot_general(wt_ref[D_MODEL:D_MODEL + KV_WIDTH, :], x, NT_DIMS, preferred_element_type=F32)
    vt_ref[:, rows] = vt.astype(BF16)
    k_ref[rows, :] = _dot(x, w_ref[:, k0:k0 + KV_WIDTH]).astype(BF16)


def _inproj_kernel(h_ref, *refs):
    _project(0, h_ref, *refs)
    _project(1, h_ref, *refs)


def _inproj_embed_kernel(kind_t, srca_t, srcb_t, *refs):
    del srca_t, srcb_t
    xa_refs, xb_refs = refs[:BLOCKS_PER_TILE], refs[BLOCKS_PER_TILE:2 * BLOCKS_PER_TILE]
    hdr_ref, eg_ref, eb_ref = refs[2 * BLOCKS_PER_TILE:2 * BLOCKS_PER_TILE + 3]
    w_ref, wt_ref, h_ref, qt_ref, k_ref, vt_ref, g_ref = refs[2 * BLOCKS_PER_TILE + 3:]
    for hf in range(2):
        for j in range(hf * BLOCKS_PER_TILE // 2, (hf + 1) * BLOCKS_PER_TILE // 2):
            kind = kind_t[pl.program_id(0) * BLOCKS_PER_TILE + j]
            x = jnp.where(kind == 0, xa_refs[j][...], jnp.where(kind == 1, xb_refs[j][...], 0.0))
            x = jnp.where(kind == -1, hdr_ref[...], x)
            h_ref[BLOCK * j:BLOCK * (j + 1), :] = _layer_norm(x, eg_ref[...], eb_ref[...])
        _project(hf, h_ref, w_ref, wt_ref, qt_ref, k_ref, vt_ref, g_ref)


def _inproj_out(t, row, col):
    specs = [pl.BlockSpec((D_MODEL, TOKEN_TILE), col),
             pl.BlockSpec((TOKEN_TILE, KV_WIDTH), row),
             pl.BlockSpec((KV_WIDTH, TOKEN_TILE), col),
             pl.BlockSpec((TOKEN_TILE, GATE_WIDTH), row)]
    shapes = [jax.ShapeDtypeStruct((D_MODEL, t), BF16),
              jax.ShapeDtypeStruct((t, KV_WIDTH), BF16),
              jax.ShapeDtypeStruct((KV_WIDTH, t), BF16),
              jax.ShapeDtypeStruct((t, GATE_WIDTH), BF16)]
    return specs, shapes


def _inproj_embed(xa, xb, header, eg, eb, w_in, w_qvt, lay):
    t = lay["nb"] * BLOCK
    n = BLOCKS_PER_TILE
    blk = lambda sel: pl.BlockSpec((BLOCK, D_MODEL), sel)
    from_a = [blk(lambda i, k, sa, sb, j=j: (sa[i * n + j], 0)) for j in range(n)]
    from_b = [blk(lambda i, k, sa, sb, j=j: (sb[i * n + j], 0)) for j in range(n)]
    const = lambda i, *_: (0, 0)
    row = lambda i, *_: (i, 0)
    col = lambda i, *_: (0, i)
    out_specs, out_shape = _inproj_out(t, row, col)
    grid_spec = pltpu.PrefetchScalarGridSpec(
        num_scalar_prefetch=3,
        grid=(t // TOKEN_TILE,),
        in_specs=from_a + from_b + [blk(const), pl.BlockSpec((1, D_MODEL), const),
                                    pl.BlockSpec((1, D_MODEL), const),
                                    _layer_spec(w_in, 0), _layer_spec(w_qvt, 0)],
        out_specs=[pl.BlockSpec((TOKEN_TILE, D_MODEL), row)] + out_specs,
    )
    return pl.pallas_call(
        _inproj_embed_kernel,
        grid_spec=grid_spec,
        out_shape=[jax.ShapeDtypeStruct((t, D_MODEL), F32)] + out_shape,
        compiler_params=pltpu.CompilerParams(vmem_limit_bytes=VMEM_LIMIT),
        name="embed_in_proj",
    )(lay["kind"], lay["src"][0], lay["src"][1], *([xa] * n), *([xb] * n), header, eg, eb, w_in, w_qvt)


def _inproj(h, w_in, w_qvt, layer):
    t = h.shape[0]
    row = lambda i: (i, 0)
    col = lambda i: (0, i)
    out_specs, out_shape = _inproj_out(t, row, col)
    return pl.pallas_call(
        _inproj_kernel,
        grid=(t // TOKEN_TILE,),
        in_specs=[pl.BlockSpec((TOKEN_TILE, D_MODEL), row),
                  _layer_spec(w_in, layer), _layer_spec(w_qvt, layer)],
        out_specs=out_specs,
        out_shape=out_shape,
        compiler_params=pltpu.CompilerParams(vmem_limit_bytes=VMEM_LIMIT),
        name="in_proj",
    )(h, w_in, w_qvt)


def _attn_kernel(meta_t, var_t, sink_ref, qt_ref, *refs, n_cast):
    del meta_t
    kp_ref, ko_ref, kn_ref = refs[0:3]
    kh_refs = refs[3:3 + ATTN_BLOCKS]
    vp_ref, vo_ref, vn_ref = refs[3 + ATTN_BLOCKS:6 + ATTN_BLOCKS]
    vh_refs = refs[6 + ATTN_BLOCKS:6 + 2 * ATTN_BLOCKS]
    n_in = 7 + 2 * ATTN_BLOCKS + n_cast
    bias_ref, o_ref = refs[6 + 2 * ATTN_BLOCKS], refs[n_in]
    for src_ref, dst_ref in zip(refs[n_in - n_cast:n_in], refs[n_in + 1:]):
        dst_ref[...] = src_ref[...].astype(BF16)
    step = pl.program_id(0)
    zero_q = jnp.zeros((HEAD_DIM, UNIT_WIDTH), BF16)
    zero_p = jnp.zeros((META_ROW0, UNIT_WIDTH), BF16)
    ones_band = jnp.ones((ONES_ROWS, BAND_KEYS), BF16)
    ones_meta = jnp.ones((ONES_ROWS, BLOCK), BF16)

    def window(blk, before, own, after, axis):
        parts = []
        for j in range(blk - 1, blk + 2):
            sel = [slice(None), slice(None)]
            if j < 0:
                parts.append(before)
            elif j >= ATTN_BLOCKS:
                parts.append(after)
            else:
                sel[axis] = slice(BLOCK * j, BLOCK * (j + 1))
                parts.append(own[tuple(sel)])
        return parts

    def scores(blk, k, hp):
        pair = slice(BLOCK * (k // 2), BLOCK * (k // 2 + 1))
        band = window(blk, kp_ref[:, pair], ko_ref[:, pair], kn_ref[:, pair], 0)
        kc = jnp.concatenate(band + [kh_refs[blk][META_ROW0:BLOCK, pair]], axis=0)
        q0 = GQA_GROUP * HEAD_DIM * k
        qs = slice(BLOCK * blk, BLOCK * (blk + 1))
        qt2 = jnp.concatenate([qt_ref[q0 + HEAD_DIM * j:q0 + HEAD_DIM * (j + 1), qs]
                               for j in range(2 * hp, 2 * hp + 2)], axis=1)
        rhs = jnp.concatenate([qt2, zero_q] if k % 2 == 0 else [zero_q, qt2], axis=0)
        cols = slice(UNIT_WIDTH * hp, UNIT_WIDTH * (hp + 1))
        return _dot(kc, rhs) + bias_ref[var_t[ATTN_BLOCKS * step + blk], k, :, cols]

    def finish(blk, k, hp, s):
        sink_row = jnp.concatenate([jnp.full((1, BLOCK), sink_ref[GQA_GROUP * k + j] * LOG2E, F32)
                                    for j in range(2 * hp, 2 * hp + 2)], axis=1)
        m = jnp.maximum(jnp.max(s, axis=0, keepdims=True), sink_row)
        pb = jnp.exp2(s - m).astype(BF16)
        vs = slice(HEAD_DIM * k, HEAD_DIM * (k + 1))
        band = window(blk, vp_ref[vs, :], vo_ref[vs, :], vn_ref[vs, :], 1)
        vc = jnp.concatenate([jnp.concatenate(band, axis=1), ones_band], axis=0)
        vh = jnp.concatenate([vh_refs[blk][vs, :], ones_meta], axis=0)
        p_meta = jnp.concatenate([zero_p, pb[BAND_KEYS:KEYS]], axis=0)
        acc = _dot(vc, pb[0:BAND_KEYS]) + _dot(vh, p_meta)
        denom = acc[HEAD_DIM:HEAD_DIM + 1] + jnp.exp2(sink_row - m)
        ot = acc[0:HEAD_DIM] * (1.0 / denom)
        rows = slice(BLOCK * blk, BLOCK * (blk + 1))
        two = jnp.concatenate([ot[:, 0:BLOCK], ot[:, BLOCK:2 * BLOCK]], axis=0)
        c0 = GQA_GROUP * HEAD_DIM * k + BLOCK * hp
        o_ref[rows, c0:c0 + BLOCK] = two.T.astype(BF16)

    units = [(blk, k, hp) for blk in range(ATTN_BLOCKS) for k in range(N_KV_HEADS)
             for hp in range(GQA_GROUP // 2)]
    pending = [scores(*u) for u in units[:SCORE_LOOKAHEAD]]
    for idx, unit in enumerate(units):
        if idx + SCORE_LOOKAHEAD < len(units):
            pending.append(scores(*units[idx + SCORE_LOOKAHEAD]))
        finish(*unit, pending.pop(0))


def _attention(qt, kk, vt, sink, bias, lay, casts):
    t = kk.shape[0]
    nb = lay["nb"]
    n = ATTN_BLOCKS
    steps = nb // n
    cast_in, cast_out, cast_shape = [], [], []
    for w, layer in casts:
        rows, cols = w.shape[1:]
        chunks = min(1 << (steps.bit_length() - 1), rows // CAST_MIN_ROWS)
        assert rows % chunks == 0 and (rows // chunks) % CAST_MIN_ROWS == 0
        chunk = lambda i, m, v, c=chunks: jnp.minimum(i, c - 1)
        cast_in.append(pl.BlockSpec((None, rows // chunks, cols),
                                    lambda i, m, v, chunk=chunk, layer=layer: (layer, chunk(i, m, v), 0)))
        cast_out.append(pl.BlockSpec((None, rows // chunks, cols),
                                     lambda i, m, v, chunk=chunk: (0, chunk(i, m, v), 0)))
        cast_shape.append(jax.ShapeDtypeStruct((1, rows, cols), BF16))
    kspec = lambda rows, sel: pl.BlockSpec((rows, KV_WIDTH), sel)
    vspec = lambda cols, sel: pl.BlockSpec((KV_WIDTH, cols), sel)
    before = lambda i: jnp.maximum(n * i - 1, 0)
    after = lambda i: jnp.minimum(n * i + n, nb - 1)
    k_specs = [kspec(BLOCK, lambda i, m, v: (before(i), 0)),
               kspec(n * BLOCK, lambda i, m, v: (i, 0)),
               kspec(BLOCK, lambda i, m, v: (after(i), 0))]
    k_specs += [kspec(BLOCK, lambda i, m, v, j=j: (m[n * i + j], 0)) for j in range(n)]
    v_specs = [vspec(BLOCK, lambda i, m, v: (0, before(i))),
               vspec(n * BLOCK, lambda i, m, v: (0, i)),
               vspec(BLOCK, lambda i, m, v: (0, after(i)))]
    v_specs += [vspec(BLOCK, lambda i, m, v, j=j: (0, m[n * i + j])) for j in range(n)]
    grid_spec = pltpu.PrefetchScalarGridSpec(
        num_scalar_prefetch=2,
        grid=(nb // n,),
        in_specs=[pl.BlockSpec(memory_space=pltpu.SMEM),
                  pl.BlockSpec((D_MODEL, n * BLOCK), lambda i, m, v: (0, i))] + k_specs + v_specs
                 + [_const_spec((len(BLOCK_FLAG_VARIANTS), N_KV_HEADS, KEYS, GQA_GROUP * BLOCK),
                                lambda i, m, v: (0, 0, 0, 0))] + cast_in,
        out_specs=[pl.BlockSpec((n * BLOCK, D_MODEL), lambda i, m, v: (i, 0))] + cast_out,
    )
    outs = pl.pallas_call(
        functools.partial(_attn_kernel, n_cast=len(casts)),
        grid_spec=grid_spec,
        out_shape=[jax.ShapeDtypeStruct((t, D_MODEL), BF16)] + cast_shape,
        compiler_params=pltpu.CompilerParams(vmem_limit_bytes=VMEM_LIMIT,
                                             dimension_semantics=("arbitrary",)),
        name="band_attn",
    )(lay["meta"], lay["variant"], sink, qt, *([kk] * (3 + n)), *([vt] * (3 + n)), bias,
      *[w for w, _ in casts])
    return outs[0], outs[1:]


def _pool_diff(u_ext, t_ext, l_ext):
    return [_pool_group(u_ext, t_ext, l_ext, gi) for gi in range(len(POOL_WINDOWS))]


def _pool_group(u_ext, t_ext, l_ext, gi):
    w = POOL_WINDOWS[gi]
    n = u_ext.shape[0]
    rows = n - 2 * HALO
    valid = (t_ext >= 0) & (t_ext < l_ext)
    t = t_ext[HALO:HALO + rows]
    ln = l_ext[HALO:HALO + rows]
    x = jnp.where(valid, u_ext[:, gi * POOL_GROUP_WIDTH:(gi + 1) * POOL_GROUP_WIDTH], 0.0)
    acc, k = x, 1
    while k < w // 2:
        acc = acc + pltpu.roll(acc, n - k, 0)
        k *= 2
    win = acc + pltpu.roll(acc, w // 2, 0)
    cnt = jnp.minimum(t + w // 2, ln) - jnp.maximum(t - w // 2, 0)
    cnt = jnp.maximum(cnt, 1).astype(F32)
    return win[HALO:HALO + rows] / cnt - x[HALO:HALO + rows]


def _mix_kernel(t0_t, ln_t,
                attn_ref, g_ref, h_ref, hp_ref, hn_ref,
                wu_ref, wa_ref, wp_ref, ps_ref, wb_ref, wo_ref, lg_ref, lb_ref, o_ref, *, alpha):
    b0 = pl.program_id(0) * BLOCKS_PER_TILE

    def rows_of(entry, first, count):
        r = lax.broadcasted_iota(jnp.int32, (count, POOL_GROUP_WIDTH), 0) + first
        return r + t0_t[entry], jnp.zeros((count, POOL_GROUP_WIDTH), jnp.int32) + ln_t[entry]

    parts = [rows_of(b0, BLOCK - HALO, HALO)]
    parts += [rows_of(b0 + 1 + j, 0, BLOCK) for j in range(BLOCKS_PER_TILE)]
    parts += [rows_of(b0 + 1 + BLOCKS_PER_TILE, 0, HALO)]
    t_ext = jnp.concatenate([p[0] for p in parts], axis=0)
    l_ext = jnp.concatenate([p[1] for p in parts], axis=0)

    u_halves = [_dot(h_ref[_half(hf), :].astype(BF16), wu_ref[...]) for hf in range(2)]
    h_halo = jnp.concatenate([hp_ref[...], hn_ref[...]], axis=0).astype(BF16)
    u_halo = _dot(h_halo, wu_ref[...])
    u_ext = jnp.concatenate([u_halo[0:HALO]] + u_halves + [u_halo[HALO:2 * HALO]], axis=0)

    def pool(hf):
        ext = slice(HALF_TILE * hf, HALF_TILE * (hf + 1) + 2 * HALO)
        return _pool_diff(u_ext[ext], t_ext[ext], l_ext[ext])

    def attn_branch(hf):
        return _dot(attn_ref[_half(hf), :], wa_ref[...])

    def pool_branch(diffs):
        y = jnp.concatenate([_dot(d.astype(BF16), wp_ref[gi]) for gi, d in enumerate(diffs)], axis=1)
        return _dot((y * ps_ref[...]).astype(BF16), wb_ref[...])

    def gate(hf, ya, yb):
        g = g_ref[_half(hf), :].astype(F32)
        return (g[:, 0:D_MODEL] * ya + g[:, D_MODEL:2 * D_MODEL] * yb).astype(BF16)

    def norm(rows, mixed):
        o_ref[rows, :] = _layer_norm(alpha * h_ref[rows, :] + mixed, lg_ref[...], lb_ref[...])

    ya_a = attn_branch(0)
    pool_a = pool(0)
    ya_b = attn_branch(1)
    pool_b = pool(1)
    yb_a = pool_branch(pool_a)
    yb_b = pool_branch(pool_b)
    for hf, (ya, yb) in enumerate(((ya_a, yb_a), (ya_b, yb_b))):
        gated = gate(hf, ya, yb)
        for piece in range(HALF_TILE // TAIL_ROWS):
            local = slice(TAIL_ROWS * piece, TAIL_ROWS * (piece + 1))
            norm(slice(HALF_TILE * hf + local.start, HALF_TILE * hf + local.stop),
                 _dot(gated[local], wo_ref[...]))


def _mix(attn, g, h, w_in, weights, layer, lay, alpha):
    t = h.shape[0]
    n_halo = t // HALO
    per = TOKEN_TILE // HALO
    u_block = (D_MODEL + 2 * KV_WIDTH) // POOL_WIDTH
    row = lambda i, a, b: (i, 0)
    grid_spec = pltpu.PrefetchScalarGridSpec(
        num_scalar_prefetch=2,
        grid=(t // TOKEN_TILE,),
        in_specs=[pl.BlockSpec((TOKEN_TILE, D_MODEL), row),
                  pl.BlockSpec((TOKEN_TILE, GATE_WIDTH), row),
                  pl.BlockSpec((TOKEN_TILE, D_MODEL), row),
                  pl.BlockSpec((HALO, D_MODEL), lambda i, a, b: (jnp.maximum(i * per - 1, 0), 0)),
                  pl.BlockSpec((HALO, D_MODEL), lambda i, a, b: (jnp.minimum((i + 1) * per, n_halo - 1), 0)),
                  _const_spec((None, D_MODEL, POOL_WIDTH),
                              lambda i, a, b: (_layer_of(w_in, layer), 0, u_block))]
                 + [_layer_spec(w, layer) for w in weights],
        out_specs=pl.BlockSpec((TOKEN_TILE, D_MODEL), row),
    )
    return pl.pallas_call(
        functools.partial(_mix_kernel, alpha=alpha),
        grid_spec=grid_spec,
        out_shape=jax.ShapeDtypeStruct((t, D_MODEL), F32),
        compiler_params=pltpu.CompilerParams(vmem_limit_bytes=VMEM_LIMIT),
        name="mix_ln",
    )(lay["t0"], lay["ln"], attn, g, h, h, h, w_in, *weights)


def _mlp_body(x_ref, w1_ref, b1_ref, w2_ref, b2_ref, lg_ref, lb_ref, store, alpha):
    def ff(xb, chunks):
        acc = None
        for c in chunks:
            cs = slice(c * FF_CHUNK, (c + 1) * FF_CHUNK)
            a = jnp.maximum(_dot(xb, w1_ref[:, cs]) + b1_ref[:, cs], 0.0)
            part = _dot((a * a).astype(BF16), w2_ref[cs, :])
            acc = part if acc is None else acc + part
        return acc

    def norm(x, acc):
        return _layer_norm(alpha * x + (acc + b2_ref[...]), lg_ref[...], lb_ref[...])

    n_chunks = D_FF // FF_CHUNK
    x = x_ref[...]
    xb = x.astype(BF16)
    acc = ff(xb, tuple(range(n_chunks - 1)))
    for piece in range(TOKEN_TILE // TAIL_ROWS):
        rows = slice(TAIL_ROWS * piece, TAIL_ROWS * (piece + 1))
        store(rows, norm(x[rows], acc[rows] + ff(xb[rows], (n_chunks - 1,))))


def _mlp_kernel(kind_t, dsta_t, dstb_t, *refs, alpha):
    del kind_t, dsta_t, dstb_t
    ins, o_ref = refs[:-1], refs[-1]

    def store(rows, value):
        o_ref[rows, :] = value

    _mlp_body(*ins, store, alpha)


def _mlp_final_kernel(kind_t, dsta_t, dstb_t, *refs, alpha):
    ins, (ya_ref, yb_ref, buf_ref, sem_ref) = refs[:-4], refs[-4:]
    step = pl.program_id(0)
    slot = step % 2

    def store(rows, value):
        buf_ref[slot, rows, :] = value

    _mlp_body(*ins, store, alpha)

    def copies(of_step, of_slot, act):
        for j in range(BLOCKS_PER_TILE):
            b = of_step * BLOCKS_PER_TILE + j
            for gi, (y_ref, dst_t) in enumerate(((ya_ref, dsta_t), (yb_ref, dstb_t))):
                @pl.when(kind_t[b] == gi)
                def _():
                    src = buf_ref.at[of_slot, pl.ds(BLOCK * j, BLOCK), :]
                    dst = y_ref.at[pl.ds(pl.multiple_of(dst_t[b] * BLOCK, BLOCK), BLOCK), :]
                    act(pltpu.make_async_copy(src, dst, sem_ref.at[of_slot, j]), j)

    copies(step, slot, lambda c, j: c.start(priority=j % 2))

    @pl.when(step > 0)
    def _():
        copies(step - 1, 1 - slot, lambda c, j: c.wait())

    @pl.when(step == pl.num_programs(0) - 1)
    def _():
        copies(step, slot, lambda c, j: c.wait())


def _mlp(x, weights, layer, lay, alpha, out_rows=None):
    t = x.shape[0]
    row = lambda i, *_: (i, 0)
    in_specs = [pl.BlockSpec((TOKEN_TILE, D_MODEL), row)] + [_layer_spec(w, layer) for w in weights]
    if out_rows is None:
        body, name = _mlp_kernel, "mlp_ln"
        out_specs = pl.BlockSpec((TOKEN_TILE, D_MODEL), row)
        out_shape = jax.ShapeDtypeStruct((t, D_MODEL), F32)
        scratch = []
    else:
        body, name = _mlp_final_kernel, "mlp_ln_out"
        out_specs = [pl.BlockSpec(memory_space=pl.ANY)] * 2
        out_shape = [jax.ShapeDtypeStruct((r, D_MODEL), F32) for r in out_rows]
        scratch = [pltpu.VMEM((2, TOKEN_TILE, D_MODEL), F32),
                   pltpu.SemaphoreType.DMA((2, BLOCKS_PER_TILE))]
    grid_spec = pltpu.PrefetchScalarGridSpec(
        num_scalar_prefetch=3, grid=(t // TOKEN_TILE,),
        in_specs=in_specs, out_specs=out_specs, scratch_shapes=scratch)
    return pl.pallas_call(
        functools.partial(body, alpha=alpha),
        grid_spec=grid_spec,
        out_shape=out_shape,
        compiler_params=pltpu.CompilerParams(vmem_limit_bytes=VMEM_LIMIT,
                                             dimension_semantics=("arbitrary",)),
        name=name,
    )(lay["kind"], lay["src"][0], lay["src"][1], x, *weights)


def kernel(x_prompt, x_sample, meta_tokens, ln_emb_g, ln_emb_b, w_in, sink, w_pool, pool_scale, w_bo_attn,
           w_bo_pool, w_out, ln1_g, ln1_b, w_mlp1, b_mlp1, w_mlp2, b_mlp2, ln2_g, ln2_b):
    depth = w_in.shape[0]
    alpha = float((2 * depth) ** 0.25)
    groups = (x_prompt, x_sample)
    lay = _layout([x.shape[:2] for x in groups])
    header = jnp.concatenate([jnp.zeros((META_ROW0, D_MODEL), F32), meta_tokens.astype(F32)], axis=0)

    row3 = lambda a: a.reshape(a.shape[0], 1, a.shape[-1])
    w_qvt = _qv_transposed(w_in)
    w_pool_rows = w_pool.reshape(depth, -1, POOL_GROUP_WIDTH)
    bias = jnp.asarray(_alibi_bias_table())

    out_rows = [x.shape[0] * x.shape[1] for x in groups]
    w_in_b = w_in[0:1].astype(BF16)
    for l in range(depth):
        if l == 0:
            h, qt, kk, vt, g = _inproj_embed(
                x_prompt.reshape(-1, D_MODEL), x_sample.reshape(-1, D_MODEL), header,
                ln_emb_g.reshape(1, -1), ln_emb_b.reshape(1, -1), w_in_b, w_qvt, lay)
        else:
            qt, kk, vt, g = _inproj(h, w_in_b, w_qvt, l)
        casts = [(w_bo_attn, l), (w_pool_rows, l), (w_bo_pool, l), (w_out, l), (w_mlp1, l), (w_mlp2, l)]
        if l + 1 < depth:
            casts.append((w_in, l + 1))
        attn, cast = _attention(qt, kk, vt, sink[l].astype(F32), bias, lay, casts)
        wa_b, wp_b, wb_b, wo_b, w1_b, w2_b = cast[:6]
        mix_w = (wa_b, wp_b.reshape((1,) + w_pool.shape[1:]), row3(pool_scale), wb_b, wo_b,
                 row3(ln1_g), row3(ln1_b))
        x1 = _mix(attn, g, h, w_in_b, mix_w, l, lay, alpha)
        mlp_w = (w1_b, row3(b_mlp1), w2_b, row3(b_mlp2), row3(ln2_g), row3(ln2_b))
        h = _mlp(x1, mlp_w, l, lay, alpha, out_rows if l == depth - 1 else None)
        if l + 1 < depth:
            w_in_b = cast[6]
    return tuple(y.reshape(x.shape) for y, x in zip(h, groups))
```
